```python
import jax
import jax.numpy as jnp
from jax import lax
import numpy as np

D_MODEL = 2048
BATCH = 4
SEQ = 4096
DEPTH = 2

HEAD_DIM = 128
MOBA_HEADS = D_MODEL // (2 * HEAD_DIM)
FOX_HEADS = D_MODEL // (2 * HEAD_DIM)
MOBA_WIDTH = MOBA_HEADS * HEAD_DIM
FOX_WIDTH = FOX_HEADS * HEAD_DIM
MIX_WIDTH = MOBA_WIDTH + FOX_WIDTH
IN_WIDTH = 3 * MOBA_WIDTH + 3 * FOX_WIDTH + FOX_HEADS
MOBA_BLOCK = 256
MOBA_TOPK = 3
MOBA_Q_CHUNK = 32
FOX_Q_CHUNK = 128
ROPE_THETA = 10000.0
D_FF = (7 * D_MODEL) // 2
N_EXPERTS = 8
TOP_K = 2
EPS = 1e-6
F32 = jnp.float32

kernel_name = 'hybrid_moba_fox_adaln_moe_block'


def rmsnorm(x, g):
    x32 = x.astype(F32)
    y = x32 * lax.rsqrt(jnp.mean(x32 * x32, axis=-1, keepdims=True) + EPS)
    return (y * g.astype(F32)).astype(x.dtype)


def rope_tables(seq_len):
    inv = ROPE_THETA ** (-jnp.arange(0, HEAD_DIM, 2, dtype=F32) / HEAD_DIM)
    ang = jnp.arange(seq_len, dtype=F32)[:, None] * inv[None, :]
    return jnp.cos(ang), jnp.sin(ang)


def apply_rope(t, cos, sin):
    t32 = t.astype(F32)
    t1, t2 = jnp.split(t32, 2, axis=-1)
    return jnp.concatenate([t1 * cos - t2 * sin, t2 * cos + t1 * sin], axis=-1).astype(t.dtype)


def to_heads(t, n_heads):
    b, s, _ = t.shape
    return t.reshape(b, s, n_heads, HEAD_DIM).transpose(0, 2, 1, 3)


def from_heads(t):
    b, h, s, d = t.shape
    return t.transpose(0, 2, 1, 3).reshape(b, s, h * d)


def moba_attention(q, k, v):
    b, h, s, dh = q.shape
    nb = -(-s // MOBA_BLOCK)
    pad = nb * MOBA_BLOCK - s
    kp = jnp.pad(k, ((0, 0), (0, 0), (0, pad), (0, 0)))
    vp = jnp.pad(v, ((0, 0), (0, 0), (0, pad), (0, 0)))
    kb = kp.reshape(b, h, nb, MOBA_BLOCK, dh)
    vb = vp.reshape(b, h, nb, MOBA_BLOCK, dh)
    k_mean = jnp.mean(kb.astype(F32), axis=3)
    gate = jnp.einsum('bhsd,bhnd->bhsn', q.astype(F32), k_mean)
    qblk = jnp.arange(s) // MOBA_BLOCK
    past = jnp.arange(nb)[None, :] < qblk[:, None]
    gate = jnp.where(past, gate, -jnp.inf)
    kk = min(MOBA_TOPK, nb)
    _, sel = lax.top_k(gate, kk)
    valid = sel < qblk[:, None]
    n_chunks = s // MOBA_Q_CHUNK
    q_ch = q.reshape(b, h, n_chunks, MOBA_Q_CHUNK, dh).transpose(2, 0, 1, 3, 4)
    sel_ch = sel.reshape(b, h, n_chunks, MOBA_Q_CHUNK, kk).transpose(2, 0, 1, 3, 4)
    val_ch = valid.reshape(b, h, n_chunks, MOBA_Q_CHUNK, kk).transpose(2, 0, 1, 3, 4)
    scale = dh ** -0.5
    b_idx = jnp.arange(b)[:, None, None, None]
    h_idx = jnp.arange(h)[None, :, None, None]

    def chunk_fn(args):
        ci, q_c, sel_c, val_c = args
        start = ci * MOBA_Q_CHUNK
        blk = start // MOBA_BLOCK
        k_sel = kb[b_idx, h_idx, sel_c]
        v_sel = vb[b_idx, h_idx, sel_c]
        s_sel = jnp.einsum('bhqd,bhqjnd->bhqjn', q_c, k_sel, preferred_element_type=F32) * scale
        s_sel = jnp.where(val_c[..., None], s_sel, -jnp.inf).reshape(b, h, MOBA_Q_CHUNK, kk * MOBA_BLOCK)
        k_own = lax.dynamic_slice_in_dim(kp, blk * MOBA_BLOCK, MOBA_BLOCK, axis=2)
        v_own = lax.dynamic_slice_in_dim(vp, blk * MOBA_BLOCK, MOBA_BLOCK, axis=2)
        s_own = jnp.einsum('bhqd,bhnd->bhqn', q_c, k_own, preferred_element_type=F32) * scale
        qpos = start + jnp.arange(MOBA_Q_CHUNK)
        kpos = blk * MOBA_BLOCK + jnp.arange(MOBA_BLOCK)
        s_own = jnp.where(kpos[None, :] <= qpos[:, None], s_own, -jnp.inf)
        p = jax.nn.softmax(jnp.concatenate([s_sel, s_own], axis=-1), axis=-1)
        p_sel = p[..., :kk * MOBA_BLOCK].reshape(b, h, MOBA_Q_CHUNK, kk, MOBA_BLOCK).astype(v.dtype)
        p_own = p[..., kk * MOBA_BLOCK:].astype(v.dtype)
        return (jnp.einsum('bhqjn,bhqjnd->bhqd', p_sel, v_sel)
                + jnp.einsum('bhqn,bhnd->bhqd', p_own, v_own))

    out = lax.map(chunk_fn, (jnp.arange(n_chunks), q_ch, sel_ch, val_ch))
    return out.transpose(1, 2, 0, 3, 4).reshape(b, h, s, dh)


def forgetting_attention(q, k, v, log_f):
    b, h, s, dh = q.shape
    cum = lax.cumsum(log_f, axis=2)
    n_chunks = s // FOX_Q_CHUNK
    q_ch = q.reshape(b, h, n_chunks, FOX_Q_CHUNK, dh).transpose(2, 0, 1, 3, 4)
    c_ch = cum.reshape(b, h, n_chunks, FOX_Q_CHUNK).transpose(2, 0, 1, 3)
    kpos = jnp.arange(s)
    scale = dh ** -0.5

    def chunk_fn(args):
        ci, q_c, c_c = args
        sc = jnp.einsum('bhqd,bhkd->bhqk', q_c, k, preferred_element_type=F32) * scale
        sc = sc + c_c[..., None] - cum[:, :, None, :]
        qpos = ci * FOX_Q_CHUNK + jnp.arange(FOX_Q_CHUNK)
        sc = jnp.where(kpos[None, :] <= qpos[:, None], sc, -jnp.inf)
        p = jax.nn.softmax(sc, axis=-1).astype(v.dtype)
        return jnp.einsum('bhqk,bhkd->bhqd', p, v)

    out = lax.map(chunk_fn, (jnp.arange(n_chunks), q_ch, c_ch))
    return out.transpose(1, 2, 0, 3, 4).reshape(b, h, s, dh)


def hybrid_mixer(h, w_in, forget_b, moba_out_g, fox_out_g, w_o, cos, sin):
    proj = jnp.einsum('bsd,de->bse', h, w_in)
    cuts = [MOBA_WIDTH, 2 * MOBA_WIDTH, 3 * MOBA_WIDTH,
            3 * MOBA_WIDTH + FOX_WIDTH, 3 * MOBA_WIDTH + 2 * FOX_WIDTH, 3 * MOBA_WIDTH + 3 * FOX_WIDTH]
    mq, mk, mv, fq, fk, fv, fz = jnp.split(proj, cuts, axis=-1)
    mq = apply_rope(to_heads(mq, MOBA_HEADS), cos, sin)
    mk = apply_rope(to_heads(mk, MOBA_HEADS), cos, sin)
    moba_o = from_heads(moba_attention(mq, mk, to_heads(mv, MOBA_HEADS)))
    log_f = jax.nn.log_sigmoid(fz.astype(F32) + forget_b.astype(F32)).transpose(0, 2, 1)
    fox_o = from_heads(forgetting_attention(to_heads(fq, FOX_HEADS), to_heads(fk, FOX_HEADS),
                                            to_heads(fv, FOX_HEADS), log_f))
    merged = jnp.concatenate([rmsnorm(moba_o, moba_out_g), rmsnorm(fox_o, fox_out_g)], axis=-1)
    return jnp.einsum('bse,ed->bsd', merged, w_o)


def swiglu(t, w_gate, w_up, w_down):
    return jnp.matmul(jax.nn.silu(jnp.matmul(t, w_gate)) * jnp.matmul(t, w_up), w_down)


def moe_swiglu(h, router_w, w_gate, w_up, w_down):
    b, s, d = h.shape
    t = h.reshape(b * s, d)
    logits = jnp.einsum('nd,de->ne', t, router_w, preferred_element_type=F32)
    top_v, top_i = lax.top_k(logits, TOP_K)
    top_w = jax.nn.softmax(top_v, axis=-1)
    comb = jnp.sum(jax.nn.one_hot(top_i, N_EXPERTS, dtype=F32) * top_w[..., None], axis=1)
    y = jnp.zeros_like(t)
    for e in range(N_EXPERTS):
        y = y + comb[:, e:e + 1].astype(t.dtype) * swiglu(t, w_gate[e], w_up[e], w_down[e])
    return y.reshape(b, s, d)


def setup_inputs(seed: int = 0) -> dict:
    key = jax.random.key(seed)
    ks = jax.random.split(key, 22)
    d = D_MODEL
    n_layers = DEPTH
    n_dense = (DEPTH + 1) // 2
    n_moe = DEPTH // 2

    def nrm(k, shape, fan_in, gain=1.0):
        return jax.random.normal(k, shape, F32) * (gain * fan_in ** -0.5)

    def gain(k, shape):
        return 1.0 + 0.1 * jax.random.normal(k, shape, F32)

    return {
        'x': jax.random.normal(ks[0], (BATCH, SEQ, d), F32),
        'c': jax.random.normal(ks[1], (BATCH, d), F32),
        'attn_norm_g': gain(ks[2], (n_layers, d)),
        'ada_w': nrm(ks[3], (n_layers, d, 6 * d), d, 0.5),
        'ada_b': 0.02 * jax.random.normal(ks[4], (n_layers, 6 * d), F32),
        'w_in': nrm(ks[5], (n_layers, d, IN_WIDTH), d),
        'forget_b': jax.random.uniform(ks[6], (n_layers, FOX_HEADS), F32, minval=2.0, maxval=6.0),
        'moba_out_g': gain(ks[7], (n_layers, MOBA_WIDTH)),
        'fox_out_g': gain(ks[8], (n_layers, FOX_WIDTH)),
        'w_o': nrm(ks[9], (n_layers, MIX_WIDTH, d), MIX_WIDTH),
        'ffn_norm_g': gain(ks[10], (n_layers, d)),
        'dense_w_gate': nrm(ks[11], (n_dense, d, D_FF), d),
        'dense_w_up': nrm(ks[12], (n_dense, d, D_FF), d),
        'dense_w_down': nrm(ks[13], (n_dense, D_FF, d), D_FF),
        'router_w': nrm(ks[14], (n_moe, d, N_EXPERTS), d),
        'moe_w_gate': nrm(ks[15], (n_moe, N_EXPERTS, d, D_FF), d),
        'moe_w_up': nrm(ks[16], (n_moe, N_EXPERTS, d, D_FF), d),
        'moe_w_down': nrm(ks[17], (n_moe, N_EXPERTS, D_FF, d), D_FF),
        'final_norm_g': gain(ks[18], (d,)),
    }


def reference(x, c, attn_norm_g, ada_w, ada_b, w_in, forget_b, moba_out_g, fox_out_g, w_o,
              ffn_norm_g, dense_w_gate, dense_w_up, dense_w_down, router_w, moe_w_gate,
              moe_w_up, moe_w_down, final_norm_g):
    seq_len = x.shape[1]
    cos, sin = rope_tables(seq_len)
    c_act = jax.nn.silu(c)
    for l in range(DEPTH):
        mod = (jnp.matmul(c_act, ada_w[l]) + ada_b[l])[:, None, :]
        sh_a, sc_a, g_a, sh_f, sc_f, g_f = jnp.split(mod, 6, axis=-1)
        h = rmsnorm(x, attn_norm_g[l]) * (1.0 + sc_a) + sh_a
        x = x + g_a * hybrid_mixer(h, w_in[l], forget_b[l], moba_out_g[l], fox_out_g[l], w_o[l], cos, sin)
        h = rmsnorm(x, ffn_norm_g[l]) * (1.0 + sc_f) + sh_f
        i = l // 2
        if l % 2 == 0:
            y = swiglu(h, dense_w_gate[i], dense_w_up[i], dense_w_down[i])
        else:
            y = moe_swiglu(h, router_w[i], moe_w_gate[i], moe_w_up[i], moe_w_down[i])
        x = x + g_f * y
    return rmsnorm(x, final_norm_g)
```

```python
import functools

import jax
import jax.numpy as jnp
from jax import lax
from jax.experimental import pallas as pl
from jax.experimental.pallas import tpu as pltpu

F32 = jnp.float32
BF16 = jnp.bfloat16

HEAD_DIM = 128
MOBA_BLOCK = 256
MOBA_TOPK = 3
ROPE_THETA = 10000.0
EPS = 1e-6
TOP_K = 2
LANES = 128
VMEM_LIMIT = 56 * 1024 * 1024

NEG_INF = float("-inf")


def _params(semantics):
    return pltpu.CompilerParams(dimension_semantics=semantics, vmem_limit_bytes=VMEM_LIMIT)


def _tile(n, want):
    t = min(n, want)
    assert n % t == 0, (n, want)
    return t


def _ada_kernel(c_ref, w_ref, b_ref, o_ref):
    c = c_ref[...]
    c_act = c * jax.nn.sigmoid(c)
    o_ref[...] = jnp.dot(c_act, w_ref[...], preferred_element_type=F32,
                         precision=lax.Precision.HIGHEST) + b_ref[...]


def ada_modulation(c, ada_w, ada_b):
    n_layers, d, d6 = ada_w.shape
    b = c.shape[0]
    rows = -(-b // 8) * 8
    c_pad = jnp.zeros((rows, d), F32).at[:b].set(c)
    tn = _tile(d6, 1024)
    out = pl.pallas_call(
        _ada_kernel,
        grid=(n_layers, d6 // tn),
        in_specs=[
            pl.BlockSpec((rows, d), lambda l, j: (0, 0)),
            pl.BlockSpec((None, d, tn), lambda l, j: (l, 0, j)),
            pl.BlockSpec((None, 1, tn), lambda l, j: (l, 0, j)),
        ],
        out_specs=pl.BlockSpec((None, rows, tn), lambda l, j: (l, 0, j)),
        out_shape=jax.ShapeDtypeStruct((n_layers, rows, d6), F32),
        compiler_params=_params(("arbitrary", "arbitrary")),
        name="ada_modulation",
    )(c_pad, ada_w, ada_b.reshape(n_layers, 1, d6))
    return out[:, :b].reshape(n_layers, b, 6, d)


def _rms(x, g):
    return x * lax.rsqrt(jnp.mean(x * x, axis=-1, keepdims=True) + EPS) * g


def _norm_mod_kernel(x_ref, g_ref, mod_ref, o_ref, *, sh_row):
    y = _rms(x_ref[...], g_ref[...])
    sh = mod_ref[sh_row:sh_row + 1, :]
    sc = mod_ref[sh_row + 1:sh_row + 2, :]
    o_ref[...] = (y * (1.0 + sc) + sh).astype(o_ref.dtype)


def _top2_comb(logits, n_experts):
    lane = lax.broadcasted_iota(jnp.int32, logits.shape, 1).astype(F32)
    lg = jnp.where(lane < n_experts, logits, NEG_INF)
    m1 = jnp.max(lg, axis=-1, keepdims=True)
    i1 = jnp.min(jnp.where(lg == m1, lane, float(LANES)), axis=-1, keepdims=True)
    lg2 = jnp.where(lane == i1, NEG_INF, lg)
    m2 = jnp.max(lg2, axis=-1, keepdims=True)
    i2 = jnp.min(jnp.where(lg2 == m2, lane, float(LANES)), axis=-1, keepdims=True)
    e2 = jnp.exp(m2 - m1)
    w1 = 1.0 / (1.0 + e2)
    w2 = e2 / (1.0 + e2)
    return jnp.where(lane == i1, w1, 0.0) + jnp.where(lane == i2, w2, 0.0)


def _norm_mod_router_kernel(x_ref, g_ref, mod_ref, wr_ref, o_ref, comb_ref, *, sh_row, n_experts):
    y = _rms(x_ref[...], g_ref[...])
    sh = mod_ref[sh_row:sh_row + 1, :]
    sc = mod_ref[sh_row + 1:sh_row + 2, :]
    h = y * (1.0 + sc) + sh
    o_ref[...] = h.astype(o_ref.dtype)
    logits = jnp.dot(h, wr_ref[...], preferred_element_type=F32, precision=lax.Precision.HIGHEST)
    comb_ref[...] = _top2_comb(logits, n_experts)


def norm_mod(x2, g, mod, seq, sh_row, router_w=None, tm=512):
    n, d = x2.shape
    tm = _tile(seq, tm)
    per_b = seq // tm
    in_specs = [
        pl.BlockSpec((tm, d), lambda i: (i, 0)),
        pl.BlockSpec((1, d), lambda i: (0, 0)),
        pl.BlockSpec((None, 6, d), lambda i: (i // per_b, 0, 0)),
    ]
    args = [x2, g.reshape(1, d), mod]
    if router_w is None:
        return pl.pallas_call(
            functools.partial(_norm_mod_kernel, sh_row=sh_row),
            grid=(n // tm,),
            in_specs=in_specs,
            out_specs=pl.BlockSpec((tm, d), lambda i: (i, 0)),
            out_shape=jax.ShapeDtypeStruct((n, d), BF16),
            compiler_params=_params(("arbitrary",)),
            name="norm_mod",
        )(*args)
    n_experts = router_w.shape[1]
    wr = jnp.zeros((d, LANES), F32).at[:, :n_experts].set(router_w)
    return pl.pallas_call(
        functools.partial(_norm_mod_router_kernel, sh_row=sh_row, n_experts=n_experts),
        grid=(n // tm,),
        in_specs=in_specs + [pl.BlockSpec((d, LANES), lambda i: (0, 0))],
        out_specs=[pl.BlockSpec((tm, d), lambda i: (i, 0)),
                   pl.BlockSpec((tm, LANES), lambda i: (i, 0))],
        out_shape=[jax.ShapeDtypeStruct((n, d), BF16),
                   jax.ShapeDtypeStruct((n, LANES), F32)],
        compiler_params=_params(("arbitrary",)),
        name="norm_mod_router",
    )(*args, wr)


def _final_norm_kernel(x_ref, g_ref, o_ref):
    o_ref[...] = _rms(x_ref[...], g_ref[...])


def final_norm(x2, g, tm=512):
    n, d = x2.shape
    tm = _tile(n, tm)
    return pl.pallas_call(
        _final_norm_kernel,
        grid=(n // tm,),
        in_specs=[pl.BlockSpec((tm, d), lambda i: (i, 0)),
                  pl.BlockSpec((1, d), lambda i: (0, 0))],
        out_specs=pl.BlockSpec((tm, d), lambda i: (i, 0)),
        out_shape=jax.ShapeDtypeStruct((n, d), F32),
        compiler_params=_params(("arbitrary",)),
        name="final_norm",
    )(x2, g.reshape(1, d))


def _in_proj_kernel(h_ref, w_ref, wz_ref, cos_ref, sin_ref, o_ref, fz_ref, *, n_rope_tiles):
    j = pl.program_id(1)
    acc = jnp.dot(h_ref[...], w_ref[...], preferred_element_type=F32)
    tn = acc.shape[1]

    @pl.when(j < n_rope_tiles)
    def _():
        c = cos_ref[...]
        s = sin_ref[...]
        for hh in range(tn // HEAD_DIM):
            t = acc[:, hh * HEAD_DIM:(hh + 1) * HEAD_DIM]
            rot = pltpu.roll(t, HEAD_DIM // 2, 1)
            o_ref[:, hh * HEAD_DIM:(hh + 1) * HEAD_DIM] = (t * c + rot * s).astype(o_ref.dtype)

    @pl.when(j >= n_rope_tiles)
    def _():
        o_ref[...] = acc.astype(o_ref.dtype)

    @pl.when(j == 0)
    def _():
        fz_ref[...] = jnp.dot(h_ref[...], wz_ref[...], preferred_element_type=F32)


def in_proj(h, w_qkv, wz, cos_t, sin_t, seq, n_rope_cols, tm=1024, tn=1024):
    n, d = h.shape
    width = w_qkv.shape[1]
    tm = _tile(seq, tm)
    tn = _tile(width, tn)
    assert n_rope_cols % tn == 0
    per_b = seq // tm
    return pl.pallas_call(
        functools.partial(_in_proj_kernel, n_rope_tiles=n_rope_cols // tn),
        grid=(n // tm, width // tn),
        in_specs=[
            pl.BlockSpec((tm, d), lambda i, j: (i, 0)),
            pl.BlockSpec((d, tn), lambda i, j: (0, j)),
            pl.BlockSpec((d, LANES), lambda i, j: (0, 0)),
            pl.BlockSpec((tm, HEAD_DIM), lambda i, j: (i % per_b, 0)),
            pl.BlockSpec((tm, HEAD_DIM), lambda i, j: (i % per_b, 0)),
        ],
        out_specs=[pl.BlockSpec((tm, tn), lambda i, j: (i, j)),
                   pl.BlockSpec((tm, LANES), lambda i, j: (i, 0))],
        out_shape=[jax.ShapeDtypeStruct((n, width), BF16),
                   jax.ShapeDtypeStruct((n, LANES), F32)],
        compiler_params=_params(("arbitrary", "arbitrary")),
        name="in_proj",
    )(h, w_qkv, wz, cos_t, sin_t)


def _logf_cumsum_kernel(z_ref, b_ref, o_ref):
    z = z_ref[...] + b_ref[...]
    lf = jnp.minimum(z, 0.0) - jnp.log1p(jnp.exp(-jnp.abs(z)))
    r = lax.broadcasted_iota(jnp.int32, (LANES, LANES), 0)
    c = lax.broadcasted_iota(jnp.int32, (LANES, LANES), 1)
    tri = jnp.where(r <= c, 1.0, 0.0).astype(F32)
    carry = jnp.zeros((z.shape[0], 1), F32)
    for ch in range(z.shape[1] // LANES):
        seg = lf[:, ch * LANES:(ch + 1) * LANES]
        cs = jnp.dot(seg, tri, preferred_element_type=F32, precision=lax.Precision.HIGHEST) + carry
        o_ref[:, ch * LANES:(ch + 1) * LANES] = cs
        carry = cs[:, LANES - 1:LANES]


def logf_cumsum(z, forget_b):
    b, h, s = z.shape
    return pl.pallas_call(
        _logf_cumsum_kernel,
        grid=(b,),
        in_specs=[pl.BlockSpec((None, h, s), lambda i: (i, 0, 0)),
                  pl.BlockSpec((h, 1), lambda i: (0, 0))],
        out_specs=pl.BlockSpec((None, h, s), lambda i: (i, 0, 0)),
        out_shape=jax.ShapeDtypeStruct((b, h, s), F32),
        compiler_params=_params(("arbitrary",)),
        name="logf_cumsum",
    )(z, forget_b.reshape(h, 1))


def _qk(q, k):
    return lax.dot_general(q, k, (((1,), (1,)), ((), ())), preferred_element_type=F32)


def _softmax_first(s, v, m_scr, l_scr, acc_scr):
    m = jnp.max(s, axis=-1, keepdims=True)
    p = jnp.exp(s - m)
    m_scr[...] = m
    l_scr[...] = jnp.sum(p, axis=-1, keepdims=True)
    acc_scr[...] = jnp.dot(p.astype(v.dtype), v, preferred_element_type=F32)


def _softmax_update(s, v, m_scr, l_scr, acc_scr):
    m_old = m_scr[...]
    m_new = jnp.maximum(m_old, jnp.max(s, axis=-1, keepdims=True))
    alpha = jnp.exp(m_old - m_new)
    p = jnp.exp(s - m_new)
    l_scr[...] = alpha * l_scr[...] + jnp.sum(p, axis=-1, keepdims=True)
    acc_scr[...] = alpha * acc_scr[...] + jnp.dot(p.astype(v.dtype), v, preferred_element_type=F32)
    m_scr[...] = m_new


def _causal(tq):
    row = lax.broadcasted_iota(jnp.int32, (tq, tq), 0)
    col = lax.broadcasted_iota(jnp.int32, (tq, tq), 1)
    return col <= row


def _fox_kernel(q_ref, k_ref, v_ref, cw_ref, ct_ref, o_ref, m_scr, l_scr, acc_scr, *, scale):
    h = pl.program_id(1)
    i = pl.program_id(2)
    tq = q_ref.shape[0]
    q = q_ref[...]
    ct = ct_ref[...]
    lane = lax.broadcasted_iota(jnp.int32, ct.shape, 1)
    f_row = jnp.sum(jnp.where(lane == h, ct, 0.0), axis=-1, keepdims=True)

    def scores(j):
        start = pl.multiple_of(j * tq, tq)
        kj = k_ref[pl.ds(start, tq), :]
        vj = v_ref[pl.ds(start, tq), :]
        s = _qk(q, kj) * scale + f_row - cw_ref[:, pl.ds(start, tq)]
        return s, vj

    s, vj = scores(i)
    _softmax_first(jnp.where(_causal(tq), s, NEG_INF), vj, m_scr, l_scr, acc_scr)

    def body(j, carry):
        s, vj = scores(j)
        _softmax_update(s, vj, m_scr, l_scr, acc_scr)
        return carry

    lax.fori_loop(0, i, body, 0)
    o_ref[...] = (acc_scr[...] / l_scr[...]).astype(o_ref.dtype)


def fox_attention(proj, cum, cum_t, batch, seq, heads, q_col, k_col, v_col, tq=256):
    n = proj.shape[0]
    tq = _tile(seq, tq)
    nq = seq // tq
    scale = HEAD_DIM ** -0.5
    return pl.pallas_call(
        functools.partial(_fox_kernel, scale=scale),
        grid=(batch, heads, nq),
        in_specs=[
            pl.BlockSpec((tq, HEAD_DIM), lambda b, h, i: (b * nq + i, q_col + h)),
            pl.BlockSpec((seq, HEAD_DIM), lambda b, h, i: (b, k_col + h)),
            pl.BlockSpec((seq, HEAD_DIM), lambda b, h, i: (b, v_col + h)),
            pl.BlockSpec((None, 1, seq), lambda b, h, i: (b * heads + h, 0, 0)),
            pl.BlockSpec((None, tq, heads), lambda b, h, i: (b, i, 0)),
        ],
        out_specs=pl.BlockSpec((tq, HEAD_DIM), lambda b, h, i: (b * nq + i, h)),
        out_shape=jax.ShapeDtypeStruct((n, heads * HEAD_DIM), BF16),
        scratch_shapes=[pltpu.VMEM((tq, 1), F32), pltpu.VMEM((tq, 1), F32),
                        pltpu.VMEM((tq, HEAD_DIM), F32)],
        compiler_params=_params(("arbitrary", "arbitrary", "arbitrary")),
        name="fox_attention",
    )(proj, proj, proj, cum.reshape(batch * heads, 1, seq), cum_t)


def _moba_kernel(q_ref, k_ref, v_ref, o_ref, kmean_scr, m_scr, l_scr, acc_scr, *, scale, n_blocks):
    n = pl.program_id(2)
    blk = q_ref.shape[0]
    q = q_ref[...]

    @pl.when(n == 0)
    def _():
        kmean_scr[...] = jnp.zeros_like(kmean_scr)
        for b in range(n_blocks):
            kb = k_ref[b * blk:(b + 1) * blk, :].astype(F32)
            kmean_scr[b:b + 1, :] = jnp.mean(kb, axis=0, keepdims=True)

    gate = lax.dot_general(q.astype(F32), kmean_scr[...], (((1,), (1,)), ((), ())),
                           preferred_element_type=F32, precision=lax.Precision.HIGHEST)
    lane_i = lax.broadcasted_iota(jnp.int32, gate.shape, 1)
    lane = lane_i.astype(F32)
    g = jnp.where(lane_i < n, gate, NEG_INF)
    sel = jnp.zeros(gate.shape, F32)
    for _ in range(min(MOBA_TOPK, n_blocks)):
        m = jnp.max(g, axis=-1, keepdims=True)
        idx = jnp.min(jnp.where(g == m, lane, float(LANES)), axis=-1, keepdims=True)
        hit = lane == idx
        sel = jnp.where(jnp.logical_and(hit, m > NEG_INF), 1.0, sel)
        g = jnp.where(hit, NEG_INF, g)

    start = pl.multiple_of(n * blk, blk)
    s = _qk(q, k_ref[pl.ds(start, blk), :]) * scale
    _softmax_first(jnp.where(_causal(blk), s, NEG_INF), v_ref[pl.ds(start, blk), :],
                   m_scr, l_scr, acc_scr)

    def body(j, carry):
        st = pl.multiple_of(j * blk, blk)
        sel_j = jnp.sum(jnp.where(lane_i == j, sel, 0.0), axis=-1, keepdims=True)
        s = _qk(q, k_ref[pl.ds(st, blk), :]) * scale
        s = jnp.where(sel_j > 0.0, s, NEG_INF)
        _softmax_update(s, v_ref[pl.ds(st, blk), :], m_scr, l_scr, acc_scr)
        return carry

    lax.fori_loop(0, n, body, 0)
    o_ref[...] = (acc_scr[...] / l_scr[...]).astype(o_ref.dtype)


def moba_attention(proj, batch, seq, heads, q_col, k_col, v_col):
    n = proj.shape[0]
    blk = MOBA_BLOCK
    assert seq % blk == 0 and seq // blk <= LANES
    nb = seq // blk
    scale = HEAD_DIM ** -0.5
    return pl.pallas_call(
        functools.partial(_moba_kernel, scale=scale, n_blocks=nb),
        grid=(batch, heads, nb),
        in_specs=[
            pl.BlockSpec((blk, HEAD_DIM), lambda b, h, i: (b * nb + i, q_col + h)),
            pl.BlockSpec((seq, HEAD_DIM), lambda b, h, i: (b, k_col + h)),
            pl.BlockSpec((seq, HEAD_DIM), lambda b, h, i: (b, v_col + h)),
        ],
        out_specs=pl.BlockSpec((blk, HEAD_DIM), lambda b, h, i: (b * nb + i, h)),
        out_shape=jax.ShapeDtypeStruct((n, heads * HEAD_DIM), BF16),
        scratch_shapes=[pltpu.VMEM((LANES, HEAD_DIM), F32),
                        pltpu.VMEM((blk, 1), F32), pltpu.VMEM((blk, 1), F32),
                        pltpu.VMEM((blk, HEAD_DIM), F32)],
        compiler_params=_params(("arbitrary", "arbitrary", "arbitrary")),
        name="moba_attention",
    )(proj, proj, proj)


def _out_proj_kernel(mo_ref, fo_ref, g1_ref, g2_ref, w_ref, x_ref, mod_ref, o_ref, *, gate_row):
    half = mo_ref.shape[1]
    m1 = _rms(mo_ref[...].astype(F32), g1_ref[...]).astype(BF16)
    m2 = _rms(fo_ref[...].astype(F32), g2_ref[...]).astype(BF16)
    acc = jnp.dot(m1, w_ref[0:half, :], preferred_element_type=F32)
    acc = acc + jnp.dot(m2, w_ref[half:2 * half, :], preferred_element_type=F32)
    o_ref[...] = x_ref[...] + mod_ref[gate_row:gate_row + 1, :] * acc


def out_proj(moba_o, fox_o, g1, g2, w_o, x2, mod, seq, gate_row, tm=512):
    n, d = x2.shape
    half = moba_o.shape[1]
    tm = _tile(seq, tm)
    per_b = seq // tm
    return pl.pallas_call(
        functools.partial(_out_proj_kernel, gate_row=gate_row),
        grid=(n // tm,),
        in_specs=[
            pl.BlockSpec((tm, half), lambda i: (i, 0)),
            pl.BlockSpec((tm, half), lambda i: (i, 0)),
            pl.BlockSpec((1, half), lambda i: (0, 0)),
            pl.BlockSpec((1, half), lambda i: (0, 0)),
            pl.BlockSpec((2 * half, d), lambda i: (0, 0)),
            pl.BlockSpec((tm, d), lambda i: (i, 0)),
            pl.BlockSpec((None, 6, d), lambda i: (i // per_b, 0, 0)),
        ],
        out_specs=pl.BlockSpec((tm, d), lambda i: (i, 0)),
        out_shape=jax.ShapeDtypeStruct((n, d), F32),
        compiler_params=_params(("arbitrary",)),
        name="out_proj",
    )(moba_o, fox_o, g1.reshape(1, half), g2.reshape(1, half), w_o, x2, mod)


def _swiglu_tile(h, wg, wu):
    a = jnp.dot(h, wg, preferred_element_type=F32)
    b = jnp.dot(h, wu, preferred_element_type=F32)
    return a * jax.nn.sigmoid(a) * b


def _ffn_kernel(h_ref, wg_ref, wu_ref, wd_ref, x_ref, mod_ref, o_ref, *, gate_row):
    f = pl.program_id(1)
    t = _swiglu_tile(h_ref[...], wg_ref[...], wu_ref[...]).astype(BF16)
    contrib = jnp.dot(t, wd_ref[...], preferred_element_type=F32)

    @pl.when(f == 0)
    def _():
        o_ref[...] = contrib

    @pl.when(f > 0)
    def _():
        o_ref[...] += contrib

    @pl.when(f == pl.num_programs(1) - 1)
    def _():
        o_ref[...] = x_ref[...] + mod_ref[gate_row:gate_row + 1, :] * o_ref[...]


def ffn_dense(h, wg, wu, wd, x2, mod, seq, gate_row, tm=512, tf=512):
    n, d = x2.shape
    ff = wg.shape[1]
    tm = _tile(seq, tm)
    tf = _tile(ff, tf)
    per_b = seq // tm
    return pl.pallas_call(
        functools.partial(_ffn_kernel, gate_row=gate_row),
        grid=(n // tm, ff // tf),
        in_specs=[
            pl.BlockSpec((tm, d), lambda i, f: (i, 0)),
            pl.BlockSpec((d, tf), lambda i, f: (0, f)),
            pl.BlockSpec((d, tf), lambda i, f: (0, f)),
            pl.BlockSpec((tf, d), lambda i, f: (f, 0)),
            pl.BlockSpec((tm, d), lambda i, f: (i, 0)),
            pl.BlockSpec((None, 6, d), lambda i, f: (i // per_b, 0, 0)),
        ],
        out_specs=pl.BlockSpec((tm, d), lambda i, f: (i, 0)),
        out_shape=jax.ShapeDtypeStruct((n, d), F32),
        compiler_params=_params(("arbitrary", "arbitrary")),
        name="ffn_dense",
    )(h, wg, wu, wd, x2, mod)


def _moe_kernel(h_ref, comb_ref, wg_ref, wu_ref, wd_ref, x_ref, mod_ref, o_ref, *, gate_row):
    e = pl.program_id(1)
    f = pl.program_id(2)
    comb = comb_ref[...]
    lane = lax.broadcasted_iota(jnp.int32, comb.shape, 1)
    w_e = jnp.sum(jnp.where(lane == e, comb, 0.0), axis=-1, keepdims=True)
    t = (_swiglu_tile(h_ref[...], wg_ref[...], wu_ref[...]) * w_e).astype(BF16)
    contrib = jnp.dot(t, wd_ref[...], preferred_element_type=F32)
    first = jnp.logical_and(e == 0, f == 0)
    last = jnp.logical_and(e == pl.num_programs(1) - 1, f == pl.num_programs(2) - 1)

    @pl.when(first)
    def _():
        o_ref[...] = contrib

    @pl.when(jnp.logical_not(first))
    def _():
        o_ref[...] += contrib

    @pl.when(last)
    def _():
        o_ref[...] = x_ref[...] + mod_ref[gate_row:gate_row + 1, :] * o_ref[...]


def ffn_moe(h, comb, wg, wu, wd, x2, mod, seq, gate_row, tm=512, tf=512):
    n, d = x2.shape
    n_experts, _, ff = wg.shape
    tm = _tile(seq, tm)
    tf = _tile(ff, tf)
    per_b = seq // tm
    return pl.pallas_call(
        functools.partial(_moe_kernel, gate_row=gate_row),
        grid=(n // tm, n_experts, ff // tf),
        in_specs=[
            pl.BlockSpec((tm, d), lambda i, e, f: (i, 0)),
            pl.BlockSpec((tm, LANES), lambda i, e, f: (i, 0)),
            pl.BlockSpec((None, d, tf), lambda i, e, f: (e, 0, f)),
            pl.BlockSpec((None, d, tf), lambda i, e, f: (e, 0, f)),
            pl.BlockSpec((None, tf, d), lambda i, e, f: (e, f, 0)),
            pl.BlockSpec((tm, d), lambda i, e, f: (i, 0)),
            pl.BlockSpec((None, 6, d), lambda i, e, f: (i // per_b, 0, 0)),
        ],
        out_specs=pl.BlockSpec((tm, d), lambda i, e, f: (i, 0)),
        out_shape=jax.ShapeDtypeStruct((n, d), F32),
        compiler_params=_params(("arbitrary", "arbitrary", "arbitrary")),
        name="ffn_moe",
    )(h, comb, wg, wu, wd, x2, mod)


def _rope_tables(seq):
    inv = ROPE_THETA ** (-jnp.arange(0, HEAD_DIM, 2, dtype=F32) / HEAD_DIM)
    ang = jnp.arange(seq, dtype=F32)[:, None] * inv[None, :]
    cos, sin = jnp.cos(ang), jnp.sin(ang)
    return jnp.concatenate([cos, cos], axis=-1), jnp.concatenate([-sin, sin], axis=-1)


def kernel(x, c, attn_norm_g, ada_w, ada_b, w_in, forget_b, moba_out_g, fox_out_g, w_o, ffn_norm_g,
           dense_w_gate, dense_w_up, dense_w_down, router_w, moe_w_gate, moe_w_up, moe_w_down,
           final_norm_g):
    batch, seq, d = x.shape
    depth = attn_norm_g.shape[0]
    moba_w = moba_out_g.shape[1]
    fox_w = fox_out_g.shape[1]
    moba_h = moba_w // HEAD_DIM
    fox_h = fox_w // HEAD_DIM
    qkv_w = 3 * moba_w + 3 * fox_w
    assert moba_w == fox_w

    cos_t, sin_t = _rope_tables(seq)
    mods = ada_modulation(c, ada_w, ada_b)
    x2 = x.reshape(batch * seq, d)
    hb = HEAD_DIM

    for l in range(depth):
        mod = mods[l]
        h = norm_mod(x2, attn_norm_g[l], mod, seq, sh_row=0)
        w_qkv = w_in[l][:, :qkv_w].astype(BF16)
        wz = jnp.zeros((d, LANES), BF16).at[:, :fox_h].set(w_in[l][:, qkv_w:].astype(BF16))
        proj, fz = in_proj(h, w_qkv, wz, cos_t, sin_t, seq, n_rope_cols=2 * moba_w)
        z = fz[:, :fox_h].reshape(batch, seq, fox_h).transpose(0, 2, 1)
        cum = logf_cumsum(z, forget_b[l])
        cum_t = cum.transpose(0, 2, 1)
        moba_o = moba_attention(proj, batch, seq, moba_h, q_col=0, k_col=moba_w // hb,
                                v_col=2 * moba_w // hb)
        fc = 3 * moba_w // hb
        fox_o = fox_attention(proj, cum, cum_t, batch, seq, fox_h, q_col=fc,
                              k_col=fc + fox_w // hb, v_col=fc + 2 * fox_w // hb)
        x2 = out_proj(moba_o, fox_o, moba_out_g[l], fox_out_g[l], w_o[l].astype(BF16), x2, mod,
                      seq, gate_row=2)
        i = l // 2
        if l % 2 == 0:
            h = norm_mod(x2, ffn_norm_g[l], mod, seq, sh_row=3)
            x2 = ffn_dense(h, dense_w_gate[i].astype(BF16), dense_w_up[i].astype(BF16),
                           dense_w_down[i].astype(BF16), x2, mod, seq, gate_row=5)
        else:
            h, comb = norm_mod(x2, ffn_norm_g[l], mod, seq, sh_row=3, router_w=router_w[i])
            x2 = ffn_moe(h, comb, moe_w_gate[i].astype(BF16), moe_w_up[i].astype(BF16),
                         moe_w_down[i].astype(BF16), x2, mod, seq, gate_row=5)
    return final_norm(x2, final_norm_g).reshape(batch, seq, d)
```

```python
import functools
import math

import jax
import jax.numpy as jnp
from jax import lax
from jax.experimental import pallas as pl
from jax.experimental.pallas import tpu as pltpu

F32 = jnp.float32
BF16 = jnp.bfloat16

HEAD_DIM = 128
MOBA_BLOCK = 256
MOBA_TOPK = 3
ROPE_THETA = 10000.0
EPS = 1e-6
TOP_K = 2
LANES = 128
VMEM_LIMIT = 56 * 1024 * 1024

NEG_INF = float("-inf")
NEG_BIG = -(2.0 ** 100)
LOG2E = math.log2(math.e)
Q_SCALE = HEAD_DIM ** -0.5 * LOG2E


def _params(semantics, **kw):
    return pltpu.CompilerParams(dimension_semantics=semantics, vmem_limit_bytes=VMEM_LIMIT, **kw)


def _tile(n, want):
    t = min(n, want)
    assert n % t == 0, (n, want)
    return t


def _ada_kernel(c_ref, w_ref, b_ref, o_ref):
    c = c_ref[...]
    c_act = c * jax.nn.sigmoid(c)
    o_ref[...] = jnp.dot(c_act, w_ref[...], preferred_element_type=F32,
                         precision=lax.Precision.HIGHEST) + b_ref[...]


def ada_modulation(c, ada_w, ada_b):
    n_layers, d, d6 = ada_w.shape
    b = c.shape[0]
    rows = -(-b // 8) * 8
    c_pad = jnp.zeros((rows, d), F32).at[:b].set(c)
    tn = _tile(d6, 1024)
    out = pl.pallas_call(
        _ada_kernel,
        grid=(n_layers, d6 // tn),
        in_specs=[
            pl.BlockSpec((rows, d), lambda l, j: (0, 0)),
            pl.BlockSpec((None, d, tn), lambda l, j: (l, 0, j)),
            pl.BlockSpec((None, 1, tn), lambda l, j: (l, 0, j)),
        ],
        out_specs=pl.BlockSpec((None, rows, tn), lambda l, j: (l, 0, j)),
        out_shape=jax.ShapeDtypeStruct((n_layers, rows, d6), F32),
        compiler_params=_params(("arbitrary", "arbitrary")),
        name="ada_modulation",
    )(c_pad, ada_w, ada_b.reshape(n_layers, 1, d6))
    return out[:, :b].reshape(n_layers, b, 6, d)


def _rms(x, g):
    return x * lax.rsqrt(jnp.mean(x * x, axis=-1, keepdims=True) + EPS) * g


def _norm_mod_kernel(x_ref, g_ref, mod_ref, o_ref, *, sh_row):
    y = _rms(x_ref[...], g_ref[...])
    sh = mod_ref[sh_row:sh_row + 1, :]
    sc = mod_ref[sh_row + 1:sh_row + 2, :]
    o_ref[...] = (y * (1.0 + sc) + sh).astype(o_ref.dtype)


def _top2_route(logits, n_experts):
    lane = lax.broadcasted_iota(jnp.int32, logits.shape, 1).astype(F32)
    lg = jnp.where(lane < n_experts, logits, NEG_INF)
    m1 = jnp.max(lg, axis=-1, keepdims=True)
    i1 = jnp.min(jnp.where(lg == m1, lane, float(LANES)), axis=-1, keepdims=True)
    lg2 = jnp.where(lane == i1, NEG_INF, lg)
    m2 = jnp.max(lg2, axis=-1, keepdims=True)
    i2 = jnp.min(jnp.where(lg2 == m2, lane, float(LANES)), axis=-1, keepdims=True)
    e2 = jnp.exp(m2 - m1)
    w1 = 1.0 / (1.0 + e2)
    w2 = e2 / (1.0 + e2)
    return jnp.where(lane == 0.0, i1, jnp.where(lane == 1.0, i2,
                     jnp.where(lane == 2.0, w1, jnp.where(lane == 3.0, w2, 0.0))))


def _norm_mod_router_kernel(x_ref, g_ref, mod_ref, wr_ref, o_ref, route_ref, *, sh_row, n_experts):
    y = _rms(x_ref[...], g_ref[...])
    sh = mod_ref[sh_row:sh_row + 1, :]
    sc = mod_ref[sh_row + 1:sh_row + 2, :]
    h = y * (1.0 + sc) + sh
    o_ref[...] = h.astype(o_ref.dtype)
    logits = jnp.dot(h, wr_ref[...], preferred_element_type=F32, precision=lax.Precision.HIGHEST)
    route_ref[...] = _top2_route(logits, n_experts)


def norm_mod(x2, g, mod, seq, sh_row, router_w=None, tm=512):
    n, d = x2.shape
    tm = _tile(seq, tm)
    per_b = seq // tm
    in_specs = [
        pl.BlockSpec((tm, d), lambda i: (i, 0)),
        pl.BlockSpec((1, d), lambda i: (0, 0)),
        pl.BlockSpec((None, 6, d), lambda i: (i // per_b, 0, 0)),
    ]
    args = [x2, g.reshape(1, d), mod]
    if router_w is None:
        return pl.pallas_call(
            functools.partial(_norm_mod_kernel, sh_row=sh_row),
            grid=(n // tm,),
            in_specs=in_specs,
            out_specs=pl.BlockSpec((tm, d), lambda i: (i, 0)),
            out_shape=jax.ShapeDtypeStruct((n, d), BF16),
            compiler_params=_params(("arbitrary",)),
            name="norm_mod",
        )(*args)
    n_experts = router_w.shape[1]
    wr = jnp.zeros((d, LANES), F32).at[:, :n_experts].set(router_w)
    return pl.pallas_call(
        functools.partial(_norm_mod_router_kernel, sh_row=sh_row, n_experts=n_experts),
        grid=(n // tm,),
        in_specs=in_specs + [pl.BlockSpec((d, LANES), lambda i: (0, 0))],
        out_specs=[pl.BlockSpec((tm, d), lambda i: (i, 0)),
                   pl.BlockSpec((tm, LANES), lambda i: (i, 0))],
        out_shape=[jax.ShapeDtypeStruct((n, d), BF16),
                   jax.ShapeDtypeStruct((n, LANES), F32)],
        compiler_params=_params(("arbitrary",)),
        name="norm_mod_router",
    )(*args, wr)


def _final_norm_kernel(x_ref, g_ref, o_ref):
    o_ref[...] = _rms(x_ref[...], g_ref[...])


def final_norm(x2, g, tm=512):
    n, d = x2.shape
    tm = _tile(n, tm)
    return pl.pallas_call(
        _final_norm_kernel,
        grid=(n // tm,),
        in_specs=[pl.BlockSpec((tm, d), lambda i: (i, 0)),
                  pl.BlockSpec((1, d), lambda i: (0, 0))],
        out_specs=pl.BlockSpec((tm, d), lambda i: (i, 0)),
        out_shape=jax.ShapeDtypeStruct((n, d), F32),
        compiler_params=_params(("arbitrary",)),
        name="final_norm",
    )(x2, g.reshape(1, d))


PLAIN, ROPE, ROPE_SCALED, SCALED = 0, 1, 2, 3


def _in_proj_kernel(h_ref, w_ref, wz_ref, cos_ref, sin_ref, o_ref, fz_ref, *, kinds):
    j = pl.program_id(1)
    acc = jnp.dot(h_ref[...], w_ref[...], preferred_element_type=F32)
    tn = acc.shape[1]

    def is_kind(kind):
        hits = [jj for jj, k in enumerate(kinds) if k == kind]
        cond = j == hits[0]
        for jj in hits[1:]:
            cond = jnp.logical_or(cond, j == jj)
        return cond

    for kind in sorted(set(kinds)):
        @pl.when(is_kind(kind))
        def _(kind=kind):
            mult = Q_SCALE if kind in (ROPE_SCALED, SCALED) else 1.0
            if kind in (ROPE, ROPE_SCALED):
                c = cos_ref[...] * mult
                s = sin_ref[...] * mult
                for hh in range(tn // HEAD_DIM):
                    t = acc[:, hh * HEAD_DIM:(hh + 1) * HEAD_DIM]
                    rot = pltpu.roll(t, HEAD_DIM // 2, 1)
                    o_ref[:, hh * HEAD_DIM:(hh + 1) * HEAD_DIM] = (t * c + rot * s).astype(o_ref.dtype)
            elif kind == SCALED:
                o_ref[...] = (acc * mult).astype(o_ref.dtype)
            else:
                o_ref[...] = acc.astype(o_ref.dtype)

    @pl.when(j == 0)
    def _():
        fz_ref[...] = jnp.dot(h_ref[...], wz_ref[...], preferred_element_type=F32)


def in_proj(h, w_qkv, wz, cos_t, sin_t, seq, kinds, tm=1024, tn=1024):
    n, d = h.shape
    width = w_qkv.shape[1]
    tm = _tile(seq, tm)
    assert width == tn * len(kinds)
    per_b = seq // tm
    return pl.pallas_call(
        functools.partial(_in_proj_kernel, kinds=tuple(kinds)),
        grid=(n // tm, width // tn),
        in_specs=[
            pl.BlockSpec((tm, d), lambda i, j: (i, 0)),
            pl.BlockSpec((d, tn), lambda i, j: (0, j)),
            pl.BlockSpec((d, LANES), lambda i, j: (0, 0)),
            pl.BlockSpec((tm, HEAD_DIM), lambda i, j: (i % per_b, 0)),
            pl.BlockSpec((tm, HEAD_DIM), lambda i, j: (i % per_b, 0)),
        ],
        out_specs=[pl.BlockSpec((tm, tn), lambda i, j: (i, j)),
                   pl.BlockSpec((tm, LANES), lambda i, j: (i, 0))],
        out_shape=[jax.ShapeDtypeStruct((n, width), BF16),
                   jax.ShapeDtypeStruct((n, LANES), F32)],
        compiler_params=_params(("arbitrary", "arbitrary")),
        name="in_proj",
    )(h, w_qkv, wz, cos_t, sin_t)


def _logf_cumsum_kernel(z_ref, b_ref, o_ref):
    z = z_ref[...] + b_ref[...]
    lf = jnp.minimum(z, 0.0) - jnp.log1p(jnp.exp(-jnp.abs(z)))
    r = lax.broadcasted_iota(jnp.int32, (LANES, LANES), 0)
    c = lax.broadcasted_iota(jnp.int32, (LANES, LANES), 1)
    tri = jnp.where(c <= r, 1.0, 0.0).astype(F32)
    carry = jnp.zeros((1, z.shape[1]), F32)
    for ch in range(z.shape[0] // LANES):
        seg = lf[ch * LANES:(ch + 1) * LANES, :]
        cs = jnp.dot(tri, seg, preferred_element_type=F32, precision=lax.Precision.HIGHEST) + carry
        o_ref[ch * LANES:(ch + 1) * LANES, :] = cs
        carry = cs[LANES - 1:LANES, :]


def logf_cumsum(z, forget_b):
    b, s, h = z.shape
    return pl.pallas_call(
        _logf_cumsum_kernel,
        grid=(b,),
        in_specs=[pl.BlockSpec((None, s, h), lambda i: (i, 0, 0)),
                  pl.BlockSpec((1, h), lambda i: (0, 0))],
        out_specs=pl.BlockSpec((None, s, h), lambda i: (i, 0, 0)),
        out_shape=jax.ShapeDtypeStruct((b, s, h), F32),
        compiler_params=_params(("arbitrary",)),
        name="logf_cumsum",
    )(z, forget_b.reshape(1, h))


def _split3(f):
    hi = f.astype(BF16).astype(F32)
    r = f - hi
    mid = r.astype(BF16).astype(F32)
    lo = (r - mid).astype(BF16).astype(F32)
    return hi, mid, lo


def _pick_lane(x, idx):
    lane = lax.broadcasted_iota(jnp.int32, x.shape, 1)
    return jnp.sum(jnp.where(lane == idx, x, 0.0), axis=-1, keepdims=True)


def _attn_kernel(*refs, mode, tq, tk, seq):
    if mode == "fox":
        q_ref, k_ref, v_ref, ct_ref, o_ref, kaug, vaug, m_scr, acc_scr = refs
    else:
        q_ref, k_ref, v_ref, o_ref, kaug, vaug, m_scr, acc_scr, kmean_scr = refs
    h = pl.program_id(1)
    i = pl.program_id(2)
    n_moba_blocks = seq // MOBA_BLOCK
    per_q = tq // tk

    @pl.when(i == 0)
    def _():
        lane = lax.broadcasted_iota(jnp.int32, (tk, LANES), 1)
        row = lax.broadcasted_iota(jnp.int32, (tk, LANES), 0)

        def build(cidx, carry):
            st = pl.multiple_of(cidx * tk, tk)
            kaug[pl.ds(st, tk), 0:HEAD_DIM] = k_ref[pl.ds(st, tk), :]
            vaug[pl.ds(st, tk), 0:HEAD_DIM] = v_ref[pl.ds(st, tk), :]
            vaug[pl.ds(st, tk), HEAD_DIM:2 * HEAD_DIM] = jnp.ones((tk, HEAD_DIM), BF16)
            if mode == "fox":
                c = _pick_lane(ct_ref[pl.ds(st, tk), :], h) * LOG2E
                hi, mid, lo = _split3(c)
                ext = jnp.where(lane < 3, 1.0, jnp.where(lane == 3, -hi, jnp.where(
                    lane == 4, -mid, jnp.where(lane == 5, -lo, 0.0))))
            else:
                ext = jnp.where(lane == (st + row) // MOBA_BLOCK, 1.0, 0.0)
            kaug[pl.ds(st, tk), HEAD_DIM:2 * HEAD_DIM] = ext.astype(BF16)
            return carry

        lax.fori_loop(0, seq // tk, build, 0)
        if mode == "moba":
            kmean_scr[...] = jnp.zeros_like(kmean_scr)
            for b in range(n_moba_blocks):
                kb = k_ref[b * MOBA_BLOCK:(b + 1) * MOBA_BLOCK, :].astype(F32)
                kmean_scr[b:b + 1, :] = jnp.mean(kb, axis=0, keepdims=True)

    q = q_ref[...]
    lane = lax.broadcasted_iota(jnp.int32, (tq, LANES), 1)
    if mode == "fox":
        start = pl.multiple_of(i * tq, tq)
        f = _pick_lane(ct_ref[pl.ds(start, tq), :], h) * LOG2E
        hi, mid, lo = _split3(f)
        ext = jnp.where(lane == 0, hi, jnp.where(lane == 1, mid, jnp.where(
            lane == 2, lo, jnp.where(lane < 6, 1.0, 0.0))))
    else:
        gate = lax.dot_general(q.astype(F32), kmean_scr[...], (((1,), (1,)), ((), ())),
                               preferred_element_type=F32, precision=lax.Precision.HIGHEST)
        row = lax.broadcasted_iota(jnp.int32, (tq, LANES), 0)
        own = (i * tq + row) // MOBA_BLOCK
        lane_f = lane.astype(F32)
        g = jnp.where(lane < own, gate, NEG_INF)
        keep = lane == own
        for _ in range(min(MOBA_TOPK, n_moba_blocks)):
            m = jnp.max(g, axis=-1, keepdims=True)
            idx = jnp.min(jnp.where(g == m, lane_f, float(LANES)), axis=-1, keepdims=True)
            hit = lane_f == idx
            keep = jnp.logical_or(keep, jnp.logical_and(hit, m > NEG_INF))
            g = jnp.where(hit, NEG_INF, g)
        ext = jnp.where(keep, 0.0, NEG_BIG)
    q_aug = jnp.concatenate([q, ext.astype(BF16)], axis=1)

    def tile(j, diag_off):
        st = pl.multiple_of(j * tk, tk)
        s = lax.dot_general(q_aug, kaug[pl.ds(st, tk), :], (((1,), (1,)), ((), ())),
                            preferred_element_type=F32)
        if diag_off is not None:
            r2 = lax.broadcasted_iota(jnp.int32, (tq, tk), 0)
            c2 = lax.broadcasted_iota(jnp.int32, (tq, tk), 1) + diag_off
            ok = c2 <= r2
            if mode == "moba":
                ok = jnp.logical_or(ok, c2 // MOBA_BLOCK != r2 // MOBA_BLOCK)
            s = jnp.where(ok, s, NEG_INF)
        first = diag_off == 0
        if first:
            m_new = jnp.max(s, axis=-1, keepdims=True)
        else:
            m_old = m_scr[...]
            m_new = jnp.maximum(m_old, jnp.max(s, axis=-1, keepdims=True))
        p = jnp.exp2(s - m_new).astype(BF16)
        pv = jnp.dot(p, vaug[pl.ds(st, tk), :], preferred_element_type=F32)
        if first:
            acc_scr[...] = pv
        else:
            acc_scr[...] = jnp.exp2(m_old - m_new) * acc_scr[...] + pv
        m_scr[...] = m_new

    for dd in range(per_q):
        tile(i * per_q + dd, dd * tk)

    def body(j, carry):
        tile(j, None)
        return carry

    lax.fori_loop(0, i * per_q, body, 0)
    o_ref[...] = (acc_scr[:, 0:HEAD_DIM] / acc_scr[:, HEAD_DIM:2 * HEAD_DIM]).astype(o_ref.dtype)


def attention(proj, batch, seq, heads, q_col, k_col, v_col, cum_t=None, tq=1024, tk=512):
    n = proj.shape[0]
    tq = _tile(seq, tq)
    tk = _tile(tq, tk)
    nq = seq // tq
    mode = "moba" if cum_t is None else "fox"
    assert tk % MOBA_BLOCK == 0 and seq // MOBA_BLOCK <= LANES
    in_specs = [
        pl.BlockSpec((tq, HEAD_DIM), lambda b, h, i: (b * nq + i, q_col + h)),
        pl.BlockSpec((seq, HEAD_DIM), lambda b, h, i: (b, k_col + h)),
        pl.BlockSpec((seq, HEAD_DIM), lambda b, h, i: (b, v_col + h)),
    ]
    args = [proj, proj, proj]
    scratch = [pltpu.VMEM((seq, 2 * HEAD_DIM), BF16), pltpu.VMEM((seq, 2 * HEAD_DIM), BF16),
               pltpu.VMEM((tq, 1), F32), pltpu.VMEM((tq, 2 * HEAD_DIM), F32)]
    if mode == "fox":
        in_specs.append(pl.BlockSpec((None, seq, heads), lambda b, h, i: (b, 0, 0)))
        args.append(cum_t)
    else:
        scratch.append(pltpu.VMEM((LANES, HEAD_DIM), F32))
    return pl.pallas_call(
        functools.partial(_attn_kernel, mode=mode, tq=tq, tk=tk, seq=seq),
        grid=(batch, heads, nq),
        in_specs=in_specs,
        out_specs=pl.BlockSpec((tq, HEAD_DIM), lambda b, h, i: (b * nq + i, h)),
        out_shape=jax.ShapeDtypeStruct((n, heads * HEAD_DIM), BF16),
        scratch_shapes=scratch,
        compiler_params=_params(("arbitrary", "arbitrary", "arbitrary")),
        name=mode + "_attention",
    )(*args)


def _out_proj_kernel(mo_ref, fo_ref, g1_ref, g2_ref, w_ref, x_ref, mod_ref, o_ref, *, gate_row):
    half = mo_ref.shape[1]
    m1 = _rms(mo_ref[...].astype(F32), g1_ref[...]).astype(BF16)
    m2 = _rms(fo_ref[...].astype(F32), g2_ref[...]).astype(BF16)
    acc = jnp.dot(m1, w_ref[0:half, :], preferred_element_type=F32)
    acc = acc + jnp.dot(m2, w_ref[half:2 * half, :], preferred_element_type=F32)
    o_ref[...] = x_ref[...] + mod_ref[gate_row:gate_row + 1, :] * acc


def out_proj(moba_o, fox_o, g1, g2, w_o, x2, mod, seq, gate_row, tm=512):
    n, d = x2.shape
    half = moba_o.shape[1]
    tm = _tile(seq, tm)
    per_b = seq // tm
    return pl.pallas_call(
        functools.partial(_out_proj_kernel, gate_row=gate_row),
        grid=(n // tm,),
        in_specs=[
            pl.BlockSpec((tm, half), lambda i: (i, 0)),
            pl.BlockSpec((tm, half), lambda i: (i, 0)),
            pl.BlockSpec((1, half), lambda i: (0, 0)),
            pl.BlockSpec((1, half), lambda i: (0, 0)),
            pl.BlockSpec((2 * half, d), lambda i: (0, 0)),
            pl.BlockSpec((tm, d), lambda i: (i, 0)),
            pl.BlockSpec((None, 6, d), lambda i: (i // per_b, 0, 0)),
        ],
        out_specs=pl.BlockSpec((tm, d), lambda i: (i, 0)),
        out_shape=jax.ShapeDtypeStruct((n, d), F32),
        compiler_params=_params(("arbitrary",)),
        name="out_proj",
    )(moba_o, fox_o, g1.reshape(1, half), g2.reshape(1, half), w_o, x2, mod)


def _swiglu_tile(h, wg, wu):
    a = jnp.dot(h, wg, preferred_element_type=F32)
    b = jnp.dot(h, wu, preferred_element_type=F32)
    return a * jax.nn.sigmoid(a) * b


def _ffn_kernel(h_ref, wg_ref, wu_ref, wd_ref, x_ref, mod_ref, o_ref, *, gate_row):
    f = pl.program_id(1)
    t = _swiglu_tile(h_ref[...], wg_ref[...], wu_ref[...]).astype(BF16)

    @pl.when(f == 0)
    def _():
        o_ref[...] = jnp.dot(t, wd_ref[...], preferred_element_type=F32)

    @pl.when(f > 0)
    def _():
        o_ref[...] += jnp.dot(t, wd_ref[...], preferred_element_type=F32)

    @pl.when(f == pl.num_programs(1) - 1)
    def _():
        o_ref[...] = x_ref[...] + mod_ref[gate_row:gate_row + 1, :] * o_ref[...]


def ffn_dense(h, wg, wu, wd, x2, mod, seq, gate_row, tm=512, tf=512):
    n, d = x2.shape
    ff = wg.shape[1]
    tm = _tile(seq, tm)
    tf = _tile(ff, tf)
    per_b = seq // tm
    return pl.pallas_call(
        functools.partial(_ffn_kernel, gate_row=gate_row),
        grid=(n // tm, ff // tf),
        in_specs=[
            pl.BlockSpec((tm, d), lambda i, f: (i, 0)),
            pl.BlockSpec((d, tf), lambda i, f: (0, f)),
            pl.BlockSpec((d, tf), lambda i, f: (0, f)),
            pl.BlockSpec((tf, d), lambda i, f: (f, 0)),
            pl.BlockSpec((tm, d), lambda i, f: (i, 0)),
            pl.BlockSpec((None, 6, d), lambda i, f: (i // per_b, 0, 0)),
        ],
        out_specs=pl.BlockSpec((tm, d), lambda i, f: (i, 0)),
        out_shape=jax.ShapeDtypeStruct((n, d), F32),
        compiler_params=_params(("arbitrary", "arbitrary")),
        name="ffn_dense",
    )(h, wg, wu, wd, x2, mod)


def _route_plan(route, n_experts, tm):
    n = route.shape[0]
    pairs = TOP_K * n
    e_flat = route[:, :TOP_K].astype(jnp.int32).reshape(pairs)
    onehot = (e_flat[:, None] == jnp.arange(n_experts, dtype=jnp.int32)[None, :]).astype(jnp.int32)
    csum = jnp.cumsum(onehot, axis=0)
    counts = csum[-1]
    gend = jnp.cumsum(counts)
    gstart = gend - counts
    rank = jnp.sum(csum * onehot, axis=1) - 1
    pos = (jnp.sum(gstart[None, :] * onehot, axis=1) + rank).astype(jnp.int32)

    n_tiles = pairs // tm
    n_items = n_tiles + n_experts - 1
    first_row = jnp.arange(n_tiles, dtype=jnp.int32) * tm
    e_lo = jnp.sum((gend[None, :] <= first_row[:, None]).astype(jnp.int32), axis=1)
    e_hi = jnp.sum((gend[None, :] <= (first_row + tm - 1)[:, None]).astype(jnp.int32), axis=1)
    per_tile = e_hi - e_lo + 1
    item_end = jnp.cumsum(per_tile)
    item_start = item_end - per_tile
    total = item_end[-1]
    w = jnp.arange(n_items, dtype=jnp.int32)
    wi_tile = jnp.minimum(jnp.sum((item_end[None, :] <= w[:, None]).astype(jnp.int32), axis=1),
                          n_tiles - 1)
    wi_expert = jnp.minimum(e_lo[wi_tile] + (w - item_start[wi_tile]), e_hi[wi_tile])
    valid = w < total
    wi_first = jnp.logical_and(valid, w == item_start[wi_tile]).astype(jnp.int32)
    bounds = jnp.concatenate([gstart, gend[-1:]]).astype(jnp.int32)
    return pos, (wi_tile.astype(jnp.int32), wi_expert.astype(jnp.int32),
                 valid.astype(jnp.int32), wi_first, bounds)


def _dispatch_kernel(pos_ref, h_ref, xs_ref, sem):
    i = pl.program_id(0)
    tm = h_ref.shape[0]

    def issue(r, carry):
        for k in range(TOP_K):
            dst = pos_ref[(i * tm + r) * TOP_K + k]
            pltpu.make_async_copy(h_ref.at[r], xs_ref.at[dst], sem).start()
        return carry

    lax.fori_loop(0, tm, issue, 0)
    for k in range(TOP_K):
        pltpu.make_async_copy(h_ref, xs_ref.at[pl.ds(0, tm)], sem).wait()


def moe_dispatch(h3, pos, tm=512):
    n, sub, lanes = h3.shape
    tm = _tile(n, tm)
    return pl.pallas_call(
        _dispatch_kernel,
        grid_spec=pltpu.PrefetchScalarGridSpec(
            num_scalar_prefetch=1,
            grid=(n // tm,),
            in_specs=[pl.BlockSpec((tm, sub, lanes), lambda i, pos: (i, 0, 0))],
            out_specs=pl.BlockSpec(memory_space=pl.ANY),
            scratch_shapes=[pltpu.SemaphoreType.DMA(())],
        ),
        out_shape=jax.ShapeDtypeStruct((TOP_K * n, sub, lanes), h3.dtype),
        compiler_params=_params(("arbitrary",), has_side_effects=True),
        name="moe_dispatch",
    )(pos, h3)


def _moe_ffn_kernel(tile_ref, exp_ref, valid_ref, first_ref, bounds_ref,
                    xs_ref, wg_ref, wu_ref, wd_ref, o_ref):
    w = pl.program_id(0)
    f = pl.program_id(1)
    tm = xs_ref.shape[0]

    @pl.when(valid_ref[w] == 1)
    def _():
        e = exp_ref[w]
        rows = tile_ref[w] * tm + lax.broadcasted_iota(jnp.int32, (tm, 1), 0)
        mine = jnp.logical_and(rows >= bounds_ref[e], rows < bounds_ref[e + 1])
        t = _swiglu_tile(xs_ref[...], wg_ref[...].astype(BF16), wu_ref[...].astype(BF16))
        t = jnp.where(mine, t, 0.0).astype(BF16)
        wd = wd_ref[...].astype(BF16)
        fresh = jnp.logical_and(first_ref[w] == 1, f == 0)

        @pl.when(fresh)
        def _():
            o_ref[...] = jnp.dot(t, wd, preferred_element_type=F32)

        @pl.when(jnp.logical_not(fresh))
        def _():
            o_ref[...] += jnp.dot(t, wd, preferred_element_type=F32)


def moe_ffn(xs, plan, wg, wu, wd, tm, tf=256):
    rows, d = xs.shape
    n_experts, _, ff = wg.shape
    tf = _tile(ff, tf)
    n_items = rows // tm + n_experts - 1
    return pl.pallas_call(
        _moe_ffn_kernel,
        grid_spec=pltpu.PrefetchScalarGridSpec(
            num_scalar_prefetch=5,
            grid=(n_items, ff // tf),
            in_specs=[
                pl.BlockSpec((tm, d), lambda w, f, tl, ex, va, fi, bo: (tl[w], 0)),
                pl.BlockSpec((None, d, tf), lambda w, f, tl, ex, va, fi, bo: (ex[w], 0, f)),
                pl.BlockSpec((None, d, tf), lambda w, f, tl, ex, va, fi, bo: (ex[w], 0, f)),
                pl.BlockSpec((None, tf, d), lambda w, f, tl, ex, va, fi, bo: (ex[w], f, 0)),
            ],
            out_specs=pl.BlockSpec((tm, d), lambda w, f, tl, ex, va, fi, bo: (tl[w], 0)),
        ),
        out_shape=jax.ShapeDtypeStruct((rows, d), F32),
        compiler_params=_params(("arbitrary", "arbitrary")),
        name="moe_ffn",
    )(*plan, xs, wg, wu, wd)


def _combine_kernel(pos_ref, ys_ref, route_ref, x_ref, mod_ref, g_ref, o_ref, buf, sem,
                    *, gate_row, final):
    i = pl.program_id(0)
    tm = x_ref.shape[0]

    def issue(r, carry):
        for k in range(TOP_K):
            src = pos_ref[(i * tm + r) * TOP_K + k]
            pltpu.make_async_copy(ys_ref.at[pl.ds(src, 1), :], buf.at[k, pl.ds(r, 1), :], sem).start()
        return carry

    lax.fori_loop(0, tm, issue, 0)
    for k in range(TOP_K):
        pltpu.make_async_copy(ys_ref.at[pl.ds(0, tm), :], buf.at[k], sem).wait()
    route = route_ref[...]
    y = buf[0] * route[:, TOP_K:TOP_K + 1]
    for k in range(1, TOP_K):
        y = y + buf[k] * route[:, TOP_K + k:TOP_K + k + 1]
    x_new = x_ref[...] + mod_ref[gate_row:gate_row + 1, :] * y
    o_ref[...] = _rms(x_new, g_ref[...]) if final else x_new


def moe_combine(ys, pos, route, x2, mod, seq, gate_row, final_g=None, tm=256):
    n, d = x2.shape
    tm = _tile(seq, tm)
    per_b = seq // tm
    g = jnp.ones((1, d), F32) if final_g is None else final_g.reshape(1, d)
    return pl.pallas_call(
        functools.partial(_combine_kernel, gate_row=gate_row, final=final_g is not None),
        grid_spec=pltpu.PrefetchScalarGridSpec(
            num_scalar_prefetch=1,
            grid=(n // tm,),
            in_specs=[
                pl.BlockSpec(memory_space=pl.ANY),
                pl.BlockSpec((tm, LANES), lambda i, pos: (i, 0)),
                pl.BlockSpec((tm, d), lambda i, pos: (i, 0)),
                pl.BlockSpec((None, 6, d), lambda i, pos: (i // per_b, 0, 0)),
                pl.BlockSpec((1, d), lambda i, pos: (0, 0)),
            ],
            out_specs=pl.BlockSpec((tm, d), lambda i, pos: (i, 0)),
            scratch_shapes=[pltpu.VMEM((TOP_K, tm, d), F32), pltpu.SemaphoreType.DMA(())],
        ),
        out_shape=jax.ShapeDtypeStruct((n, d), F32),
        compiler_params=_params(("arbitrary",)),
        name="moe_combine",
    )(pos, ys, route, x2, mod, g)


def ffn_moe(h, route, wg, wu, wd, x2, mod, seq, gate_row, final_g=None, tm=1024):
    n, d = x2.shape
    n_experts = wg.shape[0]
    tm = _tile(TOP_K * n, tm)
    pos, plan = _route_plan(route, n_experts, tm)
    xs3 = moe_dispatch(h.reshape(n, d // LANES, LANES), pos)
    ys = moe_ffn(xs3.reshape(TOP_K * n, d), plan, wg, wu, wd, tm)
    return moe_combine(ys, pos, route, x2, mod, seq, gate_row, final_g)


def _rope_tables(seq):
    inv = ROPE_THETA ** (-jnp.arange(0, HEAD_DIM, 2, dtype=F32) / HEAD_DIM)
    ang = jnp.arange(seq, dtype=F32)[:, None] * inv[None, :]
    cos, sin = jnp.cos(ang), jnp.sin(ang)
    return jnp.concatenate([cos, cos], axis=-1), jnp.concatenate([-sin, sin], axis=-1)


def kernel(x, c, attn_norm_g, ada_w, ada_b, w_in, forget_b, moba_out_g, fox_out_g, w_o, ffn_norm_g,
           dense_w_gate, dense_w_up, dense_w_down, router_w, moe_w_gate, moe_w_up, moe_w_down,
           final_norm_g):
    batch, seq, d = x.shape
    depth = attn_norm_g.shape[0]
    moba_w = moba_out_g.shape[1]
    fox_w = fox_out_g.shape[1]
    moba_h = moba_w // HEAD_DIM
    fox_h = fox_w // HEAD_DIM
    qkv_w = 3 * moba_w + 3 * fox_w
    assert moba_w == fox_w
    kinds = (ROPE_SCALED, ROPE, PLAIN, SCALED, PLAIN, PLAIN)
    hb = HEAD_DIM

    cos_t, sin_t = _rope_tables(seq)
    mods = ada_modulation(c, ada_w, ada_b)
    x2 = x.reshape(batch * seq, d)
    out = None

    for l in range(depth):
        mod = mods[l]
        h = norm_mod(x2, attn_norm_g[l], mod, seq, sh_row=0)
        w_qkv = w_in[l][:, :qkv_w].astype(BF16)
        wz = jnp.zeros((d, LANES), BF16).at[:, :fox_h].set(w_in[l][:, qkv_w:].astype(BF16))
        proj, fz = in_proj(h, w_qkv, wz, cos_t, sin_t, seq, kinds, tn=moba_w)
        cum_t = logf_cumsum(fz[:, :fox_h].reshape(batch, seq, fox_h), forget_b[l])
        moba_o = attention(proj, batch, seq, moba_h, q_col=0, k_col=moba_w // hb,
                           v_col=2 * moba_w // hb)
        fc = 3 * moba_w // hb
        fox_o = attention(proj, batch, seq, fox_h, q_col=fc, k_col=fc + fox_w // hb,
                          v_col=fc + 2 * fox_w // hb, cum_t=cum_t)
        x2 = out_proj(moba_o, fox_o, moba_out_g[l], fox_out_g[l], w_o[l].astype(BF16), x2, mod,
                      seq, gate_row=2)
        i = l // 2
        last = l == depth - 1
        if l % 2 == 0:
            h = norm_mod(x2, ffn_norm_g[l], mod, seq, sh_row=3)
            x2 = ffn_dense(h, dense_w_gate[i].astype(BF16), dense_w_up[i].astype(BF16),
                           dense_w_down[i].astype(BF16), x2, mod, seq, gate_row=5)
        else:
            h, route = norm_mod(x2, ffn_norm_g[l], mod, seq, sh_row=3, router_w=router_w[i])
            x2 = ffn_moe(h, route, moe_w_gate[i], moe_w_up[i], moe_w_down[i], x2, mod, seq,
                         gate_row=5, final_g=final_norm_g if last else None)
            if last:
                out = x2
    if out is None:
        out = final_norm(x2, final_norm_g)
    return out.reshape(batch, seq, d)
```

```python
import functools
import math

import jax
import jax.numpy as jnp
from jax import lax
from jax.experimental import pallas as pl
from jax.experimental.pallas import tpu as pltpu

F32 = jnp.float32
BF16 = jnp.bfloat16

HEAD_DIM = 128
MOBA_BLOCK = 256
MOBA_TOPK = 3
ROPE_THETA = 10000.0
EPS = 1e-6
TOP_K = 2
LANES = 128
VMEM_LIMIT = 56 * 1024 * 1024

NEG_INF = float("-inf")
NEG_BIG = -(2.0 ** 100)
LOG2E = math.log2(math.e)
Q_SCALE = HEAD_DIM ** -0.5 * LOG2E


def _params(semantics, **kw):
    return pltpu.CompilerParams(dimension_semantics=semantics, vmem_limit_bytes=VMEM_LIMIT, **kw)


def _tile(n, want):
    t = min(n, want)
    assert n % t == 0, (n, want)
    return t


def _ada_kernel(c_ref, w_ref, b_ref, o_ref):
    c = c_ref[...]
    c_act = c * jax.nn.sigmoid(c)
    o_ref[...] = jnp.dot(c_act, w_ref[...], preferred_element_type=F32,
                         precision=lax.Precision.HIGHEST) + b_ref[...]


def ada_modulation(c, ada_w, ada_b):
    n_layers, d, d6 = ada_w.shape
    b = c.shape[0]
    rows = -(-b // 8) * 8
    c_pad = jnp.zeros((rows, d), F32).at[:b].set(c)
    tn = _tile(d6, 1024)
    out = pl.pallas_call(
        _ada_kernel,
        grid=(n_layers, d6 // tn),
        in_specs=[
            pl.BlockSpec((rows, d), lambda l, j: (0, 0)),
            pl.BlockSpec((None, d, tn), lambda l, j: (l, 0, j)),
            pl.BlockSpec((None, 1, tn), lambda l, j: (l, 0, j)),
        ],
        out_specs=pl.BlockSpec((None, rows, tn), lambda l, j: (l, 0, j)),
        out_shape=jax.ShapeDtypeStruct((n_layers, rows, d6), F32),
        compiler_params=_params(("arbitrary", "arbitrary")),
        name="ada_modulation",
    )(c_pad, ada_w, ada_b.reshape(n_layers, 1, d6))
    return out[:, :b].reshape(n_layers, b, 6, d)


def _rms(x, g):
    return x * lax.rsqrt(jnp.mean(x * x, axis=-1, keepdims=True) + EPS) * g


def _norm_mod_kernel(x_ref, g_ref, mod_ref, o_ref, *, sh_row):
    y = _rms(x_ref[...], g_ref[...])
    sh = mod_ref[sh_row:sh_row + 1, :]
    sc = mod_ref[sh_row + 1:sh_row + 2, :]
    o_ref[...] = (y * (1.0 + sc) + sh).astype(o_ref.dtype)


def _top2_route(logits, n_experts):
    lane = lax.broadcasted_iota(jnp.int32, logits.shape, 1).astype(F32)
    lg = jnp.where(lane < n_experts, logits, NEG_INF)
    m1 = jnp.max(lg, axis=-1, keepdims=True)
    i1 = jnp.min(jnp.where(lg == m1, lane, float(LANES)), axis=-1, keepdims=True)
    lg2 = jnp.where(lane == i1, NEG_INF, lg)
    m2 = jnp.max(lg2, axis=-1, keepdims=True)
    i2 = jnp.min(jnp.where(lg2 == m2, lane, float(LANES)), axis=-1, keepdims=True)
    e2 = jnp.exp(m2 - m1)
    w1 = 1.0 / (1.0 + e2)
    w2 = e2 / (1.0 + e2)
    return jnp.where(lane == 0.0, i1, jnp.where(lane == 1.0, i2,
                     jnp.where(lane == 2.0, w1, jnp.where(lane == 3.0, w2, 0.0))))


def norm_mod(x2, g, mod, seq, sh_row, tm=512):
    n, d = x2.shape
    tm = _tile(seq, tm)
    per_b = seq // tm
    return pl.pallas_call(
        functools.partial(_norm_mod_kernel, sh_row=sh_row),
        grid=(n // tm,),
        in_specs=[
            pl.BlockSpec((tm, d), lambda i: (i, 0)),
            pl.BlockSpec((1, d), lambda i: (0, 0)),
            pl.BlockSpec((None, 6, d), lambda i: (i // per_b, 0, 0)),
        ],
        out_specs=pl.BlockSpec((tm, d), lambda i: (i, 0)),
        out_shape=jax.ShapeDtypeStruct((n, d), BF16),
        compiler_params=_params(("arbitrary",)),
        name="norm_mod",
    )(x2, g.reshape(1, d), mod)


def _final_norm_kernel(x_ref, g_ref, o_ref):
    o_ref[...] = _rms(x_ref[...], g_ref[...])


def final_norm(x2, g, tm=512):
    n, d = x2.shape
    tm = _tile(n, tm)
    return pl.pallas_call(
        _final_norm_kernel,
        grid=(n // tm,),
        in_specs=[pl.BlockSpec((tm, d), lambda i: (i, 0)),
                  pl.BlockSpec((1, d), lambda i: (0, 0))],
        out_specs=pl.BlockSpec((tm, d), lambda i: (i, 0)),
        out_shape=jax.ShapeDtypeStruct((n, d), F32),
        compiler_params=_params(("arbitrary",)),
        name="final_norm",
    )(x2, g.reshape(1, d))


PLAIN, ROPE, ROPE_SCALED, SCALED = 0, 1, 2, 3


def _in_proj_kernel(h_ref, w_ref, wz_ref, cos_ref, sin_ref, o_ref, fz_ref, *, kinds):
    j = pl.program_id(1)
    tn = w_ref.shape[1]

    def is_kind(kind):
        hits = [jj for jj, k in enumerate(kinds) if k == kind]
        cond = j == hits[0]
        for jj in hits[1:]:
            cond = jnp.logical_or(cond, j == jj)
        return cond

    for kind in sorted(set(kinds)):
        @pl.when(is_kind(kind))
        def _(kind=kind):
            acc = jnp.dot(h_ref[...], w_ref[...], preferred_element_type=F32)
            mult = Q_SCALE if kind in (ROPE_SCALED, SCALED) else 1.0
            if kind in (ROPE, ROPE_SCALED):
                c = cos_ref[...] * mult
                s = sin_ref[...] * mult
                for hh in range(tn // HEAD_DIM):
                    t = acc[:, hh * HEAD_DIM:(hh + 1) * HEAD_DIM]
                    rot = pltpu.roll(t, HEAD_DIM // 2, 1)
                    o_ref[:, hh * HEAD_DIM:(hh + 1) * HEAD_DIM] = (t * c + rot * s).astype(o_ref.dtype)
            elif kind == SCALED:
                o_ref[...] = (acc * mult).astype(o_ref.dtype)
            else:
                o_ref[...] = acc.astype(o_ref.dtype)

    @pl.when(j == 0)
    def _():
        fz_ref[...] = jnp.dot(h_ref[...], wz_ref[...], preferred_element_type=F32)


def in_proj(h, w_qkv, wz, cos_t, sin_t, seq, kinds, tm=1024, tn=1024):
    n, d = h.shape
    width = w_qkv.shape[1]
    tm = _tile(seq, tm)
    assert width == tn * len(kinds)
    per_b = seq // tm
    return pl.pallas_call(
        functools.partial(_in_proj_kernel, kinds=tuple(kinds)),
        grid=(n // tm, width // tn),
        in_specs=[
            pl.BlockSpec((tm, d), lambda i, j: (i, 0)),
            pl.BlockSpec((d, tn), lambda i, j: (0, j)),
            pl.BlockSpec((d, LANES), lambda i, j: (0, 0)),
            pl.BlockSpec((tm, HEAD_DIM), lambda i, j: (i % per_b, 0)),
            pl.BlockSpec((tm, HEAD_DIM), lambda i, j: (i % per_b, 0)),
        ],
        out_specs=[pl.BlockSpec((tm, tn), lambda i, j: (i, j)),
                   pl.BlockSpec((tm, LANES), lambda i, j: (i, 0))],
        out_shape=[jax.ShapeDtypeStruct((n, width), BF16),
                   jax.ShapeDtypeStruct((n, LANES), F32)],
        compiler_params=_params(("arbitrary", "arbitrary")),
        name="in_proj",
    )(h, w_qkv, wz, cos_t, sin_t)


def _logf_cumsum_kernel(z_ref, b_ref, o_ref):
    z = z_ref[...] + b_ref[...]
    lf = jnp.minimum(z, 0.0) - jnp.log1p(jnp.exp(-jnp.abs(z)))
    r = lax.broadcasted_iota(jnp.int32, (LANES, LANES), 0)
    c = lax.broadcasted_iota(jnp.int32, (LANES, LANES), 1)
    tri = jnp.where(c <= r, 1.0, 0.0).astype(F32)
    carry = jnp.zeros((1, z.shape[1]), F32)
    for ch in range(z.shape[0] // LANES):
        seg = lf[ch * LANES:(ch + 1) * LANES, :]
        cs = jnp.dot(tri, seg, preferred_element_type=F32, precision=lax.Precision.HIGHEST) + carry
        o_ref[ch * LANES:(ch + 1) * LANES, :] = cs
        carry = cs[LANES - 1:LANES, :]


def logf_cumsum(z, forget_b):
    b, s, h = z.shape
    return pl.pallas_call(
        _logf_cumsum_kernel,
        grid=(b,),
        in_specs=[pl.BlockSpec((None, s, h), lambda i: (i, 0, 0)),
                  pl.BlockSpec((1, h), lambda i: (0, 0))],
        out_specs=pl.BlockSpec((None, s, h), lambda i: (i, 0, 0)),
        out_shape=jax.ShapeDtypeStruct((b, s, h), F32),
        compiler_params=_params(("arbitrary",)),
        name="logf_cumsum",
    )(z, forget_b.reshape(1, h))


def _split3(f):
    hi = f.astype(BF16).astype(F32)
    r = f - hi
    mid = r.astype(BF16).astype(F32)
    lo = (r - mid).astype(BF16).astype(F32)
    return hi, mid, lo


def _pick_lane(x, idx):
    lane = lax.broadcasted_iota(jnp.int32, x.shape, 1)
    return jnp.sum(jnp.where(lane == idx, x, 0.0), axis=-1, keepdims=True)


def _attn_kernel(*refs, mode, tq, tk, seq):
    if mode == "fox":
        q_ref, k_ref, v_ref, ct_ref, o_ref, kaug, vaug, qaug, s_a, s_b, bias_scr, m_scr, acc_scr = refs
    else:
        (q_ref, k_ref, v_ref, o_ref, kaug, vaug, qaug, s_a, s_b, bias_scr, m_scr, acc_scr,
         kmean_scr, kmean3_scr) = refs
    h = pl.program_id(1)
    i = pl.program_id(2)
    n_moba_blocks = seq // MOBA_BLOCK

    @pl.when(jnp.logical_and(jnp.logical_and(pl.program_id(0) == 0, h == 0), i == 0))
    def _():
        for dd in range(2):
            r2 = lax.broadcasted_iota(jnp.int32, (tq, tk), 0)
            c2 = lax.broadcasted_iota(jnp.int32, (tq, tk), 1) + dd * tk
            ok = c2 <= r2
            if mode == "moba":
                ok = jnp.logical_or(ok, c2 // MOBA_BLOCK != r2 // MOBA_BLOCK)
            bias_scr[dd] = jnp.where(ok, 0.0, NEG_INF)

    @pl.when(i == 0)
    def _():
        lane = lax.broadcasted_iota(jnp.int32, (tk, LANES), 1)
        row = lax.broadcasted_iota(jnp.int32, (tk, LANES), 0)

        def build(cidx, carry):
            st = pl.multiple_of(cidx * tk, tk)
            kaug[pl.ds(st, tk), 0:HEAD_DIM] = k_ref[pl.ds(st, tk), :]
            vaug[pl.ds(st, tk), 0:HEAD_DIM] = v_ref[pl.ds(st, tk), :]
            vaug[pl.ds(st, tk), HEAD_DIM:2 * HEAD_DIM] = jnp.ones((tk, HEAD_DIM), BF16)
            if mode == "fox":
                c = _pick_lane(ct_ref[pl.ds(st, tk), :], h) * LOG2E
                hi, mid, lo = _split3(c)
                ext = jnp.where(lane < 3, 1.0, jnp.where(lane == 3, -hi, jnp.where(
                    lane == 4, -mid, jnp.where(lane == 5, -lo, 0.0))))
            else:
                ext = jnp.where(lane == (st + row) // MOBA_BLOCK, 1.0, 0.0)
            kaug[pl.ds(st, tk), HEAD_DIM:2 * HEAD_DIM] = ext.astype(BF16)
            return carry

        lax.fori_loop(0, seq // tk, build, 0)
        if mode == "moba":
            kmean_scr[...] = jnp.zeros_like(kmean_scr)
            for b in range(n_moba_blocks):
                kb = k_ref[b * MOBA_BLOCK:(b + 1) * MOBA_BLOCK, :].astype(F32)
                kmean_scr[b:b + 1, :] = jnp.mean(kb, axis=0, keepdims=True)
            for piece, part in enumerate(_split3(kmean_scr[...])):
                kmean3_scr[piece * LANES:(piece + 1) * LANES, :] = part.astype(BF16)

    lane = lax.broadcasted_iota(jnp.int32, (MOBA_BLOCK, LANES), 1)
    for rb in range(tq // MOBA_BLOCK):
        rows = pl.ds(rb * MOBA_BLOCK, MOBA_BLOCK)
        q = q_ref[rows, :]
        if mode == "fox":
            start = pl.multiple_of(i * tq + rb * MOBA_BLOCK, MOBA_BLOCK)
            f = _pick_lane(ct_ref[pl.ds(start, MOBA_BLOCK), :], h) * LOG2E
            hi, mid, lo = _split3(f)
            ext = jnp.where(lane == 0, hi, jnp.where(lane == 1, mid, jnp.where(
                lane == 2, lo, jnp.where(lane < 6, 1.0, 0.0))))
        else:
            g3 = lax.dot_general(q, kmean3_scr[...], (((1,), (1,)), ((), ())),
                                 preferred_element_type=F32)
            gate = g3[:, 0:LANES] + g3[:, LANES:2 * LANES] + g3[:, 2 * LANES:3 * LANES]
            own = i * (tq // MOBA_BLOCK) + rb
            lane_f = lane.astype(F32)
            g = jnp.where(lane < own, gate, NEG_INF)
            keep = lane == own
            for _ in range(min(MOBA_TOPK, n_moba_blocks)):
                m = jnp.max(g, axis=-1, keepdims=True)
                idx = jnp.min(jnp.where(g == m, lane_f, float(LANES)), axis=-1, keepdims=True)
                hit = lane_f == idx
                keep = jnp.logical_or(keep, jnp.logical_and(hit, m > NEG_INF))
                g = jnp.where(hit, NEG_INF, g)
            ext = jnp.where(keep, 0.0, NEG_BIG)
        qaug[rows, 0:HEAD_DIM] = q
        qaug[rows, HEAD_DIM:2 * HEAD_DIM] = ext.astype(BF16)

    def scores(j):
        st = pl.multiple_of(j * tk, tk)
        return lax.dot_general(qaug[...], kaug[pl.ds(st, tk), :], (((1,), (1,)), ((), ())),
                               preferred_element_type=F32)

    def update(s, j, diag_off):
        st = pl.multiple_of(j * tk, tk)
        if diag_off is not None:
            s = s + bias_scr[diag_off // tk]
        m_old = m_scr[...]
        m_new = jnp.maximum(m_old, jnp.max(s, axis=-1, keepdims=True))
        p = jnp.exp2(s - m_new).astype(BF16)
        pv = jnp.dot(p, vaug[pl.ds(st, tk), :], preferred_element_type=F32)
        acc_scr[...] = jnp.exp2(m_old - m_new) * acc_scr[...] + pv
        m_scr[...] = m_new

    m_scr[...] = jnp.full(m_scr.shape, NEG_INF, F32)
    acc_scr[...] = jnp.zeros(acc_scr.shape, F32)
    s_a[...] = scores(0)

    def body(jj, carry):
        s_b[...] = scores(2 * jj + 1)
        update(s_a[...], 2 * jj, None)
        s_a[...] = scores(2 * jj + 2)
        update(s_b[...], 2 * jj + 1, None)
        return carry

    lax.fori_loop(0, i, body, 0)
    s_b[...] = scores(2 * i + 1)
    update(s_a[...], 2 * i, 0)
    update(s_b[...], 2 * i + 1, tk)
    o_ref[...] = (acc_scr[:, 0:HEAD_DIM] / acc_scr[:, HEAD_DIM:2 * HEAD_DIM]).astype(o_ref.dtype)


def attention(proj, batch, seq, heads, q_col, k_col, v_col, cum_t=None, tq=1024):
    n = proj.shape[0]
    tq = _tile(seq, tq)
    tk = tq // 2
    nq = seq // tq
    mode = "moba" if cum_t is None else "fox"
    assert tk % MOBA_BLOCK == 0 and seq // MOBA_BLOCK <= LANES
    in_specs = [
        pl.BlockSpec((tq, HEAD_DIM), lambda b, h, i: (b * nq + i, q_col + h)),
        pl.BlockSpec((seq, HEAD_DIM), lambda b, h, i: (b, k_col + h)),
        pl.BlockSpec((seq, HEAD_DIM), lambda b, h, i: (b, v_col + h)),
    ]
    args = [proj, proj, proj]
    scratch = [pltpu.VMEM((seq, 2 * HEAD_DIM), BF16), pltpu.VMEM((seq, 2 * HEAD_DIM), BF16),
               pltpu.VMEM((tq, 2 * HEAD_DIM), BF16),
               pltpu.VMEM((tq, tk), F32), pltpu.VMEM((tq, tk), F32),
               pltpu.VMEM((2, tq, tk), F32),
               pltpu.VMEM((tq, 1), F32), pltpu.VMEM((tq, 2 * HEAD_DIM), F32)]
    if mode == "fox":
        in_specs.append(pl.BlockSpec((None, seq, heads), lambda b, h, i: (b, 0, 0)))
        args.append(cum_t)
    else:
        scratch += [pltpu.VMEM((LANES, HEAD_DIM), F32), pltpu.VMEM((3 * LANES, HEAD_DIM), BF16)]
    return pl.pallas_call(
        functools.partial(_attn_kernel, mode=mode, tq=tq, tk=tk, seq=seq),
        grid=(batch, heads, nq),
        in_specs=in_specs,
        out_specs=pl.BlockSpec((tq, HEAD_DIM), lambda b, h, i: (b * nq + i, h)),
        out_shape=jax.ShapeDtypeStruct((n, heads * HEAD_DIM), BF16),
        scratch_shapes=scratch,
        compiler_params=_params(("arbitrary", "arbitrary", "arbitrary")),
        name=mode + "_attention",
    )(*args)


def _out_proj_kernel(*refs, gate_row, sh_row, n_experts):
    if n_experts:
        (mo_ref, fo_ref, g1_ref, g2_ref, w_ref, x_ref, mod_ref, gf_ref, wr_ref,
         o_ref, h_ref, route_ref) = refs
    else:
        mo_ref, fo_ref, g1_ref, g2_ref, w_ref, x_ref, mod_ref, gf_ref, o_ref, h_ref = refs
    half = mo_ref.shape[1]
    m1 = _rms(mo_ref[...].astype(F32), g1_ref[...]).astype(BF16)
    m2 = _rms(fo_ref[...].astype(F32), g2_ref[...]).astype(BF16)
    acc = jnp.dot(m1, w_ref[0:half, :], preferred_element_type=F32)
    acc = acc + jnp.dot(m2, w_ref[half:2 * half, :], preferred_element_type=F32)
    x_new = x_ref[...] + mod_ref[gate_row:gate_row + 1, :] * acc
    o_ref[...] = x_new
    sh = mod_ref[sh_row:sh_row + 1, :]
    sc = mod_ref[sh_row + 1:sh_row + 2, :]
    h = _rms(x_new, gf_ref[...]) * (1.0 + sc) + sh
    h_ref[...] = h.astype(h_ref.dtype)
    if n_experts:
        logits = jnp.dot(h, wr_ref[...], preferred_element_type=F32, precision=lax.Precision.HIGHEST)
        route_ref[...] = _top2_route(logits, n_experts)


def out_proj(moba_o, fox_o, g1, g2, w_o, x2, mod, ffn_g, seq, gate_row, sh_row, router_w=None, tm=512):
    n, d = x2.shape
    half = moba_o.shape[1]
    tm = _tile(seq, tm)
    per_b = seq // tm
    n_experts = 0 if router_w is None else router_w.shape[1]
    in_specs = [
        pl.BlockSpec((tm, half), lambda i: (i, 0)),
        pl.BlockSpec((tm, half), lambda i: (i, 0)),
        pl.BlockSpec((1, half), lambda i: (0, 0)),
        pl.BlockSpec((1, half), lambda i: (0, 0)),
        pl.BlockSpec((2 * half, d), lambda i: (0, 0)),
        pl.BlockSpec((tm, d), lambda i: (i, 0)),
        pl.BlockSpec((None, 6, d), lambda i: (i // per_b, 0, 0)),
        pl.BlockSpec((1, d), lambda i: (0, 0)),
    ]
    args = [moba_o, fox_o, g1.reshape(1, half), g2.reshape(1, half), w_o, x2, mod, ffn_g.reshape(1, d)]
    out_specs = [pl.BlockSpec((tm, d), lambda i: (i, 0)), pl.BlockSpec((tm, d), lambda i: (i, 0))]
    out_shape = [jax.ShapeDtypeStruct((n, d), F32), jax.ShapeDtypeStruct((n, d), BF16)]
    if n_experts:
        in_specs.append(pl.BlockSpec((d, LANES), lambda i: (0, 0)))
        args.append(jnp.zeros((d, LANES), F32).at[:, :n_experts].set(router_w))
        out_specs.append(pl.BlockSpec((tm, LANES), lambda i: (i, 0)))
        out_shape.append(jax.ShapeDtypeStruct((n, LANES), F32))
    return pl.pallas_call(
        functools.partial(_out_proj_kernel, gate_row=gate_row, sh_row=sh_row, n_experts=n_experts),
        grid=(n // tm,),
        in_specs=in_specs,
        out_specs=out_specs,
        out_shape=out_shape,
        compiler_params=_params(("arbitrary",)),
        name="out_proj",
    )(*args)


def _swiglu_tile(h, wg, wu):
    a = jnp.dot(h, wg, preferred_element_type=F32)
    b = jnp.dot(h, wu, preferred_element_type=F32)
    return a * jax.nn.sigmoid(a) * b


def _ffn_kernel(h_ref, wg_ref, wu_ref, wd_ref, x_ref, mod_ref, o_ref, *, gate_row):
    f = pl.program_id(1)
    t = _swiglu_tile(h_ref[...], wg_ref[...], wu_ref[...]).astype(BF16)

    @pl.when(f == 0)
    def _():
        o_ref[...] = jnp.dot(t, wd_ref[...], preferred_element_type=F32)

    @pl.when(f > 0)
    def _():
        o_ref[...] += jnp.dot(t, wd_ref[...], preferred_element_type=F32)

    @pl.when(f == pl.num_programs(1) - 1)
    def _():
        o_ref[...] = x_ref[...] + mod_ref[gate_row:gate_row + 1, :] * o_ref[...]


def ffn_dense(h, wg, wu, wd, x2, mod, seq, gate_row, tm=1024, tf=512):
    n, d = x2.shape
    ff = wg.shape[1]
    tm = _tile(seq, tm)
    tf = _tile(ff, tf)
    per_b = seq // tm
    return pl.pallas_call(
        functools.partial(_ffn_kernel, gate_row=gate_row),
        grid=(n // tm, ff // tf),
        in_specs=[
            pl.BlockSpec((tm, d), lambda i, f: (i, 0)),
            pl.BlockSpec((d, tf), lambda i, f: (0, f)),
            pl.BlockSpec((d, tf), lambda i, f: (0, f)),
            pl.BlockSpec((tf, d), lambda i, f: (f, 0)),
            pl.BlockSpec((tm, d), lambda i, f: (i, 0), pipeline_mode=pl.Buffered(1)),
            pl.BlockSpec((None, 6, d), lambda i, f: (i // per_b, 0, 0)),
        ],
        out_specs=pl.BlockSpec((tm, d), lambda i, f: (i, 0)),
        out_shape=jax.ShapeDtypeStruct((n, d), F32),
        compiler_params=_params(("arbitrary", "arbitrary")),
        name="ffn_dense",
    )(h, wg, wu, wd, x2, mod)


def _route_plan(route, n_experts, tm):
    n = route.shape[0]
    pairs = TOP_K * n
    e_flat = route[:, :TOP_K].astype(jnp.int32).reshape(pairs)
    onehot = (e_flat[:, None] == jnp.arange(n_experts, dtype=jnp.int32)[None, :]).astype(jnp.int32)
    csum = jnp.cumsum(onehot, axis=0)
    counts = csum[-1]
    gend = jnp.cumsum(counts)
    gstart = gend - counts
    rank = jnp.sum(csum * onehot, axis=1) - 1
    pos = (jnp.sum(gstart[None, :] * onehot, axis=1) + rank).astype(jnp.int32)

    n_tiles = pairs // tm
    n_items = n_tiles + n_experts - 1
    first_row = jnp.arange(n_tiles, dtype=jnp.int32) * tm
    e_lo = jnp.sum((gend[None, :] <= first_row[:, None]).astype(jnp.int32), axis=1)
    e_hi = jnp.sum((gend[None, :] <= (first_row + tm - 1)[:, None]).astype(jnp.int32), axis=1)
    per_tile = e_hi - e_lo + 1
    item_end = jnp.cumsum(per_tile)
    item_start = item_end - per_tile
    total = item_end[-1]
    w = jnp.arange(n_items, dtype=jnp.int32)
    wi_tile = jnp.minimum(jnp.sum((item_end[None, :] <= w[:, None]).astype(jnp.int32), axis=1),
                          n_tiles - 1)
    wi_expert = jnp.minimum(e_lo[wi_tile] + (w - item_start[wi_tile]), e_hi[wi_tile])
    valid = w < total
    wi_first = jnp.logical_and(valid, w == item_start[wi_tile]).astype(jnp.int32)
    bounds = jnp.concatenate([gstart, gend[-1:]]).astype(jnp.int32)
    return pos, (wi_tile.astype(jnp.int32), wi_expert.astype(jnp.int32),
                 valid.astype(jnp.int32), wi_first, bounds)


def _dispatch_kernel(pos_ref, h_ref, xs_ref, sem):
    i = pl.program_id(0)
    tm = h_ref.shape[0]

    def issue(r, carry):
        for k in range(TOP_K):
            dst = pos_ref[(i * tm + r) * TOP_K + k]
            pltpu.make_async_copy(h_ref.at[r], xs_ref.at[dst], sem).start()
        return carry

    lax.fori_loop(0, tm, issue, 0)
    for k in range(TOP_K):
        pltpu.make_async_copy(h_ref, xs_ref.at[pl.ds(0, tm)], sem).wait()


def moe_dispatch(h3, pos, tm=512):
    n, sub, lanes = h3.shape
    tm = _tile(n, tm)
    return pl.pallas_call(
        _dispatch_kernel,
        grid_spec=pltpu.PrefetchScalarGridSpec(
            num_scalar_prefetch=1,
            grid=(n // tm,),
            in_specs=[pl.BlockSpec((tm, sub, lanes), lambda i, pos: (i, 0, 0))],
            out_specs=pl.BlockSpec(memory_space=pl.ANY),
            scratch_shapes=[pltpu.SemaphoreType.DMA(())],
        ),
        out_shape=jax.ShapeDtypeStruct((TOP_K * n, sub, lanes), h3.dtype),
        compiler_params=_params(("arbitrary",), has_side_effects=True),
        name="moe_dispatch",
    )(pos, h3)


def _moe_ffn_kernel(tile_ref, exp_ref, valid_ref, first_ref, bounds_ref,
                    xs_ref, wg_ref, wu_ref, wd_ref, o_ref):
    w = pl.program_id(0)
    f = pl.program_id(1)
    tm = xs_ref.shape[0]

    @pl.when(valid_ref[w] == 1)
    def _():
        e = exp_ref[w]
        rows = tile_ref[w] * tm + lax.broadcasted_iota(jnp.int32, (tm, 1), 0)
        mine = jnp.logical_and(rows >= bounds_ref[e], rows < bounds_ref[e + 1])
        t = _swiglu_tile(xs_ref[...], wg_ref[...].astype(BF16), wu_ref[...].astype(BF16))
        t = jnp.where(mine, t, 0.0).astype(BF16)
        wd = wd_ref[...].astype(BF16)
        fresh = jnp.logical_and(first_ref[w] == 1, f == 0)

        @pl.when(fresh)
        def _():
            o_ref[...] = jnp.dot(t, wd, preferred_element_type=F32)

        @pl.when(jnp.logical_not(fresh))
        def _():
            o_ref[...] += jnp.dot(t, wd, preferred_element_type=F32)


def moe_ffn(xs, plan, wg, wu, wd, tm, tf=256):
    rows, d = xs.shape
    n_experts, _, ff = wg.shape
    tf = _tile(ff, tf)
    n_items = rows // tm + n_experts - 1
    return pl.pallas_call(
        _moe_ffn_kernel,
        grid_spec=pltpu.PrefetchScalarGridSpec(
            num_scalar_prefetch=5,
            grid=(n_items, ff // tf),
            in_specs=[
                pl.BlockSpec((tm, d), lambda w, f, tl, ex, va, fi, bo: (tl[w], 0)),
                pl.BlockSpec((None, d, tf), lambda w, f, tl, ex, va, fi, bo: (ex[w], 0, f)),
                pl.BlockSpec((None, d, tf), lambda w, f, tl, ex, va, fi, bo: (ex[w], 0, f)),
                pl.BlockSpec((None, tf, d), lambda w, f, tl, ex, va, fi, bo: (ex[w], f, 0)),
            ],
            out_specs=pl.BlockSpec((tm, d), lambda w, f, tl, ex, va, fi, bo: (tl[w], 0)),
        ),
        out_shape=jax.ShapeDtypeStruct((rows, d), F32),
        compiler_params=_params(("arbitrary", "arbitrary")),
        name="moe_ffn",
    )(*plan, xs, wg, wu, wd)


def _combine_kernel(pos_ref, ys_ref, route_ref, x_ref, mod_ref, g_ref, o_ref, buf, sem,
                    *, gate_row, final):
    i = pl.program_id(0)
    tm = x_ref.shape[0]

    def issue(r, carry):
        for k in range(TOP_K):
            src = pos_ref[(i * tm + r) * TOP_K + k]
            pltpu.make_async_copy(ys_ref.at[pl.ds(src, 1), :], buf.at[k, pl.ds(r, 1), :], sem).start()
        return carry

    lax.fori_loop(0, tm, issue, 0)
    for k in range(TOP_K):
        pltpu.make_async_copy(ys_ref.at[pl.ds(0, tm), :], buf.at[k], sem).wait()
    route = route_ref[...]
    y = buf[0] * route[:, TOP_K:TOP_K + 1]
    for k in range(1, TOP_K):
        y = y + buf[k] * route[:, TOP_K + k:TOP_K + k + 1]
    x_new = x_ref[...] + mod_ref[gate_row:gate_row + 1, :] * y
    o_ref[...] = _rms(x_new, g_ref[...]) if final else x_new


def moe_combine(ys, pos, route, x2, mod, seq, gate_row, final_g=None, tm=256):
    n, d = x2.shape
    tm = _tile(seq, tm)
    per_b = seq // tm
    g = jnp.ones((1, d), F32) if final_g is None else final_g.reshape(1, d)
    return pl.pallas_call(
        functools.partial(_combine_kernel, gate_row=gate_row, final=final_g is not None),
        grid_spec=pltpu.PrefetchScalarGridSpec(
            num_scalar_prefetch=1,
            grid=(n // tm,),
            in_specs=[
                pl.BlockSpec(memory_space=pl.ANY),
                pl.BlockSpec((tm, LANES), lambda i, pos: (i, 0)),
                pl.BlockSpec((tm, d), lambda i, pos: (i, 0)),
                pl.BlockSpec((None, 6, d), lambda i, pos: (i // per_b, 0, 0)),
                pl.BlockSpec((1, d), lambda i, pos: (0, 0)),
            ],
            out_specs=pl.BlockSpec((tm, d), lambda i, pos: (i, 0)),
            scratch_shapes=[pltpu.VMEM((TOP_K, tm, d), F32), pltpu.SemaphoreType.DMA(())],
        ),
        out_shape=jax.ShapeDtypeStruct((n, d), F32),
        compiler_params=_params(("arbitrary",)),
        name="moe_combine",
    )(pos, ys, route, x2, mod, g)


def ffn_moe(h, route, wg, wu, wd, x2, mod, seq, gate_row, final_g=None, tm=1024):
    n, d = x2.shape
    n_experts = wg.shape[0]
    tm = _tile(TOP_K * n, tm)
    pos, plan = _route_plan(route, n_experts, tm)
    xs3 = moe_dispatch(h.reshape(n, d // LANES, LANES), pos)
    ys = moe_ffn(xs3.reshape(TOP_K * n, d), plan, wg, wu, wd, tm)
    return moe_combine(ys, pos, route, x2, mod, seq, gate_row, final_g)


def _rope_tables(seq):
    inv = ROPE_THETA ** (-jnp.arange(0, HEAD_DIM, 2, dtype=F32) / HEAD_DIM)
    ang = jnp.arange(seq, dtype=F32)[:, None] * inv[None, :]
    cos, sin = jnp.cos(ang), jnp.sin(ang)
    return jnp.concatenate([cos, cos], axis=-1), jnp.concatenate([-sin, sin], axis=-1)


def kernel(x, c, attn_norm_g, ada_w, ada_b, w_in, forget_b, moba_out_g, fox_out_g, w_o, ffn_norm_g,
           dense_w_gate, dense_w_up, dense_w_down, router_w, moe_w_gate, moe_w_up, moe_w_down,
           final_norm_g):
    batch, seq, d = x.shape
    depth = attn_norm_g.shape[0]
    moba_w = moba_out_g.shape[1]
    fox_w = fox_out_g.shape[1]
    moba_h = moba_w // HEAD_DIM
    fox_h = fox_w // HEAD_DIM
    qkv_w = 3 * moba_w + 3 * fox_w
    assert moba_w == fox_w
    kinds = (ROPE_SCALED, ROPE, PLAIN, SCALED, PLAIN, PLAIN)
    hb = HEAD_DIM

    cos_t, sin_t = _rope_tables(seq)
    mods = ada_modulation(c, ada_w, ada_b)
    x2 = x.reshape(batch * seq, d)
    out = None

    for l in range(depth):
        mod = mods[l]
        h = norm_mod(x2, attn_norm_g[l], mod, seq, sh_row=0)
        w_qkv = w_in[l][:, :qkv_w].astype(BF16)
        wz = jnp.zeros((d, LANES), BF16).at[:, :fox_h].set(w_in[l][:, qkv_w:].astype(BF16))
        proj, fz = in_proj(h, w_qkv, wz, cos_t, sin_t, seq, kinds, tn=moba_w)
        cum_t = logf_cumsum(fz[:, :fox_h].reshape(batch, seq, fox_h), forget_b[l])
        moba_o = attention(proj, batch, seq, moba_h, q_col=0, k_col=moba_w // hb,
                           v_col=2 * moba_w // hb)
        fc = 3 * moba_w // hb
        fox_o = attention(proj, batch, seq, fox_h, q_col=fc, k_col=fc + fox_w // hb,
                          v_col=fc + 2 * fox_w // hb, cum_t=cum_t)
        i = l // 2
        last = l == depth - 1
        w_o_l = w_o[l].astype(BF16)
        if l % 2 == 0:
            x2, h = out_proj(moba_o, fox_o, moba_out_g[l], fox_out_g[l], w_o_l, x2, mod,
                             ffn_norm_g[l], seq, gate_row=2, sh_row=3)
            x2 = ffn_dense(h, dense_w_gate[i].astype(BF16), dense_w_up[i].astype(BF16),
                           dense_w_down[i].astype(BF16), x2, mod, seq, gate_row=5)
        else:
            x2, h, route = out_proj(moba_o, fox_o, moba_out_g[l], fox_out_g[l], w_o_l, x2, mod,
                                    ffn_norm_g[l], seq, gate_row=2, sh_row=3, router_w=router_w[i])
            x2 = ffn_moe(h, route, moe_w_gate[i], moe_w_up[i], moe_w_down[i], x2, mod, seq,
                         gate_row=5, final_g=final_norm_g if last else None)
            if last:
                out = x2
    if out is None:
        out = final_norm(x2, final_norm_g)
    return out.reshape(batch, seq, d)
```

```python
import functools
import math

import jax
import jax.numpy as jnp
from jax import lax
from jax.experimental import pallas as pl
from jax.experimental.pallas import tpu as pltpu

F32 = jnp.float32
BF16 = jnp.bfloat16

HEAD_DIM = 128
MOBA_BLOCK = 256
MOBA_TOPK = 3
ROPE_THETA = 10000.0
EPS = 1e-6
TOP_K = 2
LANES = 128
VMEM_LIMIT = 56 * 1024 * 1024

NEG_INF = float("-inf")
NEG_BIG = -(2.0 ** 100)
LOG2E = math.log2(math.e)
Q_SCALE = HEAD_DIM ** -0.5 * LOG2E


def _params(semantics, **kw):
    return pltpu.CompilerParams(dimension_semantics=semantics, vmem_limit_bytes=VMEM_LIMIT, **kw)


def _tile(n, want):
    t = min(n, want)
    assert n % t == 0, (n, want)
    return t


def _ada_kernel(c_ref, w_ref, b_ref, o_ref):
    c = c_ref[...]
    c_act = c * jax.nn.sigmoid(c)
    o_ref[...] = jnp.dot(c_act, w_ref[...], preferred_element_type=F32,
                         precision=lax.Precision.HIGHEST) + b_ref[...]


def ada_modulation(c, ada_w, ada_b):
    n_layers, d, d6 = ada_w.shape
    b = c.shape[0]
    rows = -(-b // 8) * 8
    c_pad = jnp.zeros((rows, d), F32).at[:b].set(c)
    tn = _tile(d6, 1024)
    out = pl.pallas_call(
        _ada_kernel,
        grid=(n_layers, d6 // tn),
        in_specs=[
            pl.BlockSpec((rows, d), lambda l, j: (0, 0)),
            pl.BlockSpec((None, d, tn), lambda l, j: (l, 0, j)),
            pl.BlockSpec((None, 1, tn), lambda l, j: (l, 0, j)),
        ],
        out_specs=pl.BlockSpec((None, rows, tn), lambda l, j: (l, 0, j)),
        out_shape=jax.ShapeDtypeStruct((n_layers, rows, d6), F32),
        compiler_params=_params(("arbitrary", "arbitrary")),
        name="ada_modulation",
    )(c_pad, ada_w, ada_b.reshape(n_layers, 1, d6))
    return out[:, :b].reshape(n_layers, b, 6, d)


def _rms(x, g):
    return x * lax.rsqrt(jnp.mean(x * x, axis=-1, keepdims=True) + EPS) * g


def _norm_mod_kernel(x_ref, g_ref, mod_ref, o_ref, *, sh_row):
    y = _rms(x_ref[...], g_ref[...])
    sh = mod_ref[sh_row:sh_row + 1, :]
    sc = mod_ref[sh_row + 1:sh_row + 2, :]
    o_ref[...] = (y * (1.0 + sc) + sh).astype(o_ref.dtype)


def _top2_route(logits, n_experts):
    lane = lax.broadcasted_iota(jnp.int32, logits.shape, 1).astype(F32)
    lg = jnp.where(lane < n_experts, logits, NEG_INF)
    m1 = jnp.max(lg, axis=-1, keepdims=True)
    i1 = jnp.min(jnp.where(lg == m1, lane, float(LANES)), axis=-1, keepdims=True)
    lg2 = jnp.where(lane == i1, NEG_INF, lg)
    m2 = jnp.max(lg2, axis=-1, keepdims=True)
    i2 = jnp.min(jnp.where(lg2 == m2, lane, float(LANES)), axis=-1, keepdims=True)
    e2 = jnp.exp(m2 - m1)
    w1 = 1.0 / (1.0 + e2)
    w2 = e2 / (1.0 + e2)
    return jnp.where(lane == 0.0, i1, jnp.where(lane == 1.0, i2,
                     jnp.where(lane == 2.0, w1, jnp.where(lane == 3.0, w2, 0.0))))


def norm_mod(x2, g, mod, seq, sh_row, tm=512):
    n, d = x2.shape
    tm = _tile(seq, tm)
    per_b = seq // tm
    return pl.pallas_call(
        functools.partial(_norm_mod_kernel, sh_row=sh_row),
        grid=(n // tm,),
        in_specs=[
            pl.BlockSpec((tm, d), lambda i: (i, 0)),
            pl.BlockSpec((1, d), lambda i: (0, 0)),
            pl.BlockSpec((None, 6, d), lambda i: (i // per_b, 0, 0)),
        ],
        out_specs=pl.BlockSpec((tm, d), lambda i: (i, 0)),
        out_shape=jax.ShapeDtypeStruct((n, d), BF16),
        compiler_params=_params(("arbitrary",)),
        name="norm_mod",
    )(x2, g.reshape(1, d), mod)


def _final_norm_kernel(x_ref, g_ref, o_ref):
    o_ref[...] = _rms(x_ref[...], g_ref[...])


def final_norm(x2, g, tm=512):
    n, d = x2.shape
    tm = _tile(n, tm)
    return pl.pallas_call(
        _final_norm_kernel,
        grid=(n // tm,),
        in_specs=[pl.BlockSpec((tm, d), lambda i: (i, 0)),
                  pl.BlockSpec((1, d), lambda i: (0, 0))],
        out_specs=pl.BlockSpec((tm, d), lambda i: (i, 0)),
        out_shape=jax.ShapeDtypeStruct((n, d), F32),
        compiler_params=_params(("arbitrary",)),
        name="final_norm",
    )(x2, g.reshape(1, d))


PLAIN, ROPE, ROPE_SCALED, SCALED = 0, 1, 2, 3


def _in_proj_kernel(h_ref, w_ref, wz_ref, cos_ref, sin_ref, o_ref, fz_ref, *, kinds):
    j = pl.program_id(1)
    tn = w_ref.shape[1]

    def is_kind(kind):
        hits = [jj for jj, k in enumerate(kinds) if k == kind]
        cond = j == hits[0]
        for jj in hits[1:]:
            cond = jnp.logical_or(cond, j == jj)
        return cond

    for kind in sorted(set(kinds)):
        @pl.when(is_kind(kind))
        def _(kind=kind):
            acc = jnp.dot(h_ref[...], w_ref[...].astype(BF16), preferred_element_type=F32)
            mult = Q_SCALE if kind in (ROPE_SCALED, SCALED) else 1.0
            if kind in (ROPE, ROPE_SCALED):
                c = cos_ref[...] * mult
                s = sin_ref[...] * mult
                for hh in range(tn // HEAD_DIM):
                    t = acc[:, hh * HEAD_DIM:(hh + 1) * HEAD_DIM]
                    rot = pltpu.roll(t, HEAD_DIM // 2, 1)
                    o_ref[:, hh * HEAD_DIM:(hh + 1) * HEAD_DIM] = (t * c + rot * s).astype(o_ref.dtype)
            elif kind == SCALED:
                o_ref[...] = (acc * mult).astype(o_ref.dtype)
            else:
                o_ref[...] = acc.astype(o_ref.dtype)

    @pl.when(j == 0)
    def _():
        fz_ref[...] = jnp.dot(h_ref[...], wz_ref[...], preferred_element_type=F32)


def in_proj(h, w_in, layer, wz, cos_t, sin_t, seq, kinds, tn, tm=1024):
    n, d = h.shape
    width = tn * len(kinds)
    assert width <= w_in.shape[2]
    tm = _tile(seq, tm)
    per_b = seq // tm
    return pl.pallas_call(
        functools.partial(_in_proj_kernel, kinds=tuple(kinds)),
        grid=(n // tm, width // tn),
        in_specs=[
            pl.BlockSpec((tm, d), lambda i, j: (i, 0)),
            pl.BlockSpec((None, d, tn), lambda i, j: (layer, 0, j)),
            pl.BlockSpec((d, LANES), lambda i, j: (0, 0)),
            pl.BlockSpec((tm, HEAD_DIM), lambda i, j: (i % per_b, 0)),
            pl.BlockSpec((tm, HEAD_DIM), lambda i, j: (i % per_b, 0)),
        ],
        out_specs=[pl.BlockSpec((tm, tn), lambda i, j: (i, j)),
                   pl.BlockSpec((tm, LANES), lambda i, j: (i, 0))],
        out_shape=[jax.ShapeDtypeStruct((n, width), BF16),
                   jax.ShapeDtypeStruct((n, LANES), F32)],
        compiler_params=_params(("arbitrary", "arbitrary")),
        name="in_proj",
    )(h, w_in, wz, cos_t, sin_t)


def _logf_cumsum_kernel(z_ref, b_ref, o_ref):
    z = z_ref[...] + b_ref[...]
    lf = jnp.minimum(z, 0.0) - jnp.log1p(jnp.exp(-jnp.abs(z)))
    r = lax.broadcasted_iota(jnp.int32, (LANES, LANES), 0)
    c = lax.broadcasted_iota(jnp.int32, (LANES, LANES), 1)
    tri = jnp.where(c <= r, 1.0, 0.0).astype(F32)
    carry = jnp.zeros((1, z.shape[1]), F32)
    for ch in range(z.shape[0] // LANES):
        seg = lf[ch * LANES:(ch + 1) * LANES, :]
        cs = jnp.dot(tri, seg, preferred_element_type=F32, precision=lax.Precision.HIGHEST) + carry
        o_ref[ch * LANES:(ch + 1) * LANES, :] = cs
        carry = cs[LANES - 1:LANES, :]


def logf_cumsum(z, forget_b):
    b, s, h = z.shape
    return pl.pallas_call(
        _logf_cumsum_kernel,
        grid=(b,),
        in_specs=[pl.BlockSpec((None, s, h), lambda i: (i, 0, 0)),
                  pl.BlockSpec((1, h), lambda i: (0, 0))],
        out_specs=pl.BlockSpec((None, s, h), lambda i: (i, 0, 0)),
        out_shape=jax.ShapeDtypeStruct((b, s, h), F32),
        compiler_params=_params(("arbitrary",)),
        name="logf_cumsum",
    )(z, forget_b.reshape(1, h))


def _split3(f):
    hi = f.astype(BF16).astype(F32)
    r = f - hi
    mid = r.astype(BF16).astype(F32)
    lo = (r - mid).astype(BF16).astype(F32)
    return hi, mid, lo


def _pick_lane(x, idx):
    lane = lax.broadcasted_iota(jnp.int32, x.shape, 1)
    return jnp.sum(jnp.where(lane == idx, x, 0.0), axis=-1, keepdims=True)


def _attn_kernel(*refs, mode, tq, tk, seq):
    if mode == "fox":
        q_ref, k_ref, v_ref, ct_ref, o_ref, kaug, vaug, qaug, s_a, s_b, bias_scr, m_scr, acc_scr = refs
    else:
        (q_ref, k_ref, v_ref, o_ref, kaug, vaug, qaug, s_a, s_b, bias_scr, m_scr, acc_scr,
         kmean_scr, kmean3_scr) = refs
    h = pl.program_id(1)
    i = pl.program_id(2)
    n_moba_blocks = seq // MOBA_BLOCK

    @pl.when(jnp.logical_and(jnp.logical_and(pl.program_id(0) == 0, h == 0), i == 0))
    def _():
        for dd in range(2):
            r2 = lax.broadcasted_iota(jnp.int32, (tq, tk), 0)
            c2 = lax.broadcasted_iota(jnp.int32, (tq, tk), 1) + dd * tk
            ok = c2 <= r2
            if mode == "moba":
                ok = jnp.logical_or(ok, c2 // MOBA_BLOCK != r2 // MOBA_BLOCK)
            bias_scr[dd] = jnp.where(ok, 0.0, NEG_INF)

    @pl.when(i == 0)
    def _():
        lane = lax.broadcasted_iota(jnp.int32, (tk, LANES), 1)
        row = lax.broadcasted_iota(jnp.int32, (tk, LANES), 0)

        def build(cidx, carry):
            st = pl.multiple_of(cidx * tk, tk)
            kaug[pl.ds(st, tk), 0:HEAD_DIM] = k_ref[pl.ds(st, tk), :]
            vaug[pl.ds(st, tk), 0:HEAD_DIM] = v_ref[pl.ds(st, tk), :]
            vaug[pl.ds(st, tk), HEAD_DIM:2 * HEAD_DIM] = jnp.ones((tk, HEAD_DIM), BF16)
            if mode == "fox":
                c = _pick_lane(ct_ref[pl.ds(st, tk), :], h) * LOG2E
                hi, mid, lo = _split3(c)
                ext = jnp.where(lane < 3, 1.0, jnp.where(lane == 3, -hi, jnp.where(
                    lane == 4, -mid, jnp.where(lane == 5, -lo, 0.0))))
            else:
                ext = jnp.where(lane == (st + row) // MOBA_BLOCK, 1.0, 0.0)
            kaug[pl.ds(st, tk), HEAD_DIM:2 * HEAD_DIM] = ext.astype(BF16)
            return carry

        lax.fori_loop(0, seq // tk, build, 0)
        if mode == "moba":
            kmean_scr[...] = jnp.zeros_like(kmean_scr)
            for b in range(n_moba_blocks):
                kb = k_ref[b * MOBA_BLOCK:(b + 1) * MOBA_BLOCK, :].astype(F32)
                kmean_scr[b:b + 1, :] = jnp.mean(kb, axis=0, keepdims=True)
            for piece, part in enumerate(_split3(kmean_scr[...])):
                kmean3_scr[piece * LANES:(piece + 1) * LANES, :] = part.astype(BF16)

    lane = lax.broadcasted_iota(jnp.int32, (MOBA_BLOCK, LANES), 1)
    for rb in range(tq // MOBA_BLOCK):
        rows = pl.ds(rb * MOBA_BLOCK, MOBA_BLOCK)
        q = q_ref[rows, :]
        if mode == "fox":
            start = pl.multiple_of(i * tq + rb * MOBA_BLOCK, MOBA_BLOCK)
            f = _pick_lane(ct_ref[pl.ds(start, MOBA_BLOCK), :], h) * LOG2E
            hi, mid, lo = _split3(f)
            ext = jnp.where(lane == 0, hi, jnp.where(lane == 1, mid, jnp.where(
                lane == 2, lo, jnp.where(lane < 6, 1.0, 0.0))))
        else:
            g3 = lax.dot_general(q, kmean3_scr[...], (((1,), (1,)), ((), ())),
                                 preferred_element_type=F32)
            gate = g3[:, 0:LANES] + g3[:, LANES:2 * LANES] + g3[:, 2 * LANES:3 * LANES]
            own = i * (tq // MOBA_BLOCK) + rb
            lane_f = lane.astype(F32)
            g = jnp.where(lane < own, gate, NEG_INF)
            keep = lane == own
            for _ in range(min(MOBA_TOPK, n_moba_blocks)):
                m = jnp.max(g, axis=-1, keepdims=True)
                idx = jnp.min(jnp.where(g == m, lane_f, float(LANES)), axis=-1, keepdims=True)
                hit = lane_f == idx
                keep = jnp.logical_or(keep, jnp.logical_and(hit, m > NEG_INF))
                g = jnp.where(hit, NEG_INF, g)
            ext = jnp.where(keep, 0.0, NEG_BIG)
        qaug[rows, 0:HEAD_DIM] = q
        qaug[rows, HEAD_DIM:2 * HEAD_DIM] = ext.astype(BF16)

    def scores(j):
        st = pl.multiple_of(j * tk, tk)
        return lax.dot_general(qaug[...], kaug[pl.ds(st, tk), :], (((1,), (1,)), ((), ())),
                               preferred_element_type=F32)

    def update(s, j, diag_off):
        st = pl.multiple_of(j * tk, tk)
        if diag_off is not None:
            s = s + bias_scr[diag_off // tk]
        m_old = m_scr[...]
        m_new = jnp.maximum(m_old, jnp.max(s, axis=-1, keepdims=True))
        p = jnp.exp2(s - m_new).astype(BF16)
        pv = jnp.dot(p, vaug[pl.ds(st, tk), :], preferred_element_type=F32)
        acc_scr[...] = jnp.exp2(m_old - m_new) * acc_scr[...] + pv
        m_scr[...] = m_new

    m_scr[...] = jnp.full(m_scr.shape, NEG_INF, F32)
    acc_scr[...] = jnp.zeros(acc_scr.shape, F32)
    s_a[...] = scores(0)

    def body(jj, carry):
        s_b[...] = scores(2 * jj + 1)
        update(s_a[...], 2 * jj, None)
        s_a[...] = scores(2 * jj + 2)
        update(s_b[...], 2 * jj + 1, None)
        return carry

    lax.fori_loop(0, i, body, 0)
    s_b[...] = scores(2 * i + 1)
    update(s_a[...], 2 * i, 0)
    update(s_b[...], 2 * i + 1, tk)
    o_ref[...] = (acc_scr[:, 0:HEAD_DIM] / acc_scr[:, HEAD_DIM:2 * HEAD_DIM]).astype(o_ref.dtype)


def attention(proj, batch, seq, heads, q_col, k_col, v_col, cum_t=None, tq=1024):
    n = proj.shape[0]
    tq = _tile(seq, tq)
    tk = tq // 2
    nq = seq // tq
    mode = "moba" if cum_t is None else "fox"
    assert tk % MOBA_BLOCK == 0 and seq // MOBA_BLOCK <= LANES
    in_specs = [
        pl.BlockSpec((tq, HEAD_DIM), lambda b, h, i: (b * nq + i, q_col + h)),
        pl.BlockSpec((seq, HEAD_DIM), lambda b, h, i: (b, k_col + h)),
        pl.BlockSpec((seq, HEAD_DIM), lambda b, h, i: (b, v_col + h)),
    ]
    args = [proj, proj, proj]
    scratch = [pltpu.VMEM((seq, 2 * HEAD_DIM), BF16), pltpu.VMEM((seq, 2 * HEAD_DIM), BF16),
               pltpu.VMEM((tq, 2 * HEAD_DIM), BF16),
               pltpu.VMEM((tq, tk), F32), pltpu.VMEM((tq, tk), F32),
               pltpu.VMEM((2, tq, tk), F32),
               pltpu.VMEM((tq, 1), F32), pltpu.VMEM((tq, 2 * HEAD_DIM), F32)]
    if mode == "fox":
        in_specs.append(pl.BlockSpec((None, seq, heads), lambda b, h, i: (b, 0, 0)))
        args.append(cum_t)
    else:
        scratch += [pltpu.VMEM((LANES, HEAD_DIM), F32), pltpu.VMEM((3 * LANES, HEAD_DIM), BF16)]
    return pl.pallas_call(
        functools.partial(_attn_kernel, mode=mode, tq=tq, tk=tk, seq=seq),
        grid=(batch, heads, nq),
        in_specs=in_specs,
        out_specs=pl.BlockSpec((tq, HEAD_DIM), lambda b, h, i: (b * nq + i, h)),
        out_shape=jax.ShapeDtypeStruct((n, heads * HEAD_DIM), BF16),
        scratch_shapes=scratch,
        compiler_params=_params(("arbitrary", "arbitrary", "arbitrary")),
        name=mode + "_attention",
    )(*args)


def _out_proj_kernel(*refs, gate_row, sh_row, n_experts):
    if n_experts:
        (mo_ref, fo_ref, g1_ref, g2_ref, w_ref, x_ref, mod_ref, gf_ref, wr_ref,
         o_ref, h_ref, route_ref) = refs
    else:
        mo_ref, fo_ref, g1_ref, g2_ref, w_ref, x_ref, mod_ref, gf_ref, o_ref, h_ref = refs
    half = mo_ref.shape[1]
    m1 = _rms(mo_ref[...].astype(F32), g1_ref[...]).astype(BF16)
    m2 = _rms(fo_ref[...].astype(F32), g2_ref[...]).astype(BF16)
    acc = jnp.dot(m1, w_ref[0:half, :], preferred_element_type=F32)
    acc = acc + jnp.dot(m2, w_ref[half:2 * half, :], preferred_element_type=F32)
    x_new = x_ref[...] + mod_ref[gate_row:gate_row + 1, :] * acc
    o_ref[...] = x_new
    sh = mod_ref[sh_row:sh_row + 1, :]
    sc = mod_ref[sh_row + 1:sh_row + 2, :]
    h = _rms(x_new, gf_ref[...]) * (1.0 + sc) + sh
    h_hi = h.astype(BF16)
    h_ref[...] = h_hi
    if n_experts:
        h_lo = (h - h_hi.astype(F32)).astype(BF16)
        l_hi = jnp.dot(h_hi, wr_ref[...], preferred_element_type=F32)
        l_lo = jnp.dot(h_lo, wr_ref[:, 0:LANES], preferred_element_type=F32)
        logits = l_hi[:, 0:LANES] + l_hi[:, LANES:2 * LANES] + l_lo
        route_ref[...] = _top2_route(logits, n_experts)


def out_proj(moba_o, fox_o, g1, g2, w_o, x2, mod, ffn_g, seq, gate_row, sh_row, router_w=None, tm=512):
    n, d = x2.shape
    half = moba_o.shape[1]
    tm = _tile(seq, tm)
    per_b = seq // tm
    n_experts = 0 if router_w is None else router_w.shape[1]
    in_specs = [
        pl.BlockSpec((tm, half), lambda i: (i, 0)),
        pl.BlockSpec((tm, half), lambda i: (i, 0)),
        pl.BlockSpec((1, half), lambda i: (0, 0)),
        pl.BlockSpec((1, half), lambda i: (0, 0)),
        pl.BlockSpec((2 * half, d), lambda i: (0, 0)),
        pl.BlockSpec((tm, d), lambda i: (i, 0)),
        pl.BlockSpec((None, 6, d), lambda i: (i // per_b, 0, 0)),
        pl.BlockSpec((1, d), lambda i: (0, 0)),
    ]
    args = [moba_o, fox_o, g1.reshape(1, half), g2.reshape(1, half), w_o, x2, mod, ffn_g.reshape(1, d)]
    out_specs = [pl.BlockSpec((tm, d), lambda i: (i, 0)), pl.BlockSpec((tm, d), lambda i: (i, 0))]
    out_shape = [jax.ShapeDtypeStruct((n, d), F32), jax.ShapeDtypeStruct((n, d), BF16)]
    if n_experts:
        in_specs.append(pl.BlockSpec((d, 2 * LANES), lambda i: (0, 0)))
        w_hi = router_w.astype(BF16)
        w_lo = (router_w - w_hi.astype(F32)).astype(BF16)
        args.append(jnp.zeros((d, 2 * LANES), BF16).at[:, :n_experts].set(w_hi)
                    .at[:, LANES:LANES + n_experts].set(w_lo))
        out_specs.append(pl.BlockSpec((tm, LANES), lambda i: (i, 0)))
        out_shape.append(jax.ShapeDtypeStruct((n, LANES), F32))
    return pl.pallas_call(
        functools.partial(_out_proj_kernel, gate_row=gate_row, sh_row=sh_row, n_experts=n_experts),
        grid=(n // tm,),
        in_specs=in_specs,
        out_specs=out_specs,
        out_shape=out_shape,
        compiler_params=_params(("arbitrary",)),
        name="out_proj",
    )(*args)


def _swiglu_tile(h, wg, wu):
    a = jnp.dot(h, wg, preferred_element_type=F32)
    b = jnp.dot(h, wu, preferred_element_type=F32)
    return a * jax.nn.sigmoid(a) * b


def _ffn_kernel(h_ref, wg_ref, wu_ref, wd_ref, x_ref, mod_ref, o_ref, *, gate_row):
    f = pl.program_id(1)
    t = _swiglu_tile(h_ref[...], wg_ref[...], wu_ref[...]).astype(BF16)

    @pl.when(f == 0)
    def _():
        o_ref[...] = jnp.dot(t, wd_ref[...], preferred_element_type=F32)

    @pl.when(f > 0)
    def _():
        o_ref[...] += jnp.dot(t, wd_ref[...], preferred_element_type=F32)

    @pl.when(f == pl.num_programs(1) - 1)
    def _():
        o_ref[...] = x_ref[...] + mod_ref[gate_row:gate_row + 1, :] * o_ref[...]


def ffn_dense(h, wg, wu, wd, x2, mod, seq, gate_row, tm=1024, tf=512):
    n, d = x2.shape
    ff = wg.shape[1]
    tm = _tile(seq, tm)
    tf = _tile(ff, tf)
    per_b = seq // tm
    return pl.pallas_call(
        functools.partial(_ffn_kernel, gate_row=gate_row),
        grid=(n // tm, ff // tf),
        in_specs=[
            pl.BlockSpec((tm, d), lambda i, f: (i, 0)),
            pl.BlockSpec((d, tf), lambda i, f: (0, f)),
            pl.BlockSpec((d, tf), lambda i, f: (0, f)),
            pl.BlockSpec((tf, d), lambda i, f: (f, 0)),
            pl.BlockSpec((tm, d), lambda i, f: (i, 0), pipeline_mode=pl.Buffered(1)),
            pl.BlockSpec((None, 6, d), lambda i, f: (i // per_b, 0, 0)),
        ],
        out_specs=pl.BlockSpec((tm, d), lambda i, f: (i, 0)),
        out_shape=jax.ShapeDtypeStruct((n, d), F32),
        compiler_params=_params(("arbitrary", "arbitrary")),
        name="ffn_dense",
    )(h, wg, wu, wd, x2, mod)


def _route_plan(route, n_experts, tm):
    n = route.shape[0]
    pairs = TOP_K * n
    e_flat = route[:, :TOP_K].astype(jnp.int32).reshape(pairs)
    onehot = (e_flat[:, None] == jnp.arange(n_experts, dtype=jnp.int32)[None, :]).astype(jnp.int32)
    csum = jnp.cumsum(onehot, axis=0)
    counts = csum[-1]
    padded = (counts + tm - 1) // tm * tm
    pend = jnp.cumsum(padded)
    pstart = pend - padded
    rank = jnp.sum(csum * onehot, axis=1) - 1
    pos = (jnp.sum(pstart[None, :] * onehot, axis=1) + rank).astype(jnp.int32)

    n_tiles = pairs // tm + n_experts
    first_row = jnp.arange(n_tiles, dtype=jnp.int32) * tm
    tile_expert = jnp.minimum(jnp.sum((pend[None, :] <= first_row[:, None]).astype(jnp.int32), axis=1),
                              n_experts - 1)
    valid = first_row < pend[-1]
    return pos, (tile_expert.astype(jnp.int32), valid.astype(jnp.int32),
                 (pstart + counts).astype(jnp.int32))


def _dispatch_kernel(pos_ref, h_ref, init_ref, xs_ref, sem):
    del init_ref
    i = pl.program_id(0)
    tm = h_ref.shape[0]

    def issue(r, carry):
        for k in range(TOP_K):
            dst = pos_ref[(i * tm + r) * TOP_K + k]
            pltpu.make_async_copy(h_ref.at[r], xs_ref.at[dst], sem).start()
        return carry

    lax.fori_loop(0, tm, issue, 0)
    for k in range(TOP_K):
        pltpu.make_async_copy(h_ref, xs_ref.at[pl.ds(0, tm)], sem).wait()


def moe_dispatch(h3, pos, rows, tm=512):
    n, sub, lanes = h3.shape
    tm = _tile(n, tm)
    return pl.pallas_call(
        _dispatch_kernel,
        grid_spec=pltpu.PrefetchScalarGridSpec(
            num_scalar_prefetch=1,
            grid=(n // tm,),
            in_specs=[pl.BlockSpec((tm, sub, lanes), lambda i, pos: (i, 0, 0)),
                      pl.BlockSpec(memory_space=pl.ANY)],
            out_specs=pl.BlockSpec(memory_space=pl.ANY),
            scratch_shapes=[pltpu.SemaphoreType.DMA(())],
        ),
        out_shape=jax.ShapeDtypeStruct((rows, sub, lanes), h3.dtype),
        input_output_aliases={2: 0},
        compiler_params=_params(("arbitrary",)),
        name="moe_dispatch",
    )(pos, h3, jnp.zeros((rows, sub, lanes), h3.dtype))


def _moe_ffn_kernel(exp_ref, valid_ref, rowend_ref, xs_ref, wg_ref, wu_ref, wd_ref, o_ref):
    t = pl.program_id(0)
    f = pl.program_id(1)
    tm = xs_ref.shape[0]

    @pl.when(valid_ref[t] == 1)
    def _():
        rows = t * tm + lax.broadcasted_iota(jnp.int32, (tm, 1), 0)
        real = rows < rowend_ref[exp_ref[t]]
        a = _swiglu_tile(xs_ref[...], wg_ref[...].astype(BF16), wu_ref[...].astype(BF16))
        a = jnp.where(real, a, 0.0).astype(BF16)
        wd = wd_ref[...].astype(BF16)

        @pl.when(f == 0)
        def _():
            o_ref[...] = jnp.dot(a, wd, preferred_element_type=F32)

        @pl.when(f > 0)
        def _():
            o_ref[...] += jnp.dot(a, wd, preferred_element_type=F32)

    @pl.when(jnp.logical_and(valid_ref[t] == 0, f == 0))
    def _():
        o_ref[...] = jnp.zeros(o_ref.shape, o_ref.dtype)


def moe_ffn(xs, plan, wg, wu, wd, tm, tf=256):
    rows, d = xs.shape
    n_experts, _, ff = wg.shape
    tf = _tile(ff, tf)
    nf = ff // tf

    def f_idx(t, f, va):
        return jnp.where(va[t] == 1, f, nf - 1)

    return pl.pallas_call(
        _moe_ffn_kernel,
        grid_spec=pltpu.PrefetchScalarGridSpec(
            num_scalar_prefetch=3,
            grid=(rows // tm, nf),
            in_specs=[
                pl.BlockSpec((tm, d), lambda t, f, ex, va, re: (t, 0)),
                pl.BlockSpec((None, d, tf), lambda t, f, ex, va, re: (ex[t], 0, f_idx(t, f, va))),
                pl.BlockSpec((None, d, tf), lambda t, f, ex, va, re: (ex[t], 0, f_idx(t, f, va))),
                pl.BlockSpec((None, tf, d), lambda t, f, ex, va, re: (ex[t], f_idx(t, f, va), 0)),
            ],
            out_specs=pl.BlockSpec((tm, d), lambda t, f, ex, va, re: (t, 0)),
        ),
        out_shape=jax.ShapeDtypeStruct((rows, d), F32),
        compiler_params=_params(("arbitrary", "arbitrary")),
        name="moe_ffn",
    )(*plan, xs, wg, wu, wd)


def _combine_kernel(pos_ref, ys_ref, route_ref, x_ref, mod_ref, g_ref, o_ref, buf, sem,
                    *, gate_row, final):
    i = pl.program_id(0)
    tm = x_ref.shape[0]
    slot = i % 2

    def gather(step, into):
        def issue(r, carry):
            for k in range(TOP_K):
                src = pos_ref[(step * tm + r) * TOP_K + k]
                pltpu.make_async_copy(ys_ref.at[pl.ds(src, 1), :], buf.at[into, k, pl.ds(r, 1), :],
                                      sem.at[into]).start()
            return carry

        lax.fori_loop(0, tm, issue, 0)

    @pl.when(i == 0)
    def _():
        gather(0, 0)

    @pl.when(i + 1 < pl.num_programs(0))
    def _():
        gather(i + 1, 1 - slot)

    for k in range(TOP_K):
        pltpu.make_async_copy(ys_ref.at[pl.ds(0, tm), :], buf.at[slot, k], sem.at[slot]).wait()
    route = route_ref[...]
    y = buf[slot, 0] * route[:, TOP_K:TOP_K + 1]
    for k in range(1, TOP_K):
        y = y + buf[slot, k] * route[:, TOP_K + k:TOP_K + k + 1]
    x_new = x_ref[...] + mod_ref[gate_row:gate_row + 1, :] * y
    o_ref[...] = _rms(x_new, g_ref[...]) if final else x_new


def moe_combine(ys, pos, route, x2, mod, seq, gate_row, final_g=None, tm=256):
    n, d = x2.shape
    tm = _tile(seq, tm)
    per_b = seq // tm
    g = jnp.ones((1, d), F32) if final_g is None else final_g.reshape(1, d)
    return pl.pallas_call(
        functools.partial(_combine_kernel, gate_row=gate_row, final=final_g is not None),
        grid_spec=pltpu.PrefetchScalarGridSpec(
            num_scalar_prefetch=1,
            grid=(n // tm,),
            in_specs=[
                pl.BlockSpec(memory_space=pl.ANY),
                pl.BlockSpec((tm, LANES), lambda i, pos: (i, 0)),
                pl.BlockSpec((tm, d), lambda i, pos: (i, 0)),
                pl.BlockSpec((None, 6, d), lambda i, pos: (i // per_b, 0, 0)),
                pl.BlockSpec((1, d), lambda i, pos: (0, 0)),
            ],
            out_specs=pl.BlockSpec((tm, d), lambda i, pos: (i, 0)),
            scratch_shapes=[pltpu.VMEM((2, TOP_K, tm, d), F32), pltpu.SemaphoreType.DMA((2,))],
        ),
        out_shape=jax.ShapeDtypeStruct((n, d), F32),
        compiler_params=_params(("arbitrary",)),
        name="moe_combine",
    )(pos, ys, route, x2, mod, g)


def ffn_moe(h, route, wg, wu, wd, x2, mod, seq, gate_row, final_g=None, tm=1024):
    n, d = x2.shape
    n_experts = wg.shape[0]
    tm = _tile(TOP_K * n, tm)
    pos, plan = _route_plan(route, n_experts, tm)
    rows = TOP_K * n + n_experts * tm
    xs3 = moe_dispatch(h.reshape(n, d // LANES, LANES), pos, rows)
    ys = moe_ffn(xs3.reshape(rows, d), plan, wg, wu, wd, tm)
    return moe_combine(ys, pos, route, x2, mod, seq, gate_row, final_g)


def _rope_tables(seq):
    inv = ROPE_THETA ** (-jnp.arange(0, HEAD_DIM, 2, dtype=F32) / HEAD_DIM)
    ang = jnp.arange(seq, dtype=F32)[:, None] * inv[None, :]
    cos, sin = jnp.cos(ang), jnp.sin(ang)
    return jnp.concatenate([cos, cos], axis=-1), jnp.concatenate([-sin, sin], axis=-1)


def kernel(x, c, attn_norm_g, ada_w, ada_b, w_in, forget_b, moba_out_g, fox_out_g, w_o, ffn_norm_g,
           dense_w_gate, dense_w_up, dense_w_down, router_w, moe_w_gate, moe_w_up, moe_w_down,
           final_norm_g):
    batch, seq, d = x.shape
    depth = attn_norm_g.shape[0]
    moba_w = moba_out_g.shape[1]
    fox_w = fox_out_g.shape[1]
    moba_h = moba_w // HEAD_DIM
    fox_h = fox_w // HEAD_DIM
    qkv_w = 3 * moba_w + 3 * fox_w
    assert moba_w == fox_w
    kinds = (ROPE_SCALED, ROPE, PLAIN, SCALED, PLAIN, PLAIN)
    hb = HEAD_DIM

    cos_t, sin_t = _rope_tables(seq)
    mods = ada_modulation(c, ada_w, ada_b)
    x2 = x.reshape(batch * seq, d)
    out = None

    for l in range(depth):
        mod = mods[l]
        h = norm_mod(x2, attn_norm_g[l], mod, seq, sh_row=0)
        wz = jnp.zeros((d, LANES), BF16).at[:, :fox_h].set(w_in[l, :, qkv_w:].astype(BF16))
        proj, fz = in_proj(h, w_in, l, wz, cos_t, sin_t, seq, kinds, tn=moba_w)
        cum_t = logf_cumsum(fz[:, :fox_h].reshape(batch, seq, fox_h), forget_b[l])
        moba_o = attention(proj, batch, seq, moba_h, q_col=0, k_col=moba_w // hb,
                           v_col=2 * moba_w // hb)
        fc = 3 * moba_w // hb
        fox_o = attention(proj, batch, seq, fox_h, q_col=fc, k_col=fc + fox_w // hb,
                          v_col=fc + 2 * fox_w // hb, cum_t=cum_t)
        i = l // 2
        last = l == depth - 1
        w_o_l = w_o[l].astype(BF16)
        if l % 2 == 0:
            x2, h = out_proj(moba_o, fox_o, moba_out_g[l], fox_out_g[l], w_o_l, x2, mod,
                             ffn_norm_g[l], seq, gate_row=2, sh_row=3)
            x2 = ffn_dense(h, dense_w_gate[i].astype(BF16), dense_w_up[i].astype(BF16),
                           dense_w_down[i].astype(BF16), x2, mod, seq, gate_row=5)
        else:
            x2, h, route = out_proj(moba_o, fox_o, moba_out_g[l], fox_out_g[l], w_o_l, x2, mod,
                                    ffn_norm_g[l], seq, gate_row=2, sh_row=3, router_w=router_w[i])
            x2 = ffn_moe(h, route, moe_w_gate[i], moe_w_up[i], moe_w_down[i], x2, mod, seq,
                         gate_row=5, final_g=final_norm_g if last else None)
            if last:
                out = x2
    if out is None:
        out = final_norm(x2, final_norm_g)
    return out.reshape(batch, seq, d)
```

```python
import functools
import math

import jax
import jax.numpy as jnp
from jax import lax
from jax.experimental import pallas as pl
from jax.experimental.pallas import tpu as pltpu

F32 = jnp.float32
BF16 = jnp.bfloat16

HEAD_DIM = 128
MOBA_BLOCK = 256
MOBA_TOPK = 3
ROPE_THETA = 10000.0
EPS = 1e-6
TOP_K = 2
LANES = 128
VMEM_LIMIT = 56 * 1024 * 1024

NEG_INF = float("-inf")
NEG_BIG = -(2.0 ** 100)
LOG2E = math.log2(math.e)
Q_SCALE = HEAD_DIM ** -0.5 * LOG2E


def _params(semantics, **kw):
    return pltpu.CompilerParams(dimension_semantics=semantics, vmem_limit_bytes=VMEM_LIMIT, **kw)


def _tile(n, want):
    t = min(n, want)
    assert n % t == 0, (n, want)
    return t


def _ada_kernel(c_ref, w_ref, b_ref, o_ref):
    c = c_ref[...]
    c_act = c * jax.nn.sigmoid(c)
    o_ref[...] = jnp.dot(c_act, w_ref[...], preferred_element_type=F32,
                         precision=lax.Precision.HIGHEST) + b_ref[...]


def ada_modulation(c, ada_w, ada_b):
    n_layers, d, d6 = ada_w.shape
    b = c.shape[0]
    rows = -(-b // 8) * 8
    c_pad = jnp.zeros((rows, d), F32).at[:b].set(c)
    tn = _tile(d6, 1024)
    out = pl.pallas_call(
        _ada_kernel,
        grid=(n_layers, d6 // tn),
        in_specs=[
            pl.BlockSpec((rows, d), lambda l, j: (0, 0)),
            pl.BlockSpec((None, d, tn), lambda l, j: (l, 0, j)),
            pl.BlockSpec((None, 1, tn), lambda l, j: (l, 0, j)),
        ],
        out_specs=pl.BlockSpec((None, rows, tn), lambda l, j: (l, 0, j)),
        out_shape=jax.ShapeDtypeStruct((n_layers, rows, d6), F32),
        compiler_params=_params(("arbitrary", "arbitrary")),
        name="ada_modulation",
    )(c_pad, ada_w, ada_b.reshape(n_layers, 1, d6))
    return out[:, :b].reshape(n_layers, b, 6, d)


def _rms(x, g):
    return x * lax.rsqrt(jnp.mean(x * x, axis=-1, keepdims=True) + EPS) * g


def _norm_mod_kernel(x_ref, g_ref, mod_ref, o_ref, *, sh_row):
    y = _rms(x_ref[...], g_ref[...])
    sh = mod_ref[sh_row:sh_row + 1, :]
    sc = mod_ref[sh_row + 1:sh_row + 2, :]
    o_ref[...] = (y * (1.0 + sc) + sh).astype(o_ref.dtype)


def _top2_route(logits, n_experts):
    lane = lax.broadcasted_iota(jnp.int32, logits.shape, 1).astype(F32)
    lg = jnp.where(lane < n_experts, logits, NEG_INF)
    m1 = jnp.max(lg, axis=-1, keepdims=True)
    i1 = jnp.min(jnp.where(lg == m1, lane, float(LANES)), axis=-1, keepdims=True)
    lg2 = jnp.where(lane == i1, NEG_INF, lg)
    m2 = jnp.max(lg2, axis=-1, keepdims=True)
    i2 = jnp.min(jnp.where(lg2 == m2, lane, float(LANES)), axis=-1, keepdims=True)
    e2 = jnp.exp(m2 - m1)
    w1 = 1.0 / (1.0 + e2)
    w2 = e2 / (1.0 + e2)
    return jnp.where(lane == 0.0, i1, jnp.where(lane == 1.0, i2,
                     jnp.where(lane == 2.0, w1, jnp.where(lane == 3.0, w2, 0.0))))


def norm_mod(x2, g, mod, seq, sh_row, tm=512):
    n, d = x2.shape
    tm = _tile(seq, tm)
    per_b = seq // tm
    return pl.pallas_call(
        functools.partial(_norm_mod_kernel, sh_row=sh_row),
        grid=(n // tm,),
        in_specs=[
            pl.BlockSpec((tm, d), lambda i: (i, 0)),
            pl.BlockSpec((1, d), lambda i: (0, 0)),
            pl.BlockSpec((None, 6, d), lambda i: (i // per_b, 0, 0)),
        ],
        out_specs=pl.BlockSpec((tm, d), lambda i: (i, 0)),
        out_shape=jax.ShapeDtypeStruct((n, d), BF16),
        compiler_params=_params(("arbitrary",)),
        name="norm_mod",
    )(x2, g.reshape(1, d), mod)


def _final_norm_kernel(x_ref, g_ref, o_ref):
    o_ref[...] = _rms(x_ref[...], g_ref[...])


def final_norm(x2, g, tm=512):
    n, d = x2.shape
    tm = _tile(n, tm)
    return pl.pallas_call(
        _final_norm_kernel,
        grid=(n // tm,),
        in_specs=[pl.BlockSpec((tm, d), lambda i: (i, 0)),
                  pl.BlockSpec((1, d), lambda i: (0, 0))],
        out_specs=pl.BlockSpec((tm, d), lambda i: (i, 0)),
        out_shape=jax.ShapeDtypeStruct((n, d), F32),
        compiler_params=_params(("arbitrary",)),
        name="final_norm",
    )(x2, g.reshape(1, d))


PLAIN, ROPE, ROPE_SCALED, SCALED = 0, 1, 2, 3


def _in_proj_kernel(h_ref, w_ref, wz_ref, cos_ref, sin_ref, o_ref, fz_ref, *, kinds):
    j = pl.program_id(1)
    tn = w_ref.shape[1]

    def is_kind(kind):
        hits = [jj for jj, k in enumerate(kinds) if k == kind]
        cond = j == hits[0]
        for jj in hits[1:]:
            cond = jnp.logical_or(cond, j == jj)
        return cond

    for kind in sorted(set(kinds)):
        @pl.when(is_kind(kind))
        def _(kind=kind):
            acc = jnp.dot(h_ref[...], w_ref[...], preferred_element_type=F32)
            mult = Q_SCALE if kind in (ROPE_SCALED, SCALED) else 1.0
            if kind in (ROPE, ROPE_SCALED):
                c = cos_ref[...] * mult
                s = sin_ref[...] * mult
                for hh in range(tn // HEAD_DIM):
                    t = acc[:, hh * HEAD_DIM:(hh + 1) * HEAD_DIM]
                    rot = pltpu.roll(t, HEAD_DIM // 2, 1)
                    o_ref[:, hh * HEAD_DIM:(hh + 1) * HEAD_DIM] = (t * c + rot * s).astype(o_ref.dtype)
            elif kind == SCALED:
                o_ref[...] = (acc * mult).astype(o_ref.dtype)
            else:
                o_ref[...] = acc.astype(o_ref.dtype)

    @pl.when(j == 0)
    def _():
        fz_ref[...] = jnp.dot(h_ref[...], wz_ref[...], preferred_element_type=F32)


def in_proj(h, w_qkv, wz, cos_t, sin_t, seq, kinds, tn, tm=1024):
    n, d = h.shape
    width = w_qkv.shape[1]
    assert width == tn * len(kinds)
    tm = _tile(seq, tm)
    per_b = seq // tm
    return pl.pallas_call(
        functools.partial(_in_proj_kernel, kinds=tuple(kinds)),
        grid=(n // tm, width // tn),
        in_specs=[
            pl.BlockSpec((tm, d), lambda i, j: (i, 0)),
            pl.BlockSpec((d, tn), lambda i, j: (0, j)),
            pl.BlockSpec((d, LANES), lambda i, j: (0, 0)),
            pl.BlockSpec((tm, HEAD_DIM), lambda i, j: (i % per_b, 0)),
            pl.BlockSpec((tm, HEAD_DIM), lambda i, j: (i % per_b, 0)),
        ],
        out_specs=[pl.BlockSpec((tm, tn), lambda i, j: (i, j)),
                   pl.BlockSpec((tm, LANES), lambda i, j: (i, 0))],
        out_shape=[jax.ShapeDtypeStruct((n, width), BF16),
                   jax.ShapeDtypeStruct((n, LANES), F32)],
        compiler_params=_params(("arbitrary", "arbitrary")),
        name="in_proj",
    )(h, w_qkv, wz, cos_t, sin_t)


def _logf_cumsum_kernel(z_ref, b_ref, o_ref):
    z = z_ref[...] + b_ref[...]
    lf = jnp.minimum(z, 0.0) - jnp.log1p(jnp.exp(-jnp.abs(z)))
    r = lax.broadcasted_iota(jnp.int32, (LANES, LANES), 0)
    c = lax.broadcasted_iota(jnp.int32, (LANES, LANES), 1)
    tri = jnp.where(c <= r, 1.0, 0.0).astype(F32)
    carry = jnp.zeros((1, z.shape[1]), F32)
    for ch in range(z.shape[0] // LANES):
        seg = lf[ch * LANES:(ch + 1) * LANES, :]
        cs = jnp.dot(tri, seg, preferred_element_type=F32, precision=lax.Precision.HIGHEST) + carry
        o_ref[ch * LANES:(ch + 1) * LANES, :] = cs
        carry = cs[LANES - 1:LANES, :]


def logf_cumsum(z, forget_b):
    b, s, h = z.shape
    return pl.pallas_call(
        _logf_cumsum_kernel,
        grid=(b,),
        in_specs=[pl.BlockSpec((None, s, h), lambda i: (i, 0, 0)),
                  pl.BlockSpec((1, h), lambda i: (0, 0))],
        out_specs=pl.BlockSpec((None, s, h), lambda i: (i, 0, 0)),
        out_shape=jax.ShapeDtypeStruct((b, s, h), F32),
        compiler_params=_params(("arbitrary",)),
        name="logf_cumsum",
    )(z, forget_b.reshape(1, h))


def _split3(f):
    hi = f.astype(BF16).astype(F32)
    r = f - hi
    mid = r.astype(BF16).astype(F32)
    lo = (r - mid).astype(BF16).astype(F32)
    return hi, mid, lo


def _pick_lane(x, idx):
    lane = lax.broadcasted_iota(jnp.int32, x.shape, 1)
    return jnp.sum(jnp.where(lane == idx, x, 0.0), axis=-1, keepdims=True)


def _attn_kernel(*refs, mode, tq, tk, seq):
    if mode == "fox":
        q_ref, k_ref, v_ref, ct_ref, o_ref, kaug, vaug, qaug, s_a, s_b, bias_scr, m_scr, acc_scr = refs
    else:
        (q_ref, k_ref, v_ref, o_ref, kaug, vaug, qaug, s_a, s_b, bias_scr, m_scr, acc_scr,
         kmean_scr, kmean3_scr) = refs
    hg = pl.program_id(1)
    i = pl.program_id(2)
    n_moba_blocks = seq // MOBA_BLOCK
    group = q_ref.shape[1] // HEAD_DIM

    def head_cols(g):
        return slice(g * HEAD_DIM, (g + 1) * HEAD_DIM)

    @pl.when(jnp.logical_and(jnp.logical_and(pl.program_id(0) == 0, hg == 0), i == 0))
    def _():
        for dd in range(2):
            r2 = lax.broadcasted_iota(jnp.int32, (tq, tk), 0)
            c2 = lax.broadcasted_iota(jnp.int32, (tq, tk), 1) + dd * tk
            ok = c2 <= r2
            if mode == "moba":
                ok = jnp.logical_or(ok, c2 // MOBA_BLOCK != r2 // MOBA_BLOCK)
            bias_scr[dd] = jnp.where(ok, 0.0, NEG_INF)

    @pl.when(i == 0)
    def _():
        lane = lax.broadcasted_iota(jnp.int32, (tk, LANES), 1)
        row = lax.broadcasted_iota(jnp.int32, (tk, LANES), 0)

        def build(cidx, carry):
            st = pl.multiple_of(cidx * tk, tk)
            for g in range(group):
                kaug[g, pl.ds(st, tk), 0:HEAD_DIM] = k_ref[pl.ds(st, tk), head_cols(g)]
                vaug[g, pl.ds(st, tk), 0:HEAD_DIM] = v_ref[pl.ds(st, tk), head_cols(g)]
                vaug[g, pl.ds(st, tk), HEAD_DIM:2 * HEAD_DIM] = jnp.ones((tk, HEAD_DIM), BF16)
                if mode == "fox":
                    c = _pick_lane(ct_ref[pl.ds(st, tk), :], hg * group + g) * LOG2E
                    hi, mid, lo = _split3(c)
                    ext = jnp.where(lane < 3, 1.0, jnp.where(lane == 3, -hi, jnp.where(
                        lane == 4, -mid, jnp.where(lane == 5, -lo, 0.0))))
                else:
                    ext = jnp.where(lane == (st + row) // MOBA_BLOCK, 1.0, 0.0)
                kaug[g, pl.ds(st, tk), HEAD_DIM:2 * HEAD_DIM] = ext.astype(BF16)
            return carry

        lax.fori_loop(0, seq // tk, build, 0)
        if mode == "moba":
            kmean_scr[...] = jnp.zeros_like(kmean_scr)
            for g in range(group):
                for b in range(n_moba_blocks):
                    kb = k_ref[b * MOBA_BLOCK:(b + 1) * MOBA_BLOCK, head_cols(g)].astype(F32)
                    kmean_scr[g, b:b + 1, :] = jnp.mean(kb, axis=0, keepdims=True)
                for piece, part in enumerate(_split3(kmean_scr[g])):
                    kmean3_scr[g, piece * LANES:(piece + 1) * LANES, :] = part.astype(BF16)

    lane = lax.broadcasted_iota(jnp.int32, (MOBA_BLOCK, LANES), 1)
    for rb in range(tq // MOBA_BLOCK):
        rows = pl.ds(rb * MOBA_BLOCK, MOBA_BLOCK)
        for g in range(group):
            q = q_ref[rows, head_cols(g)]
            if mode == "fox":
                start = pl.multiple_of(i * tq + rb * MOBA_BLOCK, MOBA_BLOCK)
                f = _pick_lane(ct_ref[pl.ds(start, MOBA_BLOCK), :], hg * group + g) * LOG2E
                hi, mid, lo = _split3(f)
                ext = jnp.where(lane == 0, hi, jnp.where(lane == 1, mid, jnp.where(
                    lane == 2, lo, jnp.where(lane < 6, 1.0, 0.0))))
            else:
                g3 = lax.dot_general(q, kmean3_scr[g], (((1,), (1,)), ((), ())),
                                     preferred_element_type=F32)
                gate = g3[:, 0:LANES] + g3[:, LANES:2 * LANES] + g3[:, 2 * LANES:3 * LANES]
                own = i * (tq // MOBA_BLOCK) + rb
                lane_f = lane.astype(F32)
                gv = jnp.where(lane < own, gate, NEG_INF)
                keep = lane == own
                for _ in range(min(MOBA_TOPK, n_moba_blocks)):
                    m = jnp.max(gv, axis=-1, keepdims=True)
                    idx = jnp.min(jnp.where(gv == m, lane_f, float(LANES)), axis=-1, keepdims=True)
                    hit = lane_f == idx
                    keep = jnp.logical_or(keep, jnp.logical_and(hit, m > NEG_INF))
                    gv = jnp.where(hit, NEG_INF, gv)
                ext = jnp.where(keep, 0.0, NEG_BIG)
            qaug[g, rows, 0:HEAD_DIM] = q
            qaug[g, rows, HEAD_DIM:2 * HEAD_DIM] = ext.astype(BF16)

    def scores(g, j):
        st = pl.multiple_of(j * tk, tk)
        return lax.dot_general(qaug[g], kaug[g, pl.ds(st, tk), :], (((1,), (1,)), ((), ())),
                               preferred_element_type=F32)

    def update(g, s, j, diag_off):
        st = pl.multiple_of(j * tk, tk)
        if diag_off is not None:
            s = s + bias_scr[diag_off // tk]
        m_old = m_scr[g]
        m_new = jnp.maximum(m_old, jnp.max(s, axis=-1, keepdims=True))
        p = jnp.exp2(s - m_new).astype(BF16)
        pv = jnp.dot(p, vaug[g, pl.ds(st, tk), :], preferred_element_type=F32)
        acc_scr[g] = jnp.exp2(m_old - m_new) * acc_scr[g] + pv
        m_scr[g] = m_new

    heads = range(group)
    m_scr[...] = jnp.full(m_scr.shape, NEG_INF, F32)
    acc_scr[...] = jnp.zeros(acc_scr.shape, F32)
    for g in heads:
        s_a[g] = scores(g, 0)

    def body(jj, carry):
        for g in heads:
            s_b[g] = scores(g, 2 * jj + 1)
        for g in heads:
            update(g, s_a[g], 2 * jj, None)
        for g in heads:
            s_a[g] = scores(g, 2 * jj + 2)
        for g in heads:
            update(g, s_b[g], 2 * jj + 1, None)
        return carry

    lax.fori_loop(0, i, body, 0)
    for g in heads:
        s_b[g] = scores(g, 2 * i + 1)
    for g in heads:
        update(g, s_a[g], 2 * i, 0)
    for g in heads:
        update(g, s_b[g], 2 * i + 1, tk)
    for g in heads:
        o_ref[:, head_cols(g)] = (acc_scr[g, :, 0:HEAD_DIM]
                                  / acc_scr[g, :, HEAD_DIM:2 * HEAD_DIM]).astype(o_ref.dtype)


def attention(proj, batch, seq, heads, q_col, k_col, v_col, cum_t=None, tq=1024, group=2):
    n = proj.shape[0]
    tq = _tile(seq, tq)
    tk = tq // 2
    nq = seq // tq
    mode = "moba" if cum_t is None else "fox"
    assert tk % MOBA_BLOCK == 0 and seq // MOBA_BLOCK <= LANES
    gw = group * HEAD_DIM
    assert heads % group == 0 and q_col % group == 0 and k_col % group == 0 and v_col % group == 0
    in_specs = [
        pl.BlockSpec((tq, gw), lambda b, h, i: (b * nq + i, q_col // group + h)),
        pl.BlockSpec((seq, gw), lambda b, h, i: (b, k_col // group + h)),
        pl.BlockSpec((seq, gw), lambda b, h, i: (b, v_col // group + h)),
    ]
    args = [proj, proj, proj]
    wide = 2 * HEAD_DIM
    scratch = [pltpu.VMEM((group, seq, wide), BF16), pltpu.VMEM((group, seq, wide), BF16),
               pltpu.VMEM((group, tq, wide), BF16),
               pltpu.VMEM((group, tq, tk), F32), pltpu.VMEM((group, tq, tk), F32),
               pltpu.VMEM((2, tq, tk), F32),
               pltpu.VMEM((group, tq, 1), F32), pltpu.VMEM((group, tq, wide), F32)]
    if mode == "fox":
        in_specs.append(pl.BlockSpec((None, seq, heads), lambda b, h, i: (b, 0, 0)))
        args.append(cum_t)
    else:
        scratch += [pltpu.VMEM((group, LANES, HEAD_DIM), F32),
                    pltpu.VMEM((group, 3 * LANES, HEAD_DIM), BF16)]
    return pl.pallas_call(
        functools.partial(_attn_kernel, mode=mode, tq=tq, tk=tk, seq=seq),
        grid=(batch, heads // group, nq),
        in_specs=in_specs,
        out_specs=pl.BlockSpec((tq, gw), lambda b, h, i: (b * nq + i, h)),
        out_shape=jax.ShapeDtypeStruct((n, heads * HEAD_DIM), BF16),
        scratch_shapes=scratch,
        compiler_params=_params(("arbitrary", "arbitrary", "arbitrary")),
        name=mode + "_attention",
    )(*args)


def _out_proj_kernel(*refs, gate_row, sh_row, n_experts):
    if n_experts:
        (mo_ref, fo_ref, g1_ref, g2_ref, w_ref, x_ref, mod_ref, gf_ref, wr_ref,
         o_ref, h_ref, route_ref) = refs
    else:
        mo_ref, fo_ref, g1_ref, g2_ref, w_ref, x_ref, mod_ref, gf_ref, o_ref, h_ref = refs
    half = mo_ref.shape[1]
    m1 = _rms(mo_ref[...].astype(F32), g1_ref[...]).astype(BF16)
    m2 = _rms(fo_ref[...].astype(F32), g2_ref[...]).astype(BF16)
    acc = jnp.dot(m1, w_ref[0:half, :], preferred_element_type=F32)
    acc = acc + jnp.dot(m2, w_ref[half:2 * half, :], preferred_element_type=F32)
    x_new = x_ref[...] + mod_ref[gate_row:gate_row + 1, :] * acc
    o_ref[...] = x_new
    sh = mod_ref[sh_row:sh_row + 1, :]
    sc = mod_ref[sh_row + 1:sh_row + 2, :]
    h = _rms(x_new, gf_ref[...]) * (1.0 + sc) + sh
    h_hi = h.astype(BF16)
    h_ref[...] = h_hi
    if n_experts:
        h_lo = (h - h_hi.astype(F32)).astype(BF16)
        l_hi = jnp.dot(h_hi, wr_ref[...], preferred_element_type=F32)
        l_lo = jnp.dot(h_lo, wr_ref[:, 0:LANES], preferred_element_type=F32)
        logits = l_hi[:, 0:LANES] + l_hi[:, LANES:2 * LANES] + l_lo
        route_ref[...] = _top2_route(logits, n_experts)


def out_proj(moba_o, fox_o, g1, g2, w_o, x2, mod, ffn_g, seq, gate_row, sh_row, router_w=None, tm=512):
    n, d = x2.shape
    half = moba_o.shape[1]
    tm = _tile(seq, tm)
    per_b = seq // tm
    n_experts = 0 if router_w is None else router_w.shape[1]
    in_specs = [
        pl.BlockSpec((tm, half), lambda i: (i, 0)),
        pl.BlockSpec((tm, half), lambda i: (i, 0)),
        pl.BlockSpec((1, half), lambda i: (0, 0)),
        pl.BlockSpec((1, half), lambda i: (0, 0)),
        pl.BlockSpec((2 * half, d), lambda i: (0, 0)),
        pl.BlockSpec((tm, d), lambda i: (i, 0)),
        pl.BlockSpec((None, 6, d), lambda i: (i // per_b, 0, 0)),
        pl.BlockSpec((1, d), lambda i: (0, 0)),
    ]
    args = [moba_o, fox_o, g1.reshape(1, half), g2.reshape(1, half), w_o, x2, mod, ffn_g.reshape(1, d)]
    out_specs = [pl.BlockSpec((tm, d), lambda i: (i, 0)), pl.BlockSpec((tm, d), lambda i: (i, 0))]
    out_shape = [jax.ShapeDtypeStruct((n, d), F32), jax.ShapeDtypeStruct((n, d), BF16)]
    if n_experts:
        in_specs.append(pl.BlockSpec((d, 2 * LANES), lambda i: (0, 0)))
        w_hi = router_w.astype(BF16)
        w_lo = (router_w - w_hi.astype(F32)).astype(BF16)
        args.append(jnp.zeros((d, 2 * LANES), BF16).at[:, :n_experts].set(w_hi)
                    .at[:, LANES:LANES + n_experts].set(w_lo))
        out_specs.append(pl.BlockSpec((tm, LANES), lambda i: (i, 0)))
        out_shape.append(jax.ShapeDtypeStruct((n, LANES), F32))
    return pl.pallas_call(
        functools.partial(_out_proj_kernel, gate_row=gate_row, sh_row=sh_row, n_experts=n_experts),
        grid=(n // tm,),
        in_specs=in_specs,
        out_specs=out_specs,
        out_shape=out_shape,
        compiler_params=_params(("arbitrary",)),
        name="out_proj",
    )(*args)


def _swiglu_tile(h, wg, wu):
    a = jnp.dot(h, wg, preferred_element_type=F32)
    b = jnp.dot(h, wu, preferred_element_type=F32)
    return a * jax.nn.sigmoid(a) * b


def _ffn_kernel(h_ref, wg_ref, wu_ref, wd_ref, x_ref, mod_ref, o_ref, *, gate_row):
    f = pl.program_id(1)
    t = _swiglu_tile(h_ref[...], wg_ref[...], wu_ref[...]).astype(BF16)

    @pl.when(f == 0)
    def _():
        o_ref[...] = jnp.dot(t, wd_ref[...], preferred_element_type=F32)

    @pl.when(f > 0)
    def _():
        o_ref[...] += jnp.dot(t, wd_ref[...], preferred_element_type=F32)

    @pl.when(f == pl.num_programs(1) - 1)
    def _():
        o_ref[...] = x_ref[...] + mod_ref[gate_row:gate_row + 1, :] * o_ref[...]


def ffn_dense(h, wg, wu, wd, x2, mod, seq, gate_row, tm=1024, tf=512):
    n, d = x2.shape
    ff = wg.shape[1]
    tm = _tile(seq, tm)
    tf = _tile(ff, tf)
    per_b = seq // tm
    return pl.pallas_call(
        functools.partial(_ffn_kernel, gate_row=gate_row),
        grid=(n // tm, ff // tf),
        in_specs=[
            pl.BlockSpec((tm, d), lambda i, f: (i, 0)),
            pl.BlockSpec((d, tf), lambda i, f: (0, f)),
            pl.BlockSpec((d, tf), lambda i, f: (0, f)),
            pl.BlockSpec((tf, d), lambda i, f: (f, 0)),
            pl.BlockSpec((tm, d), lambda i, f: (i, 0), pipeline_mode=pl.Buffered(1)),
            pl.BlockSpec((None, 6, d), lambda i, f: (i // per_b, 0, 0)),
        ],
        out_specs=pl.BlockSpec((tm, d), lambda i, f: (i, 0)),
        out_shape=jax.ShapeDtypeStruct((n, d), F32),
        compiler_params=_params(("arbitrary", "arbitrary")),
        name="ffn_dense",
    )(h, wg, wu, wd, x2, mod)


def _route_plan(route, n_experts, tm):
    n = route.shape[0]
    pairs = TOP_K * n
    e_flat = route[:, :TOP_K].astype(jnp.int32).reshape(pairs)
    onehot = (e_flat[:, None] == jnp.arange(n_experts, dtype=jnp.int32)[None, :]).astype(jnp.int32)
    csum = jnp.cumsum(onehot, axis=0)
    counts = csum[-1]
    padded = (counts + tm - 1) // tm * tm
    pend = jnp.cumsum(padded)
    pstart = pend - padded
    rank = jnp.sum(csum * onehot, axis=1) - 1
    pos = (jnp.sum(pstart[None, :] * onehot, axis=1) + rank).astype(jnp.int32)

    n_tiles = pairs // tm + n_experts
    first_row = jnp.arange(n_tiles, dtype=jnp.int32) * tm
    tile_expert = jnp.minimum(jnp.sum((pend[None, :] <= first_row[:, None]).astype(jnp.int32), axis=1),
                              n_experts - 1)
    valid = first_row < pend[-1]
    return pos, (tile_expert.astype(jnp.int32), valid.astype(jnp.int32),
                 (pstart + counts).astype(jnp.int32))


def _dispatch_kernel(pos_ref, h_ref, init_ref, xs_ref, sem):
    del init_ref
    i = pl.program_id(0)
    tm = h_ref.shape[0]

    def issue(r, carry):
        for k in range(TOP_K):
            dst = pos_ref[(i * tm + r) * TOP_K + k]
            pltpu.make_async_copy(h_ref.at[r], xs_ref.at[dst], sem).start()
        return carry

    lax.fori_loop(0, tm, issue, 0)
    for k in range(TOP_K):
        pltpu.make_async_copy(h_ref, xs_ref.at[pl.ds(0, tm)], sem).wait()


def moe_dispatch(h3, pos, rows, tm=512):
    n, sub, lanes = h3.shape
    tm = _tile(n, tm)
    return pl.pallas_call(
        _dispatch_kernel,
        grid_spec=pltpu.PrefetchScalarGridSpec(
            num_scalar_prefetch=1,
            grid=(n // tm,),
            in_specs=[pl.BlockSpec((tm, sub, lanes), lambda i, pos: (i, 0, 0)),
                      pl.BlockSpec(memory_space=pl.ANY)],
            out_specs=pl.BlockSpec(memory_space=pl.ANY),
            scratch_shapes=[pltpu.SemaphoreType.DMA(())],
        ),
        out_shape=jax.ShapeDtypeStruct((rows, sub, lanes), h3.dtype),
        input_output_aliases={2: 0},
        compiler_params=_params(("arbitrary",)),
        name="moe_dispatch",
    )(pos, h3, jnp.zeros((rows, sub, lanes), h3.dtype))


def _moe_ffn_kernel(exp_ref, valid_ref, rowend_ref, xs_ref, wg_ref, wu_ref, wd_ref, o_ref):
    t = pl.program_id(0)
    f = pl.program_id(1)
    tm = xs_ref.shape[0]

    @pl.when(valid_ref[t] == 1)
    def _():
        rows = t * tm + lax.broadcasted_iota(jnp.int32, (tm, 1), 0)
        real = rows < rowend_ref[exp_ref[t]]
        a = _swiglu_tile(xs_ref[...], wg_ref[...].astype(BF16), wu_ref[...].astype(BF16))
        a = jnp.where(real, a, 0.0).astype(BF16)
        wd = wd_ref[...].astype(BF16)

        @pl.when(f == 0)
        def _():
            o_ref[...] = jnp.dot(a, wd, preferred_element_type=F32)

        @pl.when(f > 0)
        def _():
            o_ref[...] += jnp.dot(a, wd, preferred_element_type=F32)

    @pl.when(jnp.logical_and(valid_ref[t] == 0, f == 0))
    def _():
        o_ref[...] = jnp.zeros(o_ref.shape, o_ref.dtype)


def moe_ffn(xs, plan, wg, wu, wd, tm, tf=256):
    rows, d = xs.shape
    n_experts, _, ff = wg.shape
    tf = _tile(ff, tf)
    nf = ff // tf

    def f_idx(t, f, va):
        return jnp.where(va[t] == 1, f, nf - 1)

    return pl.pallas_call(
        _moe_ffn_kernel,
        grid_spec=pltpu.PrefetchScalarGridSpec(
            num_scalar_prefetch=3,
            grid=(rows // tm, nf),
            in_specs=[
                pl.BlockSpec((tm, d), lambda t, f, ex, va, re: (t, 0)),
                pl.BlockSpec((None, d, tf), lambda t, f, ex, va, re: (ex[t], 0, f_idx(t, f, va))),
                pl.BlockSpec((None, d, tf), lambda t, f, ex, va, re: (ex[t], 0, f_idx(t, f, va))),
                pl.BlockSpec((None, tf, d), lambda t, f, ex, va, re: (ex[t], f_idx(t, f, va), 0)),
            ],
            out_specs=pl.BlockSpec((tm, d), lambda t, f, ex, va, re: (t, 0)),
        ),
        out_shape=jax.ShapeDtypeStruct((rows, d), F32),
        compiler_params=_params(("arbitrary", "arbitrary")),
        name="moe_ffn",
    )(*plan, xs, wg, wu, wd)


def _combine_kernel(pos_ref, ys_ref, route_ref, x_ref, mod_ref, g_ref, o_ref, buf, sem,
                    *, gate_row, final):
    i = pl.program_id(0)
    tm = x_ref.shape[0]
    slot = i % 2

    def gather(step, into):
        def issue(r, carry):
            for k in range(TOP_K):
                src = pos_ref[(step * tm + r) * TOP_K + k]
                pltpu.make_async_copy(ys_ref.at[pl.ds(src, 1), :], buf.at[into, k, pl.ds(r, 1), :],
                                      sem.at[into]).start()
            return carry

        lax.fori_loop(0, tm, issue, 0)

    @pl.when(i == 0)
    def _():
        gather(0, 0)

    @pl.when(i + 1 < pl.num_programs(0))
    def _():
        gather(i + 1, 1 - slot)

    for k in range(TOP_K):
        pltpu.make_async_copy(ys_ref.at[pl.ds(0, tm), :], buf.at[slot, k], sem.at[slot]).wait()
    route = route_ref[...]
    y = buf[slot, 0] * route[:, TOP_K:TOP_K + 1]
    for k in range(1, TOP_K):
        y = y + buf[slot, k] * route[:, TOP_K + k:TOP_K + k + 1]
    x_new = x_ref[...] + mod_ref[gate_row:gate_row + 1, :] * y
    o_ref[...] = _rms(x_new, g_ref[...]) if final else x_new


def moe_combine(ys, pos, route, x2, mod, seq, gate_row, final_g=None, tm=256):
    n, d = x2.shape
    tm = _tile(seq, tm)
    per_b = seq // tm
    g = jnp.ones((1, d), F32) if final_g is None else final_g.reshape(1, d)
    return pl.pallas_call(
        functools.partial(_combine_kernel, gate_row=gate_row, final=final_g is not None),
        grid_spec=pltpu.PrefetchScalarGridSpec(
            num_scalar_prefetch=1,
            grid=(n // tm,),
            in_specs=[
                pl.BlockSpec(memory_space=pl.ANY),
                pl.BlockSpec((tm, LANES), lambda i, pos: (i, 0)),
                pl.BlockSpec((tm, d), lambda i, pos: (i, 0)),
                pl.BlockSpec((None, 6, d), lambda i, pos: (i // per_b, 0, 0)),
                pl.BlockSpec((1, d), lambda i, pos: (0, 0)),
            ],
            out_specs=pl.BlockSpec((tm, d), lambda i, pos: (i, 0)),
            scratch_shapes=[pltpu.VMEM((2, TOP_K, tm, d), F32), pltpu.SemaphoreType.DMA((2,))],
        ),
        out_shape=jax.ShapeDtypeStruct((n, d), F32),
        compiler_params=_params(("arbitrary",)),
        name="moe_combine",
    )(pos, ys, route, x2, mod, g)


def ffn_moe(h, route, wg, wu, wd, x2, mod, seq, gate_row, final_g=None, tm=1024):
    n, d = x2.shape
    n_experts = wg.shape[0]
    tm = _tile(TOP_K * n, tm)
    pos, plan = _route_plan(route, n_experts, tm)
    rows = TOP_K * n + n_experts * tm
    xs3 = moe_dispatch(h.reshape(n, d // LANES, LANES), pos, rows)
    ys = moe_ffn(xs3.reshape(rows, d), plan, wg, wu, wd, tm)
    return moe_combine(ys, pos, route, x2, mod, seq, gate_row, final_g)


def _rope_tables(seq):
    inv = ROPE_THETA ** (-jnp.arange(0, HEAD_DIM, 2, dtype=F32) / HEAD_DIM)
    ang = jnp.arange(seq, dtype=F32)[:, None] * inv[None, :]
    cos, sin = jnp.cos(ang), jnp.sin(ang)
    return jnp.concatenate([cos, cos], axis=-1), jnp.concatenate([-sin, sin], axis=-1)


def kernel(x, c, attn_norm_g, ada_w, ada_b, w_in, forget_b, moba_out_g, fox_out_g, w_o, ffn_norm_g,
           dense_w_gate, dense_w_up, dense_w_down, router_w, moe_w_gate, moe_w_up, moe_w_down,
           final_norm_g):
    batch, seq, d = x.shape
    depth = attn_norm_g.shape[0]
    moba_w = moba_out_g.shape[1]
    fox_w = fox_out_g.shape[1]
    moba_h = moba_w // HEAD_DIM
    fox_h = fox_w // HEAD_DIM
    qkv_w = 3 * moba_w + 3 * fox_w
    assert moba_w == fox_w
    kinds = (ROPE_SCALED, ROPE, PLAIN, SCALED, PLAIN, PLAIN)
    hb = HEAD_DIM

    cos_t, sin_t = _rope_tables(seq)
    mods = ada_modulation(c, ada_w, ada_b)
    x2 = x.reshape(batch * seq, d)
    out = None

    for l in range(depth):
        mod = mods[l]
        h = norm_mod(x2, attn_norm_g[l], mod, seq, sh_row=0)
        w_qkv = w_in[l][:, :qkv_w].astype(BF16)
        wz = jnp.zeros((d, LANES), BF16).at[:, :fox_h].set(w_in[l][:, qkv_w:].astype(BF16))
        proj, fz = in_proj(h, w_qkv, wz, cos_t, sin_t, seq, kinds, tn=moba_w)
        cum_t = logf_cumsum(fz[:, :fox_h].reshape(batch, seq, fox_h), forget_b[l])
        moba_o = attention(proj, batch, seq, moba_h, q_col=0, k_col=moba_w // hb,
                           v_col=2 * moba_w // hb)
        fc = 3 * moba_w // hb
        fox_o = attention(proj, batch, seq, fox_h, q_col=fc, k_col=fc + fox_w // hb,
                          v_col=fc + 2 * fox_w // hb, cum_t=cum_t)
        i = l // 2
        last = l == depth - 1
        w_o_l = w_o[l].astype(BF16)
        if l % 2 == 0:
            x2, h = out_proj(moba_o, fox_o, moba_out_g[l], fox_out_g[l], w_o_l, x2, mod,
                             ffn_norm_g[l], seq, gate_row=2, sh_row=3)
            x2 = ffn_dense(h, dense_w_gate[i].astype(BF16), dense_w_up[i].astype(BF16),
                           dense_w_down[i].astype(BF16), x2, mod, seq, gate_row=5)
        else:
            x2, h, route = out_proj(moba_o, fox_o, moba_out_g[l], fox_out_g[l], w_o_l, x2, mod,
                                    ffn_norm_g[l], seq, gate_row=2, sh_row=3, router_w=router_w[i])
            x2 = ffn_moe(h, route, moe_w_gate[i], moe_w_up[i], moe_w_down[i], x2, mod, seq,
                         gate_row=5, final_g=final_norm_g if last else None)
            if last:
                out = x2
    if out is None:
        out = final_norm(x2, final_norm_g)
    return out.reshape(batch, seq, d)
```

```python
import functools
import math

import jax
import jax.numpy as jnp
from jax import lax
from jax.experimental import pallas as pl
from jax.experimental.pallas import tpu as pltpu

F32 = jnp.float32
BF16 = jnp.bfloat16

HEAD_DIM = 128
MOBA_BLOCK = 256
MOBA_TOPK = 3
ROPE_THETA = 10000.0
EPS = 1e-6
TOP_K = 2
LANES = 128
VMEM_LIMIT = 56 * 1024 * 1024

NEG_INF = float("-inf")
NEG_BIG = -(2.0 ** 100)
LOG2E = math.log2(math.e)
Q_SCALE = HEAD_DIM ** -0.5 * LOG2E


def _params(semantics, **kw):
    return pltpu.CompilerParams(dimension_semantics=semantics, vmem_limit_bytes=VMEM_LIMIT, **kw)


def _tile(n, want):
    t = min(n, want)
    assert n % t == 0, (n, want)
    return t


def _ada_kernel(c_ref, w_ref, b_ref, o_ref):
    c = c_ref[...]
    c_act = c * jax.nn.sigmoid(c)
    o_ref[...] = jnp.dot(c_act, w_ref[...], preferred_element_type=F32,
                         precision=lax.Precision.HIGHEST) + b_ref[...]


def ada_modulation(c, ada_w, ada_b):
    n_layers, d, d6 = ada_w.shape
    b = c.shape[0]
    rows = -(-b // 8) * 8
    c_pad = jnp.zeros((rows, d), F32).at[:b].set(c)
    tn = _tile(d6, 1024)
    out = pl.pallas_call(
        _ada_kernel,
        grid=(n_layers, d6 // tn),
        in_specs=[
            pl.BlockSpec((rows, d), lambda l, j: (0, 0)),
            pl.BlockSpec((None, d, tn), lambda l, j: (l, 0, j)),
            pl.BlockSpec((None, 1, tn), lambda l, j: (l, 0, j)),
        ],
        out_specs=pl.BlockSpec((None, rows, tn), lambda l, j: (l, 0, j)),
        out_shape=jax.ShapeDtypeStruct((n_layers, rows, d6), F32),
        compiler_params=_params(("arbitrary", "arbitrary")),
        name="ada_modulation",
    )(c_pad, ada_w, ada_b.reshape(n_layers, 1, d6))
    return out[:, :b].reshape(n_layers, b, 6, d)


def _rms(x, g):
    return x * lax.rsqrt(jnp.mean(x * x, axis=-1, keepdims=True) + EPS) * g


def _norm_mod_kernel(x_ref, g_ref, mod_ref, o_ref, *, sh_row):
    y = _rms(x_ref[...], g_ref[...])
    sh = mod_ref[sh_row:sh_row + 1, :]
    sc = mod_ref[sh_row + 1:sh_row + 2, :]
    o_ref[...] = (y * (1.0 + sc) + sh).astype(o_ref.dtype)


def _top2_route(logits, n_experts):
    lane = lax.broadcasted_iota(jnp.int32, logits.shape, 1).astype(F32)
    lg = jnp.where(lane < n_experts, logits, NEG_INF)
    m1 = jnp.max(lg, axis=-1, keepdims=True)
    i1 = jnp.min(jnp.where(lg == m1, lane, float(LANES)), axis=-1, keepdims=True)
    lg2 = jnp.where(lane == i1, NEG_INF, lg)
    m2 = jnp.max(lg2, axis=-1, keepdims=True)
    i2 = jnp.min(jnp.where(lg2 == m2, lane, float(LANES)), axis=-1, keepdims=True)
    e2 = jnp.exp(m2 - m1)
    w1 = 1.0 / (1.0 + e2)
    w2 = e2 / (1.0 + e2)
    return jnp.where(lane == 0.0, i1, jnp.where(lane == 1.0, i2,
                     jnp.where(lane == 2.0, w1, jnp.where(lane == 3.0, w2, 0.0))))


def norm_mod(x2, g, mod, seq, sh_row, tm=512):
    n, d = x2.shape
    tm = _tile(seq, tm)
    per_b = seq // tm
    return pl.pallas_call(
        functools.partial(_norm_mod_kernel, sh_row=sh_row),
        grid=(n // tm,),
        in_specs=[
            pl.BlockSpec((tm, d), lambda i: (i, 0)),
            pl.BlockSpec((1, d), lambda i: (0, 0)),
            pl.BlockSpec((None, 6, d), lambda i: (i // per_b, 0, 0)),
        ],
        out_specs=pl.BlockSpec((tm, d), lambda i: (i, 0)),
        out_shape=jax.ShapeDtypeStruct((n, d), BF16),
        compiler_params=_params(("arbitrary",)),
        name="norm_mod",
    )(x2, g.reshape(1, d), mod)


def _final_norm_kernel(x_ref, g_ref, o_ref):
    o_ref[...] = _rms(x_ref[...], g_ref[...])


def final_norm(x2, g, tm=512):
    n, d = x2.shape
    tm = _tile(n, tm)
    return pl.pallas_call(
        _final_norm_kernel,
        grid=(n // tm,),
        in_specs=[pl.BlockSpec((tm, d), lambda i: (i, 0)),
                  pl.BlockSpec((1, d), lambda i: (0, 0))],
        out_specs=pl.BlockSpec((tm, d), lambda i: (i, 0)),
        out_shape=jax.ShapeDtypeStruct((n, d), F32),
        compiler_params=_params(("arbitrary",)),
        name="final_norm",
    )(x2, g.reshape(1, d))


PLAIN, ROPE, ROPE_SCALED, SCALED = 0, 1, 2, 3


def _in_proj_kernel(h_ref, w_ref, wz_ref, cos_ref, sin_ref, o_ref, fz_ref, *, kinds):
    j = pl.program_id(1)
    tn = w_ref.shape[1]

    def is_kind(kind):
        hits = [jj for jj, k in enumerate(kinds) if k == kind]
        cond = j == hits[0]
        for jj in hits[1:]:
            cond = jnp.logical_or(cond, j == jj)
        return cond

    for kind in sorted(set(kinds)):
        @pl.when(is_kind(kind))
        def _(kind=kind):
            acc = jnp.dot(h_ref[...], w_ref[...], preferred_element_type=F32)
            mult = Q_SCALE if kind in (ROPE_SCALED, SCALED) else 1.0
            if kind in (ROPE, ROPE_SCALED):
                c = cos_ref[...] * mult
                s = sin_ref[...] * mult
                for hh in range(tn // HEAD_DIM):
                    t = acc[:, hh * HEAD_DIM:(hh + 1) * HEAD_DIM]
                    rot = pltpu.roll(t, HEAD_DIM // 2, 1)
                    o_ref[:, hh * HEAD_DIM:(hh + 1) * HEAD_DIM] = (t * c + rot * s).astype(o_ref.dtype)
            elif kind == SCALED:
                o_ref[...] = (acc * mult).astype(o_ref.dtype)
            else:
                o_ref[...] = acc.astype(o_ref.dtype)

    @pl.when(j == 0)
    def _():
        fz_ref[...] = jnp.dot(h_ref[...], wz_ref[...], preferred_element_type=F32)


def in_proj(h, w_qkv, wz, cos_t, sin_t, seq, kinds, tn, tm=1024):
    n, d = h.shape
    width = w_qkv.shape[1]
    assert width == tn * len(kinds)
    tm = _tile(seq, tm)
    per_b = seq // tm
    return pl.pallas_call(
        functools.partial(_in_proj_kernel, kinds=tuple(kinds)),
        grid=(n // tm, width // tn),
        in_specs=[
            pl.BlockSpec((tm, d), lambda i, j: (i, 0)),
            pl.BlockSpec((d, tn), lambda i, j: (0, j)),
            pl.BlockSpec((d, LANES), lambda i, j: (0, 0)),
            pl.BlockSpec((tm, HEAD_DIM), lambda i, j: (i % per_b, 0)),
            pl.BlockSpec((tm, HEAD_DIM), lambda i, j: (i % per_b, 0)),
        ],
        out_specs=[pl.BlockSpec((tm, tn), lambda i, j: (i, j)),
                   pl.BlockSpec((tm, LANES), lambda i, j: (i, 0))],
        out_shape=[jax.ShapeDtypeStruct((n, width), BF16),
                   jax.ShapeDtypeStruct((n, LANES), F32)],
        compiler_params=_params(("arbitrary", "arbitrary")),
        name="in_proj",
    )(h, w_qkv, wz, cos_t, sin_t)


def _logf_cumsum_kernel(z_ref, b_ref, o_ref):
    z = z_ref[...] + b_ref[...]
    lf = jnp.minimum(z, 0.0) - jnp.log1p(jnp.exp(-jnp.abs(z)))
    r = lax.broadcasted_iota(jnp.int32, (LANES, LANES), 0)
    c = lax.broadcasted_iota(jnp.int32, (LANES, LANES), 1)
    tri = jnp.where(c <= r, 1.0, 0.0).astype(F32)
    carry = jnp.zeros((1, z.shape[1]), F32)
    for ch in range(z.shape[0] // LANES):
        seg = lf[ch * LANES:(ch + 1) * LANES, :]
        cs = jnp.dot(tri, seg, preferred_element_type=F32, precision=lax.Precision.HIGHEST) + carry
        o_ref[ch * LANES:(ch + 1) * LANES, :] = cs
        carry = cs[LANES - 1:LANES, :]


def logf_cumsum(z, forget_b):
    b, s, h = z.shape
    return pl.pallas_call(
        _logf_cumsum_kernel,
        grid=(b,),
        in_specs=[pl.BlockSpec((None, s, h), lambda i: (i, 0, 0)),
                  pl.BlockSpec((1, h), lambda i: (0, 0))],
        out_specs=pl.BlockSpec((None, s, h), lambda i: (i, 0, 0)),
        out_shape=jax.ShapeDtypeStruct((b, s, h), F32),
        compiler_params=_params(("arbitrary",)),
        name="logf_cumsum",
    )(z, forget_b.reshape(1, h))


def _split3(f):
    hi = f.astype(BF16).astype(F32)
    r = f - hi
    mid = r.astype(BF16).astype(F32)
    lo = (r - mid).astype(BF16).astype(F32)
    return hi, mid, lo


def _pick_lane(x, idx):
    lane = lax.broadcasted_iota(jnp.int32, x.shape, 1)
    return jnp.sum(jnp.where(lane == idx, x, 0.0), axis=-1, keepdims=True)


def _attn_kernel(*refs, mode, tq, tk, seq):
    if mode == "fox":
        q_ref, k_ref, v_ref, ct_ref, o_ref, kaug, vaug, qaug, s_a, s_b, bias_scr, m_scr, acc_scr = refs
    else:
        (q_ref, k_ref, v_ref, o_ref, kaug, vaug, qaug, s_a, s_b, bias_scr, m_scr, acc_scr,
         kmean_scr, kmean3_scr) = refs
    hg = pl.program_id(1)
    i = pl.program_id(2)
    n_moba_blocks = seq // MOBA_BLOCK
    nbp = -(-n_moba_blocks // 16) * 16
    group = q_ref.shape[1] // HEAD_DIM

    def head_cols(g):
        return slice(g * HEAD_DIM, (g + 1) * HEAD_DIM)

    @pl.when(jnp.logical_and(jnp.logical_and(pl.program_id(0) == 0, hg == 0), i == 0))
    def _():
        r2 = lax.broadcasted_iota(jnp.int32, (tk, tk), 0)
        c2 = lax.broadcasted_iota(jnp.int32, (tk, tk), 1)
        ok = c2 <= r2
        if mode == "moba":
            ok = jnp.logical_or(ok, c2 // MOBA_BLOCK != r2 // MOBA_BLOCK)
        bias_scr[...] = jnp.where(ok, 0.0, NEG_INF)

    @pl.when(i == 0)
    def _():
        lane = lax.broadcasted_iota(jnp.int32, (tk, LANES), 1)
        row = lax.broadcasted_iota(jnp.int32, (tk, LANES), 0)

        def build(cidx, carry):
            st = pl.multiple_of(cidx * tk, tk)
            for g in range(group):
                kaug[g, pl.ds(st, tk), 0:HEAD_DIM] = k_ref[pl.ds(st, tk), head_cols(g)]
                vaug[g, pl.ds(st, tk), 0:HEAD_DIM] = v_ref[pl.ds(st, tk), head_cols(g)]
                vaug[g, pl.ds(st, tk), HEAD_DIM:2 * HEAD_DIM] = jnp.ones((tk, HEAD_DIM), BF16)
                if mode == "fox":
                    c = _pick_lane(ct_ref[pl.ds(st, tk), :], hg * group + g) * LOG2E
                    hi, mid, lo = _split3(c)
                    ext = jnp.where(lane < 3, 1.0, jnp.where(lane == 3, -hi, jnp.where(
                        lane == 4, -mid, jnp.where(lane == 5, -lo, 0.0))))
                else:
                    ext = jnp.where(lane == (st + row) // MOBA_BLOCK, 1.0, 0.0)
                kaug[g, pl.ds(st, tk), HEAD_DIM:2 * HEAD_DIM] = ext.astype(BF16)
            return carry

        lax.fori_loop(0, seq // tk, build, 0)
        if mode == "moba":
            kmean_scr[...] = jnp.zeros_like(kmean_scr)
            for g in range(group):
                for b in range(n_moba_blocks):
                    kb = k_ref[b * MOBA_BLOCK:(b + 1) * MOBA_BLOCK, head_cols(g)].astype(F32)
                    kmean_scr[g, b:b + 1, :] = jnp.mean(kb, axis=0, keepdims=True)
                for piece, part in enumerate(_split3(kmean_scr[g, 0:nbp, :])):
                    kmean3_scr[g, piece * nbp:(piece + 1) * nbp, :] = part.astype(BF16)

    lane = lax.broadcasted_iota(jnp.int32, (MOBA_BLOCK, LANES), 1)
    for rb in range(tq // MOBA_BLOCK):
        rows = pl.ds(rb * MOBA_BLOCK, MOBA_BLOCK)
        for g in range(group):
            q = q_ref[rows, head_cols(g)]
            if mode == "fox":
                start = pl.multiple_of(i * tq + rb * MOBA_BLOCK, MOBA_BLOCK)
                f = _pick_lane(ct_ref[pl.ds(start, MOBA_BLOCK), :], hg * group + g) * LOG2E
                hi, mid, lo = _split3(f)
                ext = jnp.where(lane == 0, hi, jnp.where(lane == 1, mid, jnp.where(
                    lane == 2, lo, jnp.where(lane < 6, 1.0, 0.0))))
            else:
                g3 = lax.dot_general(kmean3_scr[g], q, (((1,), (1,)), ((), ())),
                                     preferred_element_type=F32)
                gate = g3[0:nbp] + g3[nbp:2 * nbp] + g3[2 * nbp:3 * nbp]
                own = i * (tq // MOBA_BLOCK) + rb
                blk = lax.broadcasted_iota(jnp.int32, gate.shape, 0)
                blk_f = blk.astype(F32)
                gv = jnp.where(blk < own, gate, NEG_INF)
                keep = blk == own
                for _ in range(min(MOBA_TOPK, n_moba_blocks)):
                    m = jnp.max(gv, axis=0, keepdims=True)
                    idx = jnp.min(jnp.where(gv == m, blk_f, float(LANES)), axis=0, keepdims=True)
                    hit = blk_f == idx
                    keep = jnp.logical_or(keep, jnp.logical_and(hit, m > NEG_INF))
                    gv = jnp.where(hit, NEG_INF, gv)
                ext_t = jnp.where(keep, 0.0, NEG_BIG)
                ext_t = jnp.concatenate(
                    [ext_t, jnp.zeros((LANES - nbp, MOBA_BLOCK), F32)], axis=0)
                ext = ext_t.T
            qaug[g, rows, 0:HEAD_DIM] = q
            qaug[g, rows, HEAD_DIM:2 * HEAD_DIM] = ext.astype(BF16)

    def scores(g, j):
        st = pl.multiple_of(j * tk, tk)
        return lax.dot_general(qaug[g], kaug[g, pl.ds(st, tk), :], (((1,), (1,)), ((), ())),
                               preferred_element_type=F32)

    def update(g, s, j, rows=slice(0, tq), causal=False):
        st = pl.multiple_of(j * tk, tk)
        if causal:
            s = s + bias_scr[...]
        m_old = m_scr[g, rows]
        m_new = jnp.maximum(m_old, jnp.max(s, axis=-1, keepdims=True))
        p = jnp.exp2(s - m_new).astype(BF16)
        pv = jnp.dot(p, vaug[g, pl.ds(st, tk), :], preferred_element_type=F32)
        acc_scr[g, rows] = jnp.exp2(m_old - m_new) * acc_scr[g, rows] + pv
        m_scr[g, rows] = m_new

    heads = range(group)
    m_scr[...] = jnp.full(m_scr.shape, NEG_INF, F32)
    acc_scr[...] = jnp.zeros(acc_scr.shape, F32)
    for g in heads:
        s_a[g] = scores(g, 0)

    def body(jj, carry):
        for g in heads:
            s_b[g] = scores(g, 2 * jj + 1)
        for g in heads:
            update(g, s_a[g], 2 * jj)
        for g in heads:
            s_a[g] = scores(g, 2 * jj + 2)
        for g in heads:
            update(g, s_b[g], 2 * jj + 1)
        return carry

    lax.fori_loop(0, i, body, 0)
    upper, lower = slice(0, tk), slice(tk, tq)
    for g in heads:
        st = pl.multiple_of((2 * i + 1) * tk, tk)
        s_b[g, upper] = lax.dot_general(qaug[g, lower], kaug[g, pl.ds(st, tk), :],
                                        (((1,), (1,)), ((), ())), preferred_element_type=F32)
    for g in heads:
        update(g, s_a[g, upper], 2 * i, upper, causal=True)
        update(g, s_a[g, lower], 2 * i, lower)
    for g in heads:
        update(g, s_b[g, upper], 2 * i + 1, lower, causal=True)
    for g in heads:
        o_ref[:, head_cols(g)] = (acc_scr[g, :, 0:HEAD_DIM]
                                  / acc_scr[g, :, HEAD_DIM:2 * HEAD_DIM]).astype(o_ref.dtype)


def attention(proj, batch, seq, heads, q_col, k_col, v_col, cum_t=None, tq=1024, group=2):
    n = proj.shape[0]
    tq = _tile(seq, tq)
    tk = tq // 2
    nq = seq // tq
    mode = "moba" if cum_t is None else "fox"
    assert tk % MOBA_BLOCK == 0 and seq // MOBA_BLOCK <= LANES
    gw = group * HEAD_DIM
    assert heads % group == 0 and q_col % group == 0 and k_col % group == 0 and v_col % group == 0
    in_specs = [
        pl.BlockSpec((tq, gw), lambda b, h, i: (b * nq + i, q_col // group + h)),
        pl.BlockSpec((seq, gw), lambda b, h, i: (b, k_col // group + h)),
        pl.BlockSpec((seq, gw), lambda b, h, i: (b, v_col // group + h)),
    ]
    args = [proj, proj, proj]
    wide = 2 * HEAD_DIM
    scratch = [pltpu.VMEM((group, seq, wide), BF16), pltpu.VMEM((group, seq, wide), BF16),
               pltpu.VMEM((group, tq, wide), BF16),
               pltpu.VMEM((group, tq, tk), F32), pltpu.VMEM((group, tq, tk), F32),
               pltpu.VMEM((tk, tk), F32),
               pltpu.VMEM((group, tq, 1), F32), pltpu.VMEM((group, tq, wide), F32)]
    if mode == "fox":
        in_specs.append(pl.BlockSpec((None, seq, heads), lambda b, h, i: (b, 0, 0)))
        args.append(cum_t)
    else:
        scratch += [pltpu.VMEM((group, LANES, HEAD_DIM), F32),
                    pltpu.VMEM((group, 3 * (-(-(seq // MOBA_BLOCK) // 16) * 16), HEAD_DIM), BF16)]
    return pl.pallas_call(
        functools.partial(_attn_kernel, mode=mode, tq=tq, tk=tk, seq=seq),
        grid=(batch, heads // group, nq),
        in_specs=in_specs,
        out_specs=pl.BlockSpec((tq, gw), lambda b, h, i: (b * nq + i, h)),
        out_shape=jax.ShapeDtypeStruct((n, heads * HEAD_DIM), BF16),
        scratch_shapes=scratch,
        compiler_params=_params(("arbitrary", "arbitrary", "arbitrary")),
        name=mode + "_attention",
    )(*args)


def _out_proj_kernel(*refs, gate_row, sh_row, n_experts):
    if n_experts:
        (mo_ref, fo_ref, g1_ref, g2_ref, w_ref, x_ref, mod_ref, gf_ref, wr_ref,
         o_ref, h_ref, route_ref) = refs
    else:
        mo_ref, fo_ref, g1_ref, g2_ref, w_ref, x_ref, mod_ref, gf_ref, o_ref, h_ref = refs
    half = mo_ref.shape[1]
    m1 = _rms(mo_ref[...].astype(F32), g1_ref[...]).astype(BF16)
    m2 = _rms(fo_ref[...].astype(F32), g2_ref[...]).astype(BF16)
    acc = jnp.dot(m1, w_ref[0:half, :], preferred_element_type=F32)
    acc = acc + jnp.dot(m2, w_ref[half:2 * half, :], preferred_element_type=F32)
    x_new = x_ref[...] + mod_ref[gate_row:gate_row + 1, :] * acc
    o_ref[...] = x_new
    sh = mod_ref[sh_row:sh_row + 1, :]
    sc = mod_ref[sh_row + 1:sh_row + 2, :]
    h = _rms(x_new, gf_ref[...]) * (1.0 + sc) + sh
    h_hi = h.astype(BF16)
    h_ref[...] = h_hi
    if n_experts:
        h_lo = (h - h_hi.astype(F32)).astype(BF16)
        l_hi = jnp.dot(h_hi, wr_ref[...], preferred_element_type=F32)
        l_lo = jnp.dot(h_lo, wr_ref[:, 0:LANES], preferred_element_type=F32)
        logits = l_hi[:, 0:LANES] + l_hi[:, LANES:2 * LANES] + l_lo
        route_ref[...] = _top2_route(logits, n_experts)


def out_proj(moba_o, fox_o, g1, g2, w_o, x2, mod, ffn_g, seq, gate_row, sh_row, router_w=None, tm=512):
    n, d = x2.shape
    half = moba_o.shape[1]
    tm = _tile(seq, tm)
    per_b = seq // tm
    n_experts = 0 if router_w is None else router_w.shape[1]
    in_specs = [
        pl.BlockSpec((tm, half), lambda i: (i, 0)),
        pl.BlockSpec((tm, half), lambda i: (i, 0)),
        pl.BlockSpec((1, half), lambda i: (0, 0)),
        pl.BlockSpec((1, half), lambda i: (0, 0)),
        pl.BlockSpec((2 * half, d), lambda i: (0, 0)),
        pl.BlockSpec((tm, d), lambda i: (i, 0)),
        pl.BlockSpec((None, 6, d), lambda i: (i // per_b, 0, 0)),
        pl.BlockSpec((1, d), lambda i: (0, 0)),
    ]
    args = [moba_o, fox_o, g1.reshape(1, half), g2.reshape(1, half), w_o, x2, mod, ffn_g.reshape(1, d)]
    out_specs = [pl.BlockSpec((tm, d), lambda i: (i, 0)), pl.BlockSpec((tm, d), lambda i: (i, 0))]
    out_shape = [jax.ShapeDtypeStruct((n, d), F32), jax.ShapeDtypeStruct((n, d), BF16)]
    if n_experts:
        in_specs.append(pl.BlockSpec((d, 2 * LANES), lambda i: (0, 0)))
        w_hi = router_w.astype(BF16)
        w_lo = (router_w - w_hi.astype(F32)).astype(BF16)
        args.append(jnp.zeros((d, 2 * LANES), BF16).at[:, :n_experts].set(w_hi)
                    .at[:, LANES:LANES + n_experts].set(w_lo))
        out_specs.append(pl.BlockSpec((tm, LANES), lambda i: (i, 0)))
        out_shape.append(jax.ShapeDtypeStruct((n, LANES), F32))
    return pl.pallas_call(
        functools.partial(_out_proj_kernel, gate_row=gate_row, sh_row=sh_row, n_experts=n_experts),
        grid=(n // tm,),
        in_specs=in_specs,
        out_specs=out_specs,
        out_shape=out_shape,
        compiler_params=_params(("arbitrary",)),
        name="out_proj",
    )(*args)


def _swiglu_tile(h, wg, wu):
    a = jnp.dot(h, wg, preferred_element_type=F32)
    b = jnp.dot(h, wu, preferred_element_type=F32)
    return a * jax.nn.sigmoid(a) * b


def _ffn_kernel(h_ref, wg_ref, wu_ref, wd_ref, x_ref, mod_ref, o_ref, *, gate_row):
    f = pl.program_id(1)
    t = _swiglu_tile(h_ref[...], wg_ref[...], wu_ref[...]).astype(BF16)

    @pl.when(f == 0)
    def _():
        o_ref[...] = jnp.dot(t, wd_ref[...], preferred_element_type=F32)

    @pl.when(f > 0)
    def _():
        o_ref[...] += jnp.dot(t, wd_ref[...], preferred_element_type=F32)

    @pl.when(f == pl.num_programs(1) - 1)
    def _():
        o_ref[...] = x_ref[...] + mod_ref[gate_row:gate_row + 1, :] * o_ref[...]


def ffn_dense(h, wg, wu, wd, x2, mod, seq, gate_row, tm=1024, tf=512):
    n, d = x2.shape
    ff = wg.shape[1]
    tm = _tile(seq, tm)
    tf = _tile(ff, tf)
    per_b = seq // tm
    return pl.pallas_call(
        functools.partial(_ffn_kernel, gate_row=gate_row),
        grid=(n // tm, ff // tf),
        in_specs=[
            pl.BlockSpec((tm, d), lambda i, f: (i, 0)),
            pl.BlockSpec((d, tf), lambda i, f: (0, f)),
            pl.BlockSpec((d, tf), lambda i, f: (0, f)),
            pl.BlockSpec((tf, d), lambda i, f: (f, 0)),
            pl.BlockSpec((tm, d), lambda i, f: (i, 0), pipeline_mode=pl.Buffered(1)),
            pl.BlockSpec((None, 6, d), lambda i, f: (i // per_b, 0, 0)),
        ],
        out_specs=pl.BlockSpec((tm, d), lambda i, f: (i, 0)),
        out_shape=jax.ShapeDtypeStruct((n, d), F32),
        compiler_params=_params(("arbitrary", "arbitrary")),
        name="ffn_dense",
    )(h, wg, wu, wd, x2, mod)


def _route_plan(route, n_experts, tm):
    n = route.shape[0]
    pairs = TOP_K * n
    e_flat = route[:, :TOP_K].astype(jnp.int32).reshape(pairs)
    onehot = (e_flat[:, None] == jnp.arange(n_experts, dtype=jnp.int32)[None, :]).astype(jnp.int32)
    csum = jnp.cumsum(onehot, axis=0)
    counts = csum[-1]
    padded = (counts + tm - 1) // tm * tm
    pend = jnp.cumsum(padded)
    pstart = pend - padded
    rank = jnp.sum(csum * onehot, axis=1) - 1
    pos = (jnp.sum(pstart[None, :] * onehot, axis=1) + rank).astype(jnp.int32)

    n_tiles = pairs // tm + n_experts
    first_row = jnp.arange(n_tiles, dtype=jnp.int32) * tm
    tile_expert = jnp.minimum(jnp.sum((pend[None, :] <= first_row[:, None]).astype(jnp.int32), axis=1),
                              n_experts - 1)
    valid = first_row < pend[-1]
    return pos, (tile_expert.astype(jnp.int32), valid.astype(jnp.int32),
                 (pstart + counts).astype(jnp.int32))


def _dispatch_kernel(pos_ref, h_ref, init_ref, xs_ref, sem):
    del init_ref
    i = pl.program_id(0)
    tm = h_ref.shape[0]

    def issue(r, carry):
        for k in range(TOP_K):
            dst = pos_ref[(i * tm + r) * TOP_K + k]
            pltpu.make_async_copy(h_ref.at[r], xs_ref.at[dst], sem).start()
        return carry

    lax.fori_loop(0, tm, issue, 0)
    for k in range(TOP_K):
        pltpu.make_async_copy(h_ref, xs_ref.at[pl.ds(0, tm)], sem).wait()


def moe_dispatch(h3, pos, rows, tm=512):
    n, sub, lanes = h3.shape
    tm = _tile(n, tm)
    return pl.pallas_call(
        _dispatch_kernel,
        grid_spec=pltpu.PrefetchScalarGridSpec(
            num_scalar_prefetch=1,
            grid=(n // tm,),
            in_specs=[pl.BlockSpec((tm, sub, lanes), lambda i, pos: (i, 0, 0)),
                      pl.BlockSpec(memory_space=pl.ANY)],
            out_specs=pl.BlockSpec(memory_space=pl.ANY),
            scratch_shapes=[pltpu.SemaphoreType.DMA(())],
        ),
        out_shape=jax.ShapeDtypeStruct((rows, sub, lanes), h3.dtype),
        input_output_aliases={2: 0},
        compiler_params=_params(("arbitrary",), disable_bounds_checks=True),
        name="moe_dispatch",
    )(pos, h3, jnp.zeros((rows, sub, lanes), h3.dtype))


def _moe_ffn_kernel(exp_ref, valid_ref, rowend_ref, xs_ref, wg_ref, wu_ref, wd_ref, o_ref):
    t = pl.program_id(0)
    f = pl.program_id(1)
    tm = xs_ref.shape[0]

    @pl.when(valid_ref[t] == 1)
    def _():
        rows = t * tm + lax.broadcasted_iota(jnp.int32, (tm, 1), 0)
        real = rows < rowend_ref[exp_ref[t]]
        a = _swiglu_tile(xs_ref[...], wg_ref[...].astype(BF16), wu_ref[...].astype(BF16))
        a = jnp.where(real, a, 0.0).astype(BF16)
        wd = wd_ref[...].astype(BF16)

        @pl.when(f == 0)
        def _():
            o_ref[...] = jnp.dot(a, wd, preferred_element_type=F32)

        @pl.when(f > 0)
        def _():
            o_ref[...] += jnp.dot(a, wd, preferred_element_type=F32)

    @pl.when(jnp.logical_and(valid_ref[t] == 0, f == 0))
    def _():
        o_ref[...] = jnp.zeros(o_ref.shape, o_ref.dtype)


def moe_ffn(xs, plan, wg, wu, wd, tm, tf=256):
    rows, d = xs.shape
    n_experts, _, ff = wg.shape
    tf = _tile(ff, tf)
    nf = ff // tf

    def f_idx(t, f, va):
        return jnp.where(va[t] == 1, f, nf - 1)

    return pl.pallas_call(
        _moe_ffn_kernel,
        grid_spec=pltpu.PrefetchScalarGridSpec(
            num_scalar_prefetch=3,
            grid=(rows // tm, nf),
            in_specs=[
                pl.BlockSpec((tm, d), lambda t, f, ex, va, re: (t, 0)),
                pl.BlockSpec((None, d, tf), lambda t, f, ex, va, re: (ex[t], 0, f_idx(t, f, va))),
                pl.BlockSpec((None, d, tf), lambda t, f, ex, va, re: (ex[t], 0, f_idx(t, f, va))),
                pl.BlockSpec((None, tf, d), lambda t, f, ex, va, re: (ex[t], f_idx(t, f, va), 0)),
            ],
            out_specs=pl.BlockSpec((tm, d), lambda t, f, ex, va, re: (t, 0)),
        ),
        out_shape=jax.ShapeDtypeStruct((rows, d), F32),
        compiler_params=_params(("arbitrary", "arbitrary")),
        name="moe_ffn",
    )(*plan, xs, wg, wu, wd)


def _combine_kernel(pos_ref, ys_ref, route_ref, x_ref, mod_ref, g_ref, o_ref, buf, sem,
                    *, gate_row, final):
    i = pl.program_id(0)
    tm = x_ref.shape[0]
    slot = i % 2

    def gather(step, into):
        def issue(r, carry):
            for k in range(TOP_K):
                src = pos_ref[(step * tm + r) * TOP_K + k]
                pltpu.make_async_copy(ys_ref.at[pl.ds(src, 1), :], buf.at[into, k, pl.ds(r, 1), :],
                                      sem.at[into]).start()
            return carry

        lax.fori_loop(0, tm, issue, 0)

    @pl.when(i == 0)
    def _():
        gather(0, 0)

    @pl.when(i + 1 < pl.num_programs(0))
    def _():
        gather(i + 1, 1 - slot)

    for k in range(TOP_K):
        pltpu.make_async_copy(ys_ref.at[pl.ds(0, tm), :], buf.at[slot, k], sem.at[slot]).wait()
    route = route_ref[...]
    y = buf[slot, 0] * route[:, TOP_K:TOP_K + 1]
    for k in range(1, TOP_K):
        y = y + buf[slot, k] * route[:, TOP_K + k:TOP_K + k + 1]
    x_new = x_ref[...] + mod_ref[gate_row:gate_row + 1, :] * y
    o_ref[...] = _rms(x_new, g_ref[...]) if final else x_new


def moe_combine(ys, pos, route, x2, mod, seq, gate_row, final_g=None, tm=256):
    n, d = x2.shape
    tm = _tile(seq, tm)
    per_b = seq // tm
    g = jnp.ones((1, d), F32) if final_g is None else final_g.reshape(1, d)
    return pl.pallas_call(
        functools.partial(_combine_kernel, gate_row=gate_row, final=final_g is not None),
        grid_spec=pltpu.PrefetchScalarGridSpec(
            num_scalar_prefetch=1,
            grid=(n // tm,),
            in_specs=[
                pl.BlockSpec(memory_space=pl.ANY),
                pl.BlockSpec((tm, LANES), lambda i, pos: (i, 0)),
                pl.BlockSpec((tm, d), lambda i, pos: (i, 0)),
                pl.BlockSpec((None, 6, d), lambda i, pos: (i // per_b, 0, 0)),
                pl.BlockSpec((1, d), lambda i, pos: (0, 0)),
            ],
            out_specs=pl.BlockSpec((tm, d), lambda i, pos: (i, 0)),
            scratch_shapes=[pltpu.VMEM((2, TOP_K, tm, d), F32), pltpu.SemaphoreType.DMA((2,))],
        ),
        out_shape=jax.ShapeDtypeStruct((n, d), F32),
        compiler_params=_params(("arbitrary",), disable_bounds_checks=True),
        name="moe_combine",
    )(pos, ys, route, x2, mod, g)


def ffn_moe(h, route, wg, wu, wd, x2, mod, seq, gate_row, final_g=None, tm=1024):
    n, d = x2.shape
    n_experts = wg.shape[0]
    tm = _tile(TOP_K * n, tm)
    pos, plan = _route_plan(route, n_experts, tm)
    rows = TOP_K * n + n_experts * tm
    xs3 = moe_dispatch(h.reshape(n, d // LANES, LANES), pos, rows)
    ys = moe_ffn(xs3.reshape(rows, d), plan, wg, wu, wd, tm)
    return moe_combine(ys, pos, route, x2, mod, seq, gate_row, final_g)


def _rope_tables(seq):
    inv = ROPE_THETA ** (-jnp.arange(0, HEAD_DIM, 2, dtype=F32) / HEAD_DIM)
    ang = jnp.arange(seq, dtype=F32)[:, None] * inv[None, :]
    cos, sin = jnp.cos(ang), jnp.sin(ang)
    return jnp.concatenate([cos, cos], axis=-1), jnp.concatenate([-sin, sin], axis=-1)


def kernel(x, c, attn_norm_g, ada_w, ada_b, w_in, forget_b, moba_out_g, fox_out_g, w_o, ffn_norm_g,
           dense_w_gate, dense_w_up, dense_w_down, router_w, moe_w_gate, moe_w_up, moe_w_down,
           final_norm_g):
    batch, seq, d = x.shape
    depth = attn_norm_g.shape[0]
    moba_w = moba_out_g.shape[1]
    fox_w = fox_out_g.shape[1]
    moba_h = moba_w // HEAD_DIM
    fox_h = fox_w // HEAD_DIM
    qkv_w = 3 * moba_w + 3 * fox_w
    assert moba_w == fox_w
    kinds = (ROPE_SCALED, ROPE, PLAIN, SCALED, PLAIN, PLAIN)
    hb = HEAD_DIM

    cos_t, sin_t = _rope_tables(seq)
    mods = ada_modulation(c, ada_w, ada_b)
    x2 = x.reshape(batch * seq, d)
    out = None

    for l in range(depth):
        mod = mods[l]
        h = norm_mod(x2, attn_norm_g[l], mod, seq, sh_row=0)
        w_qkv = w_in[l][:, :qkv_w].astype(BF16)
        wz = jnp.zeros((d, LANES), BF16).at[:, :fox_h].set(w_in[l][:, qkv_w:].astype(BF16))
        proj, fz = in_proj(h, w_qkv, wz, cos_t, sin_t, seq, kinds, tn=moba_w)
        cum_t = logf_cumsum(fz[:, :fox_h].reshape(batch, seq, fox_h), forget_b[l])
        moba_o = attention(proj, batch, seq, moba_h, q_col=0, k_col=moba_w // hb,
                           v_col=2 * moba_w // hb)
        fc = 3 * moba_w // hb
        fox_o = attention(proj, batch, seq, fox_h, q_col=fc, k_col=fc + fox_w // hb,
                          v_col=fc + 2 * fox_w // hb, cum_t=cum_t)
        i = l // 2
        last = l == depth - 1
        w_o_l = w_o[l].astype(BF16)
        if l % 2 == 0:
            x2, h = out_proj(moba_o, fox_o, moba_out_g[l], fox_out_g[l], w_o_l, x2, mod,
                             ffn_norm_g[l], seq, gate_row=2, sh_row=3)
            x2 = ffn_dense(h, dense_w_gate[i].astype(BF16), dense_w_up[i].astype(BF16),
                           dense_w_down[i].astype(BF16), x2, mod, seq, gate_row=5)
        else:
            x2, h, route = out_proj(moba_o, fox_o, moba_out_g[l], fox_out_g[l], w_o_l, x2, mod,
                                    ffn_norm_g[l], seq, gate_row=2, sh_row=3, router_w=router_w[i])
            x2 = ffn_moe(h, route, moe_w_gate[i], moe_w_up[i], moe_w_down[i], x2, mod, seq,
                         gate_row=5, final_g=final_norm_g if last else None)
            if last:
                out = x2
    if out is None:
        out = final_norm(x2, final_norm_g)
    return out.reshape(batch, seq, d)
```

```python
import functools
import math

import jax
import jax.numpy as jnp
from jax import lax
from jax.experimental import pallas as pl
from jax.experimental.pallas import tpu as pltpu

F32 = jnp.float32
BF16 = jnp.bfloat16

HEAD_DIM = 128
MOBA_BLOCK = 256
MOBA_TOPK = 3
ROPE_THETA = 10000.0
EPS = 1e-6
TOP_K = 2
LANES = 128
VMEM_LIMIT = 56 * 1024 * 1024

NEG_INF = float("-inf")
NEG_BIG = -(2.0 ** 100)
LOG2E = math.log2(math.e)
Q_SCALE = HEAD_DIM ** -0.5 * LOG2E


def _params(semantics, **kw):
    return pltpu.CompilerParams(dimension_semantics=semantics, vmem_limit_bytes=VMEM_LIMIT, **kw)


def _tile(n, want):
    t = min(n, want)
    assert n % t == 0, (n, want)
    return t


def _ada_kernel(c_ref, w_ref, b_ref, o_ref):
    c = c_ref[...]
    c_act = c * jax.nn.sigmoid(c)
    o_ref[...] = jnp.dot(c_act, w_ref[...], preferred_element_type=F32,
                         precision=lax.Precision.HIGHEST) + b_ref[...]


def ada_modulation(c, ada_w, ada_b):
    n_layers, d, d6 = ada_w.shape
    b = c.shape[0]
    rows = -(-b // 8) * 8
    c_pad = jnp.zeros((rows, d), F32).at[:b].set(c)
    tn = _tile(d6, 1024)
    out = pl.pallas_call(
        _ada_kernel,
        grid=(n_layers, d6 // tn),
        in_specs=[
            pl.BlockSpec((rows, d), lambda l, j: (0, 0)),
            pl.BlockSpec((None, d, tn), lambda l, j: (l, 0, j)),
            pl.BlockSpec((None, 1, tn), lambda l, j: (l, 0, j)),
        ],
        out_specs=pl.BlockSpec((None, rows, tn), lambda l, j: (l, 0, j)),
        out_shape=jax.ShapeDtypeStruct((n_layers, rows, d6), F32),
        compiler_params=_params(("arbitrary", "arbitrary")),
        name="ada_modulation",
    )(c_pad, ada_w, ada_b.reshape(n_layers, 1, d6))
    return out[:, :b].reshape(n_layers, b, 6, d)


def _rms(x, g):
    return x * lax.rsqrt(jnp.mean(x * x, axis=-1, keepdims=True) + EPS) * g


def _norm_mod_kernel(x_ref, g_ref, mod_ref, o_ref, *, sh_row):
    y = _rms(x_ref[...], g_ref[...])
    sh = mod_ref[sh_row:sh_row + 1, :]
    sc = mod_ref[sh_row + 1:sh_row + 2, :]
    o_ref[...] = (y * (1.0 + sc) + sh).astype(o_ref.dtype)


def _top2_route(logits, n_experts):
    lane = lax.broadcasted_iota(jnp.int32, logits.shape, 1).astype(F32)
    lg = jnp.where(lane < n_experts, logits, NEG_INF)
    m1 = jnp.max(lg, axis=-1, keepdims=True)
    i1 = jnp.min(jnp.where(lg == m1, lane, float(LANES)), axis=-1, keepdims=True)
    lg2 = jnp.where(lane == i1, NEG_INF, lg)
    m2 = jnp.max(lg2, axis=-1, keepdims=True)
    i2 = jnp.min(jnp.where(lg2 == m2, lane, float(LANES)), axis=-1, keepdims=True)
    e2 = jnp.exp(m2 - m1)
    w1 = 1.0 / (1.0 + e2)
    w2 = e2 / (1.0 + e2)
    return jnp.where(lane == 0.0, i1, jnp.where(lane == 1.0, i2,
                     jnp.where(lane == 2.0, w1, jnp.where(lane == 3.0, w2, 0.0))))


def norm_mod(x2, g, mod, seq, sh_row, tm=512):
    n, d = x2.shape
    tm = _tile(seq, tm)
    per_b = seq // tm
    return pl.pallas_call(
        functools.partial(_norm_mod_kernel, sh_row=sh_row),
        grid=(n // tm,),
        in_specs=[
            pl.BlockSpec((tm, d), lambda i: (i, 0)),
            pl.BlockSpec((1, d), lambda i: (0, 0)),
            pl.BlockSpec((None, 6, d), lambda i: (i // per_b, 0, 0)),
        ],
        out_specs=pl.BlockSpec((tm, d), lambda i: (i, 0)),
        out_shape=jax.ShapeDtypeStruct((n, d), BF16),
        compiler_params=_params(("arbitrary",)),
        name="norm_mod",
    )(x2, g.reshape(1, d), mod)


def _final_norm_kernel(x_ref, g_ref, o_ref):
    o_ref[...] = _rms(x_ref[...], g_ref[...])


def final_norm(x2, g, tm=512):
    n, d = x2.shape
    tm = _tile(n, tm)
    return pl.pallas_call(
        _final_norm_kernel,
        grid=(n // tm,),
        in_specs=[pl.BlockSpec((tm, d), lambda i: (i, 0)),
                  pl.BlockSpec((1, d), lambda i: (0, 0))],
        out_specs=pl.BlockSpec((tm, d), lambda i: (i, 0)),
        out_shape=jax.ShapeDtypeStruct((n, d), F32),
        compiler_params=_params(("arbitrary",)),
        name="final_norm",
    )(x2, g.reshape(1, d))


PLAIN, ROPE, ROPE_SCALED, SCALED = 0, 1, 2, 3


def _in_proj_kernel(h_ref, w_ref, wz_ref, cos_ref, sin_ref, o_ref, fz_ref, *, kinds):
    j = pl.program_id(1)
    tn = w_ref.shape[1]

    def is_kind(kind):
        hits = [jj for jj, k in enumerate(kinds) if k == kind]
        cond = j == hits[0]
        for jj in hits[1:]:
            cond = jnp.logical_or(cond, j == jj)
        return cond

    for kind in sorted(set(kinds)):
        @pl.when(is_kind(kind))
        def _(kind=kind):
            acc = jnp.dot(h_ref[...], w_ref[...], preferred_element_type=F32)
            mult = Q_SCALE if kind in (ROPE_SCALED, SCALED) else 1.0
            if kind in (ROPE, ROPE_SCALED):
                c = cos_ref[...] * mult
                s = sin_ref[...] * mult
                for hh in range(tn // HEAD_DIM):
                    t = acc[:, hh * HEAD_DIM:(hh + 1) * HEAD_DIM]
                    rot = pltpu.roll(t, HEAD_DIM // 2, 1)
                    o_ref[:, hh * HEAD_DIM:(hh + 1) * HEAD_DIM] = (t * c + rot * s).astype(o_ref.dtype)
            elif kind == SCALED:
                o_ref[...] = (acc * mult).astype(o_ref.dtype)
            else:
                o_ref[...] = acc.astype(o_ref.dtype)

    @pl.when(j == 0)
    def _():
        fz_ref[...] = jnp.dot(h_ref[...], wz_ref[...], preferred_element_type=F32)


def in_proj(h, w_qkv, wz, cos_t, sin_t, seq, kinds, tn, tm=1024):
    n, d = h.shape
    width = w_qkv.shape[1]
    assert width == tn * len(kinds)
    tm = _tile(seq, tm)
    per_b = seq // tm
    return pl.pallas_call(
        functools.partial(_in_proj_kernel, kinds=tuple(kinds)),
        grid=(n // tm, width // tn),
        in_specs=[
            pl.BlockSpec((tm, d), lambda i, j: (i, 0)),
            pl.BlockSpec((d, tn), lambda i, j: (0, j)),
            pl.BlockSpec((d, LANES), lambda i, j: (0, 0)),
            pl.BlockSpec((tm, HEAD_DIM), lambda i, j: (i % per_b, 0)),
            pl.BlockSpec((tm, HEAD_DIM), lambda i, j: (i % per_b, 0)),
        ],
        out_specs=[pl.BlockSpec((tm, tn), lambda i, j: (i, j)),
                   pl.BlockSpec((tm, LANES), lambda i, j: (i, 0))],
        out_shape=[jax.ShapeDtypeStruct((n, width), BF16),
                   jax.ShapeDtypeStruct((n, LANES), F32)],
        compiler_params=_params(("arbitrary", "arbitrary")),
        name="in_proj",
    )(h, w_qkv, wz, cos_t, sin_t)


def _logf_cumsum_kernel(z_ref, b_ref, o_ref):
    z = z_ref[...] + b_ref[...]
    lf = jnp.minimum(z, 0.0) - jnp.log1p(jnp.exp(-jnp.abs(z)))
    r = lax.broadcasted_iota(jnp.int32, (LANES, LANES), 0)
    c = lax.broadcasted_iota(jnp.int32, (LANES, LANES), 1)
    tri = jnp.where(c <= r, 1.0, 0.0).astype(F32)
    carry = jnp.zeros((1, z.shape[1]), F32)
    for ch in range(z.shape[0] // LANES):
        seg = lf[ch * LANES:(ch + 1) * LANES, :]
        cs = jnp.dot(tri, seg, preferred_element_type=F32, precision=lax.Precision.HIGHEST) + carry
        o_ref[ch * LANES:(ch + 1) * LANES, :] = cs
        carry = cs[LANES - 1:LANES, :]


def logf_cumsum(z, forget_b):
    b, s, h = z.shape
    return pl.pallas_call(
        _logf_cumsum_kernel,
        grid=(b,),
        in_specs=[pl.BlockSpec((None, s, h), lambda i: (i, 0, 0)),
                  pl.BlockSpec((1, h), lambda i: (0, 0))],
        out_specs=pl.BlockSpec((None, s, h), lambda i: (i, 0, 0)),
        out_shape=jax.ShapeDtypeStruct((b, s, h), F32),
        compiler_params=_params(("arbitrary",)),
        name="logf_cumsum",
    )(z, forget_b.reshape(1, h))


def _split3(f):
    hi = f.astype(BF16).astype(F32)
    r = f - hi
    mid = r.astype(BF16).astype(F32)
    lo = (r - mid).astype(BF16).astype(F32)
    return hi, mid, lo


def _pick_lane(x, idx):
    lane = lax.broadcasted_iota(jnp.int32, x.shape, 1)
    return jnp.sum(jnp.where(lane == idx, x, 0.0), axis=-1, keepdims=True)


def _attn_kernel(*refs, mode, tq, tk, seq):
    if mode == "fox":
        q_ref, k_ref, v_ref, ct_ref, o_ref, kaug, vaug, qaug, s_a, s_b, bias_scr, m_scr, acc_scr = refs
    else:
        (q_ref, k_ref, v_ref, o_ref, kaug, vaug, qaug, s_a, s_b, bias_scr, m_scr, acc_scr,
         kmean_scr, kmean3_scr) = refs
    hg = pl.program_id(1)
    i = pl.program_id(2)
    n_moba_blocks = seq // MOBA_BLOCK
    nbp = -(-n_moba_blocks // 16) * 16
    group = q_ref.shape[1] // HEAD_DIM

    def head_cols(g):
        return slice(g * HEAD_DIM, (g + 1) * HEAD_DIM)

    @pl.when(jnp.logical_and(jnp.logical_and(pl.program_id(0) == 0, hg == 0), i == 0))
    def _():
        r2 = lax.broadcasted_iota(jnp.int32, (tk, tk), 0)
        c2 = lax.broadcasted_iota(jnp.int32, (tk, tk), 1)
        ok = c2 <= r2
        if mode == "moba":
            ok = jnp.logical_or(ok, c2 // MOBA_BLOCK != r2 // MOBA_BLOCK)
        bias_scr[...] = jnp.where(ok, 0.0, NEG_INF)

    @pl.when(i == 0)
    def _():
        lane = lax.broadcasted_iota(jnp.int32, (tk, LANES), 1)
        row = lax.broadcasted_iota(jnp.int32, (tk, LANES), 0)

        def build(cidx, carry):
            st = pl.multiple_of(cidx * tk, tk)
            for g in range(group):
                kaug[g, pl.ds(st, tk), 0:HEAD_DIM] = k_ref[pl.ds(st, tk), head_cols(g)]
                vaug[g, pl.ds(st, tk), 0:HEAD_DIM] = v_ref[pl.ds(st, tk), head_cols(g)]
                vaug[g, pl.ds(st, tk), HEAD_DIM:2 * HEAD_DIM] = jnp.ones((tk, HEAD_DIM), BF16)
                if mode == "fox":
                    c = _pick_lane(ct_ref[pl.ds(st, tk), :], hg * group + g) * LOG2E
                    hi, mid, lo = _split3(c)
                    ext = jnp.where(lane < 3, 1.0, jnp.where(lane == 3, -hi, jnp.where(
                        lane == 4, -mid, jnp.where(lane == 5, -lo, 0.0))))
                else:
                    ext = jnp.where(lane == (st + row) // MOBA_BLOCK, 1.0, 0.0)
                kaug[g, pl.ds(st, tk), HEAD_DIM:2 * HEAD_DIM] = ext.astype(BF16)
            return carry

        lax.fori_loop(0, seq // tk, build, 0)
        if mode == "moba":
            kmean_scr[...] = jnp.zeros_like(kmean_scr)
            for g in range(group):
                for b in range(n_moba_blocks):
                    kb = k_ref[b * MOBA_BLOCK:(b + 1) * MOBA_BLOCK, head_cols(g)].astype(F32)
                    kmean_scr[g, b:b + 1, :] = jnp.mean(kb, axis=0, keepdims=True)
                for piece, part in enumerate(_split3(kmean_scr[g, 0:nbp, :])):
                    kmean3_scr[g, piece * nbp:(piece + 1) * nbp, :] = part.astype(BF16)

    lane = lax.broadcasted_iota(jnp.int32, (MOBA_BLOCK, LANES), 1)
    for rb in range(tq // MOBA_BLOCK):
        rows = pl.ds(rb * MOBA_BLOCK, MOBA_BLOCK)
        for g in range(group):
            q = q_ref[rows, head_cols(g)]
            if mode == "fox":
                start = pl.multiple_of(i * tq + rb * MOBA_BLOCK, MOBA_BLOCK)
                f = _pick_lane(ct_ref[pl.ds(start, MOBA_BLOCK), :], hg * group + g) * LOG2E
                hi, mid, lo = _split3(f)
                ext = jnp.where(lane == 0, hi, jnp.where(lane == 1, mid, jnp.where(
                    lane == 2, lo, jnp.where(lane < 6, 1.0, 0.0))))
            else:
                g3 = lax.dot_general(kmean3_scr[g], q, (((1,), (1,)), ((), ())),
                                     preferred_element_type=F32)
                gate = g3[0:nbp] + g3[nbp:2 * nbp] + g3[2 * nbp:3 * nbp]
                own = i * (tq // MOBA_BLOCK) + rb
                blk = lax.broadcasted_iota(jnp.int32, gate.shape, 0)
                blk_f = blk.astype(F32)
                gv = jnp.where(blk < own, gate, NEG_INF)
                keep = blk == own
                for _ in range(min(MOBA_TOPK, n_moba_blocks)):
                    m = jnp.max(gv, axis=0, keepdims=True)
                    idx = jnp.min(jnp.where(gv == m, blk_f, float(LANES)), axis=0, keepdims=True)
                    hit = blk_f == idx
                    keep = jnp.logical_or(keep, jnp.logical_and(hit, m > NEG_INF))
                    gv = jnp.where(hit, NEG_INF, gv)
                ext_t = jnp.where(keep, 0.0, NEG_BIG)
                ext_t = jnp.concatenate(
                    [ext_t, jnp.zeros((LANES - nbp, MOBA_BLOCK), F32)], axis=0)
                ext = ext_t.T
            qaug[g, rows, 0:HEAD_DIM] = q
            qaug[g, rows, HEAD_DIM:2 * HEAD_DIM] = ext.astype(BF16)

    def scores(g, j):
        st = pl.multiple_of(j * tk, tk)
        return lax.dot_general(qaug[g], kaug[g, pl.ds(st, tk), :], (((1,), (1,)), ((), ())),
                               preferred_element_type=F32)

    def update(g, s, j, rows=slice(0, tq), causal=False):
        st = pl.multiple_of(j * tk, tk)
        if causal:
            s = s + bias_scr[...]
        m_old = m_scr[g, rows]
        m_new = jnp.maximum(m_old, jnp.max(s, axis=-1, keepdims=True))
        p = jnp.exp2(s - m_new).astype(BF16)
        pv = jnp.dot(p, vaug[g, pl.ds(st, tk), :], preferred_element_type=F32)
        acc_scr[g, rows] = jnp.exp2(m_old - m_new) * acc_scr[g, rows] + pv
        m_scr[g, rows] = m_new

    heads = range(group)
    m_scr[...] = jnp.full(m_scr.shape, NEG_INF, F32)
    acc_scr[...] = jnp.zeros(acc_scr.shape, F32)
    for g in heads:
        s_a[g] = scores(g, 0)

    def body(jj, carry):
        for g in heads:
            s_b[g] = scores(g, 2 * jj + 1)
        for g in heads:
            update(g, s_a[g], 2 * jj)
        for g in heads:
            s_a[g] = scores(g, 2 * jj + 2)
        for g in heads:
            update(g, s_b[g], 2 * jj + 1)
        return carry

    lax.fori_loop(0, i, body, 0)
    upper, lower = slice(0, tk), slice(tk, tq)
    for g in heads:
        st = pl.multiple_of((2 * i + 1) * tk, tk)
        s_b[g, upper] = lax.dot_general(qaug[g, lower], kaug[g, pl.ds(st, tk), :],
                                        (((1,), (1,)), ((), ())), preferred_element_type=F32)
    for g in heads:
        update(g, s_a[g, upper], 2 * i, upper, causal=True)
        update(g, s_a[g, lower], 2 * i, lower)
    for g in heads:
        update(g, s_b[g, upper], 2 * i + 1, lower, causal=True)
    for g in heads:
        o_ref[:, head_cols(g)] = (acc_scr[g, :, 0:HEAD_DIM]
                                  / acc_scr[g, :, HEAD_DIM:2 * HEAD_DIM]).astype(o_ref.dtype)


def attention(proj, batch, seq, heads, q_col, k_col, v_col, cum_t=None, tq=1024, group=2):
    n = proj.shape[0]
    tq = _tile(seq, tq)
    tk = tq // 2
    nq = seq // tq
    mode = "moba" if cum_t is None else "fox"
    assert tk % MOBA_BLOCK == 0 and seq // MOBA_BLOCK <= LANES
    gw = group * HEAD_DIM
    assert heads % group == 0 and q_col % group == 0 and k_col % group == 0 and v_col % group == 0
    in_specs = [
        pl.BlockSpec((tq, gw), lambda b, h, i: (b * nq + i, q_col // group + h)),
        pl.BlockSpec((seq, gw), lambda b, h, i: (b, k_col // group + h)),
        pl.BlockSpec((seq, gw), lambda b, h, i: (b, v_col // group + h)),
    ]
    args = [proj, proj, proj]
    wide = 2 * HEAD_DIM
    scratch = [pltpu.VMEM((group, seq, wide), BF16), pltpu.VMEM((group, seq, wide), BF16),
               pltpu.VMEM((group, tq, wide), BF16),
               pltpu.VMEM((group, tq, tk), F32), pltpu.VMEM((group, tq, tk), F32),
               pltpu.VMEM((tk, tk), F32),
               pltpu.VMEM((group, tq, 1), F32), pltpu.VMEM((group, tq, wide), F32)]
    if mode == "fox":
        in_specs.append(pl.BlockSpec((None, seq, heads), lambda b, h, i: (b, 0, 0)))
        args.append(cum_t)
    else:
        scratch += [pltpu.VMEM((group, LANES, HEAD_DIM), F32),
                    pltpu.VMEM((group, 3 * (-(-(seq // MOBA_BLOCK) // 16) * 16), HEAD_DIM), BF16)]
    return pl.pallas_call(
        functools.partial(_attn_kernel, mode=mode, tq=tq, tk=tk, seq=seq),
        grid=(batch, heads // group, nq),
        in_specs=in_specs,
        out_specs=pl.BlockSpec((tq, gw), lambda b, h, i: (b * nq + i, h)),
        out_shape=jax.ShapeDtypeStruct((n, heads * HEAD_DIM), BF16),
        scratch_shapes=scratch,
        compiler_params=_params(("arbitrary", "arbitrary", "arbitrary")),
        name=mode + "_attention",
    )(*args)


def _out_proj_kernel(*refs, gate_row, sh_row, n_experts):
    if n_experts:
        (mo_ref, fo_ref, g1_ref, g2_ref, w_ref, x_ref, mod_ref, gf_ref, wr_ref,
         o_ref, h_ref, route_ref) = refs
    else:
        mo_ref, fo_ref, g1_ref, g2_ref, w_ref, x_ref, mod_ref, gf_ref, o_ref, h_ref = refs
    half = mo_ref.shape[1]
    m1 = _rms(mo_ref[...].astype(F32), g1_ref[...]).astype(BF16)
    m2 = _rms(fo_ref[...].astype(F32), g2_ref[...]).astype(BF16)
    acc = jnp.dot(m1, w_ref[0:half, :], preferred_element_type=F32)
    acc = acc + jnp.dot(m2, w_ref[half:2 * half, :], preferred_element_type=F32)
    x_new = x_ref[...] + mod_ref[gate_row:gate_row + 1, :] * acc
    o_ref[...] = x_new
    sh = mod_ref[sh_row:sh_row + 1, :]
    sc = mod_ref[sh_row + 1:sh_row + 2, :]
    h = _rms(x_new, gf_ref[...]) * (1.0 + sc) + sh
    h_hi = h.astype(BF16)
    h_ref[...] = h_hi
    if n_experts:
        h_lo = (h - h_hi.astype(F32)).astype(BF16)
        l_hi = jnp.dot(h_hi, wr_ref[...], preferred_element_type=F32)
        l_lo = jnp.dot(h_lo, wr_ref[:, 0:LANES], preferred_element_type=F32)
        logits = l_hi[:, 0:LANES] + l_hi[:, LANES:2 * LANES] + l_lo
        route_ref[...] = _top2_route(logits, n_experts)


def out_proj(moba_o, fox_o, g1, g2, w_o, x2, mod, ffn_g, seq, gate_row, sh_row, router_w=None, tm=512):
    n, d = x2.shape
    half = moba_o.shape[1]
    tm = _tile(seq, tm)
    per_b = seq // tm
    n_experts = 0 if router_w is None else router_w.shape[1]
    in_specs = [
        pl.BlockSpec((tm, half), lambda i: (i, 0)),
        pl.BlockSpec((tm, half), lambda i: (i, 0)),
        pl.BlockSpec((1, half), lambda i: (0, 0)),
        pl.BlockSpec((1, half), lambda i: (0, 0)),
        pl.BlockSpec((2 * half, d), lambda i: (0, 0)),
        pl.BlockSpec((tm, d), lambda i: (i, 0)),
        pl.BlockSpec((None, 6, d), lambda i: (i // per_b, 0, 0)),
        pl.BlockSpec((1, d), lambda i: (0, 0)),
    ]
    args = [moba_o, fox_o, g1.reshape(1, half), g2.reshape(1, half), w_o, x2, mod, ffn_g.reshape(1, d)]
    out_specs = [pl.BlockSpec((tm, d), lambda i: (i, 0)), pl.BlockSpec((tm, d), lambda i: (i, 0))]
    out_shape = [jax.ShapeDtypeStruct((n, d), F32), jax.ShapeDtypeStruct((n, d), BF16)]
    if n_experts:
        in_specs.append(pl.BlockSpec((d, 2 * LANES), lambda i: (0, 0)))
        w_hi = router_w.astype(BF16)
        w_lo = (router_w - w_hi.astype(F32)).astype(BF16)
        args.append(jnp.zeros((d, 2 * LANES), BF16).at[:, :n_experts].set(w_hi)
                    .at[:, LANES:LANES + n_experts].set(w_lo))
        out_specs.append(pl.BlockSpec((tm, LANES), lambda i: (i, 0)))
        out_shape.append(jax.ShapeDtypeStruct((n, LANES), F32))
    return pl.pallas_call(
        functools.partial(_out_proj_kernel, gate_row=gate_row, sh_row=sh_row, n_experts=n_experts),
        grid=(n // tm,),
        in_specs=in_specs,
        out_specs=out_specs,
        out_shape=out_shape,
        compiler_params=_params(("arbitrary",)),
        name="out_proj",
    )(*args)


def _swiglu_tile(h, wg, wu):
    a = jnp.dot(h, wg, preferred_element_type=F32)
    b = jnp.dot(h, wu, preferred_element_type=F32)
    return a * jax.nn.sigmoid(a) * b


def _ffn_kernel(h_ref, wg_ref, wu_ref, wd_ref, x_ref, mod_ref, o_ref, *, gate_row):
    f = pl.program_id(1)
    t = _swiglu_tile(h_ref[...], wg_ref[...], wu_ref[...]).astype(BF16)

    @pl.when(f == 0)
    def _():
        o_ref[...] = jnp.dot(t, wd_ref[...], preferred_element_type=F32)

    @pl.when(f > 0)
    def _():
        o_ref[...] += jnp.dot(t, wd_ref[...], preferred_element_type=F32)

    @pl.when(f == pl.num_programs(1) - 1)
    def _():
        o_ref[...] = x_ref[...] + mod_ref[gate_row:gate_row + 1, :] * o_ref[...]


def ffn_dense(h, wg, wu, wd, x2, mod, seq, gate_row, tm=1024, tf=512):
    n, d = x2.shape
    ff = wg.shape[1]
    tm = _tile(seq, tm)
    tf = _tile(ff, tf)
    per_b = seq // tm
    return pl.pallas_call(
        functools.partial(_ffn_kernel, gate_row=gate_row),
        grid=(n // tm, ff // tf),
        in_specs=[
            pl.BlockSpec((tm, d), lambda i, f: (i, 0)),
            pl.BlockSpec((d, tf), lambda i, f: (0, f)),
            pl.BlockSpec((d, tf), lambda i, f: (0, f)),
            pl.BlockSpec((tf, d), lambda i, f: (f, 0)),
            pl.BlockSpec((tm, d), lambda i, f: (i, 0), pipeline_mode=pl.Buffered(1)),
            pl.BlockSpec((None, 6, d), lambda i, f: (i // per_b, 0, 0)),
        ],
        out_specs=pl.BlockSpec((tm, d), lambda i, f: (i, 0)),
        out_shape=jax.ShapeDtypeStruct((n, d), F32),
        compiler_params=_params(("arbitrary", "arbitrary")),
        name="ffn_dense",
    )(h, wg, wu, wd, x2, mod)


def _route_plan(route, n_experts, tm):
    n = route.shape[0]
    pairs = TOP_K * n
    e_flat = route[:, :TOP_K].astype(jnp.int32).reshape(pairs)
    onehot = (e_flat[:, None] == jnp.arange(n_experts, dtype=jnp.int32)[None, :]).astype(jnp.int32)
    csum = jnp.cumsum(onehot, axis=0)
    counts = csum[-1]
    padded = (counts + tm - 1) // tm * tm
    pend = jnp.cumsum(padded)
    pstart = pend - padded
    rank = jnp.sum(csum * onehot, axis=1) - 1
    pos = (jnp.sum(pstart[None, :] * onehot, axis=1) + rank).astype(jnp.int32)

    n_tiles = pairs // tm + n_experts
    first_row = jnp.arange(n_tiles, dtype=jnp.int32) * tm
    tile_expert = jnp.minimum(jnp.sum((pend[None, :] <= first_row[:, None]).astype(jnp.int32), axis=1),
                              n_experts - 1)
    valid = first_row < pend[-1]
    return pos, (tile_expert.astype(jnp.int32), valid.astype(jnp.int32),
                 (pstart + counts).astype(jnp.int32))


def _dispatch_kernel(pos_ref, h_ref, init_ref, xs_ref, sem):
    del init_ref
    i = pl.program_id(0)
    tm = h_ref.shape[0]

    def issue(r, carry):
        for k in range(TOP_K):
            dst = pos_ref[(i * tm + r) * TOP_K + k]
            pltpu.make_async_copy(h_ref.at[r], xs_ref.at[dst], sem).start()
        return carry

    lax.fori_loop(0, tm, issue, 0)
    for k in range(TOP_K):
        pltpu.make_async_copy(h_ref, xs_ref.at[pl.ds(0, tm)], sem).wait()


def moe_dispatch(h3, pos, rows, tm=512):
    n, sub, lanes = h3.shape
    tm = _tile(n, tm)
    return pl.pallas_call(
        _dispatch_kernel,
        grid_spec=pltpu.PrefetchScalarGridSpec(
            num_scalar_prefetch=1,
            grid=(n // tm,),
            in_specs=[pl.BlockSpec((tm, sub, lanes), lambda i, pos: (i, 0, 0)),
                      pl.BlockSpec(memory_space=pl.ANY)],
            out_specs=pl.BlockSpec(memory_space=pl.ANY),
            scratch_shapes=[pltpu.SemaphoreType.DMA(())],
        ),
        out_shape=jax.ShapeDtypeStruct((rows, sub, lanes), h3.dtype),
        input_output_aliases={2: 0},
        compiler_params=_params(("arbitrary",), disable_bounds_checks=True),
        name="moe_dispatch",
    )(pos, h3, jnp.zeros((rows, sub, lanes), h3.dtype))


def _moe_ffn_kernel(exp_ref, valid_ref, rowend_ref, xs_ref, wg_ref, wu_ref, wd_ref, o_ref, *, sub):
    t = pl.program_id(0)
    f = pl.program_id(1)
    tm = xs_ref.shape[0]

    @pl.when(valid_ref[t] == 1)
    def _():
        rows = t * tm + lax.broadcasted_iota(jnp.int32, (tm, 1), 0)
        real = rows < rowend_ref[exp_ref[t]]

        @pl.when(f == 0)
        def _():
            o_ref[...] = jnp.zeros(o_ref.shape, o_ref.dtype)

        for c in range(wg_ref.shape[1] // sub):
            cols = slice(c * sub, (c + 1) * sub)
            a = _swiglu_tile(xs_ref[...], wg_ref[:, cols].astype(BF16), wu_ref[:, cols].astype(BF16))
            a = jnp.where(real, a, 0.0).astype(BF16)
            o_ref[...] += jnp.dot(a, wd_ref[cols, :].astype(BF16), preferred_element_type=F32)

    @pl.when(jnp.logical_and(valid_ref[t] == 0, f == 0))
    def _():
        o_ref[...] = jnp.zeros(o_ref.shape, o_ref.dtype)


def moe_ffn(xs, plan, wg, wu, wd, tm, tf=512, sub=256):
    rows, d = xs.shape
    n_experts, _, ff = wg.shape
    tf = _tile(ff, tf)
    sub = _tile(tf, sub)
    nf = ff // tf

    def f_idx(t, f, va):
        return jnp.where(va[t] == 1, f, nf - 1)

    return pl.pallas_call(
        functools.partial(_moe_ffn_kernel, sub=sub),
        grid_spec=pltpu.PrefetchScalarGridSpec(
            num_scalar_prefetch=3,
            grid=(rows // tm, nf),
            in_specs=[
                pl.BlockSpec((tm, d), lambda t, f, ex, va, re: (t, 0)),
                pl.BlockSpec((None, d, tf), lambda t, f, ex, va, re: (ex[t], 0, f_idx(t, f, va))),
                pl.BlockSpec((None, d, tf), lambda t, f, ex, va, re: (ex[t], 0, f_idx(t, f, va))),
                pl.BlockSpec((None, tf, d), lambda t, f, ex, va, re: (ex[t], f_idx(t, f, va), 0)),
            ],
            out_specs=pl.BlockSpec((tm, d), lambda t, f, ex, va, re: (t, 0)),
        ),
        out_shape=jax.ShapeDtypeStruct((rows, d), F32),
        compiler_params=_params(("arbitrary", "arbitrary")),
        name="moe_ffn",
    )(*plan, xs, wg, wu, wd)


def _combine_kernel(pos_ref, ys_ref, route_ref, x_ref, mod_ref, g_ref, o_ref, buf, sem,
                    *, gate_row, final):
    i = pl.program_id(0)
    tm = x_ref.shape[0]
    slot = i % 2

    def gather(step, into):
        def issue(r, carry):
            for k in range(TOP_K):
                src = pos_ref[(step * tm + r) * TOP_K + k]
                pltpu.make_async_copy(ys_ref.at[pl.ds(src, 1), :], buf.at[into, k, pl.ds(r, 1), :],
                                      sem.at[into]).start()
            return carry

        lax.fori_loop(0, tm, issue, 0, unroll=2)

    @pl.when(i == 0)
    def _():
        gather(0, 0)

    more = i + 1 < pl.num_programs(0)
    for parity in range(2):
        @pl.when(jnp.logical_and(more, slot == parity))
        def _(parity=parity):
            gather(i + 1, 1 - parity)

    for k in range(TOP_K):
        pltpu.make_async_copy(ys_ref.at[pl.ds(0, tm), :], buf.at[slot, k], sem.at[slot]).wait()
    route = route_ref[...]
    y = buf[slot, 0] * route[:, TOP_K:TOP_K + 1]
    for k in range(1, TOP_K):
        y = y + buf[slot, k] * route[:, TOP_K + k:TOP_K + k + 1]
    x_new = x_ref[...] + mod_ref[gate_row:gate_row + 1, :] * y
    o_ref[...] = _rms(x_new, g_ref[...]) if final else x_new


def moe_combine(ys, pos, route, x2, mod, seq, gate_row, final_g=None, tm=256):
    n, d = x2.shape
    tm = _tile(seq, tm)
    per_b = seq // tm
    g = jnp.ones((1, d), F32) if final_g is None else final_g.reshape(1, d)
    return pl.pallas_call(
        functools.partial(_combine_kernel, gate_row=gate_row, final=final_g is not None),
        grid_spec=pltpu.PrefetchScalarGridSpec(
            num_scalar_prefetch=1,
            grid=(n // tm,),
            in_specs=[
                pl.BlockSpec(memory_space=pl.ANY),
                pl.BlockSpec((tm, LANES), lambda i, pos: (i, 0)),
                pl.BlockSpec((tm, d), lambda i, pos: (i, 0)),
                pl.BlockSpec((None, 6, d), lambda i, pos: (i // per_b, 0, 0)),
                pl.BlockSpec((1, d), lambda i, pos: (0, 0)),
            ],
            out_specs=pl.BlockSpec((tm, d), lambda i, pos: (i, 0)),
            scratch_shapes=[pltpu.VMEM((2, TOP_K, tm, d), F32), pltpu.SemaphoreType.DMA((2,))],
        ),
        out_shape=jax.ShapeDtypeStruct((n, d), F32),
        compiler_params=_params(("arbitrary",), disable_bounds_checks=True),
        name="moe_combine",
    )(pos, ys, route, x2, mod, g)


def ffn_moe(h, route, wg, wu, wd, x2, mod, seq, gate_row, final_g=None, tm=1024):
    n, d = x2.shape
    n_experts = wg.shape[0]
    tm = _tile(TOP_K * n, tm)
    pos, plan = _route_plan(route, n_experts, tm)
    rows = TOP_K * n + n_experts * tm
    xs3 = moe_dispatch(h.reshape(n, d // LANES, LANES), pos, rows)
    ys = moe_ffn(xs3.reshape(rows, d), plan, wg, wu, wd, tm)
    return moe_combine(ys, pos, route, x2, mod, seq, gate_row, final_g)


def _rope_tables(seq):
    inv = ROPE_THETA ** (-jnp.arange(0, HEAD_DIM, 2, dtype=F32) / HEAD_DIM)
    ang = jnp.arange(seq, dtype=F32)[:, None] * inv[None, :]
    cos, sin = jnp.cos(ang), jnp.sin(ang)
    return jnp.concatenate([cos, cos], axis=-1), jnp.concatenate([-sin, sin], axis=-1)


def kernel(x, c, attn_norm_g, ada_w, ada_b, w_in, forget_b, moba_out_g, fox_out_g, w_o, ffn_norm_g,
           dense_w_gate, dense_w_up, dense_w_down, router_w, moe_w_gate, moe_w_up, moe_w_down,
           final_norm_g):
    batch, seq, d = x.shape
    depth = attn_norm_g.shape[0]
    moba_w = moba_out_g.shape[1]
    fox_w = fox_out_g.shape[1]
    moba_h = moba_w // HEAD_DIM
    fox_h = fox_w // HEAD_DIM
    qkv_w = 3 * moba_w + 3 * fox_w
    assert moba_w == fox_w
    kinds = (ROPE_SCALED, ROPE, PLAIN, SCALED, PLAIN, PLAIN)
    hb = HEAD_DIM

    cos_t, sin_t = _rope_tables(seq)
    mods = ada_modulation(c, ada_w, ada_b)
    x2 = x.reshape(batch * seq, d)
    out = None

    for l in range(depth):
        mod = mods[l]
        h = norm_mod(x2, attn_norm_g[l], mod, seq, sh_row=0)
        w_qkv = w_in[l][:, :qkv_w].astype(BF16)
        wz = jnp.zeros((d, LANES), BF16).at[:, :fox_h].set(w_in[l][:, qkv_w:].astype(BF16))
        proj, fz = in_proj(h, w_qkv, wz, cos_t, sin_t, seq, kinds, tn=moba_w)
        cum_t = logf_cumsum(fz[:, :fox_h].reshape(batch, seq, fox_h), forget_b[l])
        moba_o = attention(proj, batch, seq, moba_h, q_col=0, k_col=moba_w // hb,
                           v_col=2 * moba_w // hb)
        fc = 3 * moba_w // hb
        fox_o = attention(proj, batch, seq, fox_h, q_col=fc, k_col=fc + fox_w // hb,
                          v_col=fc + 2 * fox_w // hb, cum_t=cum_t)
        i = l // 2
        last = l == depth - 1
        w_o_l = w_o[l].astype(BF16)
        if l % 2 == 0:
            x2, h = out_proj(moba_o, fox_o, moba_out_g[l], fox_out_g[l], w_o_l, x2, mod,
                             ffn_norm_g[l], seq, gate_row=2, sh_row=3)
            x2 = ffn_dense(h, dense_w_gate[i].astype(BF16), dense_w_up[i].astype(BF16),
                           dense_w_down[i].astype(BF16), x2, mod, seq, gate_row=5)
        else:
            x2, h, route = out_proj(moba_o, fox_o, moba_out_g[l], fox_out_g[l], w_o_l, x2, mod,
                                    ffn_norm_g[l], seq, gate_row=2, sh_row=3, router_w=router_w[i])
            x2 = ffn_moe(h, route, moe_w_gate[i], moe_w_up[i], moe_w_down[i], x2, mod, seq,
                         gate_row=5, final_g=final_norm_g if last else None)
            if last:
                out = x2
    if out is None:
        out = final_norm(x2, final_norm_g)
    return out.reshape(batch, seq, d)
```

```python
import functools
import math

import jax
import jax.numpy as jnp
from jax import lax
from jax.experimental import pallas as pl
from jax.experimental.pallas import tpu as pltpu

F32 = jnp.float32
BF16 = jnp.bfloat16

HEAD_DIM = 128
MOBA_BLOCK = 256
MOBA_TOPK = 3
ROPE_THETA = 10000.0
EPS = 1e-6
TOP_K = 2
LANES = 128
VMEM_LIMIT = 56 * 1024 * 1024

NEG_INF = float("-inf")
NEG_BIG = -(2.0 ** 100)
LOG2E = math.log2(math.e)
Q_SCALE = HEAD_DIM ** -0.5 * LOG2E


def _params(semantics, **kw):
    return pltpu.CompilerParams(dimension_semantics=semantics, vmem_limit_bytes=VMEM_LIMIT, **kw)


def _tile(n, want):
    t = min(n, want)
    assert n % t == 0, (n, want)
    return t


def _ada_kernel(c_ref, w_ref, b_ref, o_ref):
    c = c_ref[...]
    c_act = c * jax.nn.sigmoid(c)
    o_ref[...] = jnp.dot(c_act, w_ref[...], preferred_element_type=F32,
                         precision=lax.Precision.HIGHEST) + b_ref[...]


def ada_modulation(c, ada_w, ada_b):
    n_layers, d, d6 = ada_w.shape
    b = c.shape[0]
    rows = -(-b // 8) * 8
    c_pad = jnp.zeros((rows, d), F32).at[:b].set(c)
    tn = _tile(d6, 1024)
    out = pl.pallas_call(
        _ada_kernel,
        grid=(n_layers, d6 // tn),
        in_specs=[
            pl.BlockSpec((rows, d), lambda l, j: (0, 0)),
            pl.BlockSpec((None, d, tn), lambda l, j: (l, 0, j)),
            pl.BlockSpec((None, 1, tn), lambda l, j: (l, 0, j)),
        ],
        out_specs=pl.BlockSpec((None, rows, tn), lambda l, j: (l, 0, j)),
        out_shape=jax.ShapeDtypeStruct((n_layers, rows, d6), F32),
        compiler_params=_params(("arbitrary", "arbitrary")),
        name="ada_modulation",
    )(c_pad, ada_w, ada_b.reshape(n_layers, 1, d6))
    return out[:, :b].reshape(n_layers, b, 6, d)


def _rms(x, g):
    return x * lax.rsqrt(jnp.mean(x * x, axis=-1, keepdims=True) + EPS) * g


def _top2_route(logits, n_experts):
    lane = lax.broadcasted_iota(jnp.int32, logits.shape, 1).astype(F32)
    lg = jnp.where(lane < n_experts, logits, NEG_INF)
    m1 = jnp.max(lg, axis=-1, keepdims=True)
    i1 = jnp.min(jnp.where(lg == m1, lane, float(LANES)), axis=-1, keepdims=True)
    lg2 = jnp.where(lane == i1, NEG_INF, lg)
    m2 = jnp.max(lg2, axis=-1, keepdims=True)
    i2 = jnp.min(jnp.where(lg2 == m2, lane, float(LANES)), axis=-1, keepdims=True)
    e2 = jnp.exp(m2 - m1)
    w1 = 1.0 / (1.0 + e2)
    w2 = e2 / (1.0 + e2)
    return jnp.where(lane == 0.0, i1, jnp.where(lane == 1.0, i2,
                     jnp.where(lane == 2.0, w1, jnp.where(lane == 3.0, w2, 0.0))))


def _final_norm_kernel(x_ref, g_ref, o_ref):
    o_ref[...] = _rms(x_ref[...], g_ref[...])


def final_norm(x2, g, tm=512):
    n, d = x2.shape
    tm = _tile(n, tm)
    return pl.pallas_call(
        _final_norm_kernel,
        grid=(n // tm,),
        in_specs=[pl.BlockSpec((tm, d), lambda i: (i, 0)),
                  pl.BlockSpec((1, d), lambda i: (0, 0))],
        out_specs=pl.BlockSpec((tm, d), lambda i: (i, 0)),
        out_shape=jax.ShapeDtypeStruct((n, d), F32),
        compiler_params=_params(("arbitrary",)),
        name="final_norm",
    )(x2, g.reshape(1, d))


PLAIN, ROPE, ROPE_SCALED, SCALED = 0, 1, 2, 3


def _in_proj_kernel(x_ref, g_ref, mod_ref, w_ref, wz_ref, cos_ref, sin_ref, o_ref, fz_ref, h_scr,
                    *, kinds, sh_row):
    j = pl.program_id(1)
    tn = w_ref.shape[1]

    @pl.when(j == 0)
    def _():
        y = _rms(x_ref[...], g_ref[...])
        sh = mod_ref[sh_row:sh_row + 1, :]
        sc = mod_ref[sh_row + 1:sh_row + 2, :]
        h_scr[...] = (y * (1.0 + sc) + sh).astype(h_scr.dtype)
        fz_ref[...] = jnp.dot(h_scr[...], wz_ref[...], preferred_element_type=F32)

    def is_kind(kind):
        hits = [jj for jj, k in enumerate(kinds) if k == kind]
        cond = j == hits[0]
        for jj in hits[1:]:
            cond = jnp.logical_or(cond, j == jj)
        return cond

    for kind in sorted(set(kinds)):
        @pl.when(is_kind(kind))
        def _(kind=kind):
            acc = jnp.dot(h_scr[...], w_ref[...], preferred_element_type=F32)
            mult = Q_SCALE if kind in (ROPE_SCALED, SCALED) else 1.0
            if kind in (ROPE, ROPE_SCALED):
                c = cos_ref[...] * mult
                s = sin_ref[...] * mult
                for hh in range(tn // HEAD_DIM):
                    t = acc[:, hh * HEAD_DIM:(hh + 1) * HEAD_DIM]
                    rot = pltpu.roll(t, HEAD_DIM // 2, 1)
                    o_ref[:, hh * HEAD_DIM:(hh + 1) * HEAD_DIM] = (t * c + rot * s).astype(o_ref.dtype)
            elif kind == SCALED:
                o_ref[...] = (acc * mult).astype(o_ref.dtype)
            else:
                o_ref[...] = acc.astype(o_ref.dtype)


def in_proj(x2, g, mod, w_qkv, wz, cos_t, sin_t, seq, kinds, tn, sh_row, tm=1024):
    n, d = x2.shape
    width = w_qkv.shape[1]
    assert width == tn * len(kinds)
    tm = _tile(seq, tm)
    per_b = seq // tm
    return pl.pallas_call(
        functools.partial(_in_proj_kernel, kinds=tuple(kinds), sh_row=sh_row),
        grid=(n // tm, width // tn),
        in_specs=[
            pl.BlockSpec((tm, d), lambda i, j: (i, 0)),
            pl.BlockSpec((1, d), lambda i, j: (0, 0)),
            pl.BlockSpec((None, 6, d), lambda i, j: (i // per_b, 0, 0)),
            pl.BlockSpec((d, tn), lambda i, j: (0, j)),
            pl.BlockSpec((d, LANES), lambda i, j: (0, 0)),
            pl.BlockSpec((tm, HEAD_DIM), lambda i, j: (i % per_b, 0)),
            pl.BlockSpec((tm, HEAD_DIM), lambda i, j: (i % per_b, 0)),
        ],
        out_specs=[pl.BlockSpec((tm, tn), lambda i, j: (i, j)),
                   pl.BlockSpec((tm, LANES), lambda i, j: (i, 0))],
        out_shape=[jax.ShapeDtypeStruct((n, width), BF16),
                   jax.ShapeDtypeStruct((n, LANES), F32)],
        scratch_shapes=[pltpu.VMEM((tm, d), BF16)],
        compiler_params=_params(("arbitrary", "arbitrary")),
        name="in_proj",
    )(x2, g.reshape(1, d), mod, w_qkv, wz, cos_t, sin_t)


def _logf_cumsum_kernel(z_ref, b_ref, o_ref):
    z = z_ref[...] + b_ref[...]
    lf = jnp.minimum(z, 0.0) - jnp.log1p(jnp.exp(-jnp.abs(z)))
    r = lax.broadcasted_iota(jnp.int32, (LANES, LANES), 0)
    c = lax.broadcasted_iota(jnp.int32, (LANES, LANES), 1)
    tri = jnp.where(c <= r, 1.0, 0.0).astype(F32)
    carry = jnp.zeros((1, z.shape[1]), F32)
    for ch in range(z.shape[0] // LANES):
        seg = lf[ch * LANES:(ch + 1) * LANES, :]
        cs = jnp.dot(tri, seg, preferred_element_type=F32, precision=lax.Precision.HIGHEST) + carry
        o_ref[ch * LANES:(ch + 1) * LANES, :] = cs
        carry = cs[LANES - 1:LANES, :]


def logf_cumsum(z, forget_b):
    b, s, h = z.shape
    return pl.pallas_call(
        _logf_cumsum_kernel,
        grid=(b,),
        in_specs=[pl.BlockSpec((None, s, h), lambda i: (i, 0, 0)),
                  pl.BlockSpec((1, h), lambda i: (0, 0))],
        out_specs=pl.BlockSpec((None, s, h), lambda i: (i, 0, 0)),
        out_shape=jax.ShapeDtypeStruct((b, s, h), F32),
        compiler_params=_params(("arbitrary",)),
        name="logf_cumsum",
    )(z, forget_b.reshape(1, h))


def _split3(f):
    hi = f.astype(BF16).astype(F32)
    r = f - hi
    mid = r.astype(BF16).astype(F32)
    lo = (r - mid).astype(BF16).astype(F32)
    return hi, mid, lo


def _pick_lane(x, idx):
    lane = lax.broadcasted_iota(jnp.int32, x.shape, 1)
    return jnp.sum(jnp.where(lane == idx, x, 0.0), axis=-1, keepdims=True)


def _attn_kernel(*refs, mode, tq, tk, seq):
    if mode == "fox":
        q_ref, k_ref, v_ref, ct_ref, o_ref, kaug, vaug, qaug, s_a, s_b, bias_scr, m_scr, acc_scr = refs
    else:
        (q_ref, k_ref, v_ref, o_ref, kaug, vaug, qaug, s_a, s_b, bias_scr, m_scr, acc_scr,
         kmean_scr, kmean3_scr) = refs
    hg = pl.program_id(1)
    i = pl.program_id(2)
    n_moba_blocks = seq // MOBA_BLOCK
    nbp = -(-n_moba_blocks // 16) * 16
    group = q_ref.shape[1] // HEAD_DIM

    def head_cols(g):
        return slice(g * HEAD_DIM, (g + 1) * HEAD_DIM)

    @pl.when(jnp.logical_and(jnp.logical_and(pl.program_id(0) == 0, hg == 0), i == 0))
    def _():
        r2 = lax.broadcasted_iota(jnp.int32, (tk, tk), 0)
        c2 = lax.broadcasted_iota(jnp.int32, (tk, tk), 1)
        ok = c2 <= r2
        if mode == "moba":
            ok = jnp.logical_or(ok, c2 // MOBA_BLOCK != r2 // MOBA_BLOCK)
        bias_scr[...] = jnp.where(ok, 0.0, NEG_INF)

    @pl.when(i == 0)
    def _():
        lane = lax.broadcasted_iota(jnp.int32, (tk, LANES), 1)
        row = lax.broadcasted_iota(jnp.int32, (tk, LANES), 0)

        def build(cidx, carry):
            st = pl.multiple_of(cidx * tk, tk)
            for g in range(group):
                kaug[g, pl.ds(st, tk), 0:HEAD_DIM] = k_ref[pl.ds(st, tk), head_cols(g)]
                vaug[g, pl.ds(st, tk), 0:HEAD_DIM] = v_ref[pl.ds(st, tk), head_cols(g)]
                vaug[g, pl.ds(st, tk), HEAD_DIM:2 * HEAD_DIM] = jnp.ones((tk, HEAD_DIM), BF16)
                if mode == "fox":
                    c = _pick_lane(ct_ref[pl.ds(st, tk), :], hg * group + g) * LOG2E
                    hi, mid, lo = _split3(c)
                    ext = jnp.where(lane < 3, 1.0, jnp.where(lane == 3, -hi, jnp.where(
                        lane == 4, -mid, jnp.where(lane == 5, -lo, 0.0))))
                else:
                    ext = jnp.where(lane == (st + row) // MOBA_BLOCK, 1.0, 0.0)
                kaug[g, pl.ds(st, tk), HEAD_DIM:2 * HEAD_DIM] = ext.astype(BF16)
            return carry

        lax.fori_loop(0, seq // tk, build, 0)
        if mode == "moba":
            kmean_scr[...] = jnp.zeros_like(kmean_scr)
            for g in range(group):
                for b in range(n_moba_blocks):
                    kb = k_ref[b * MOBA_BLOCK:(b + 1) * MOBA_BLOCK, head_cols(g)].astype(F32)
                    kmean_scr[g, b:b + 1, :] = jnp.mean(kb, axis=0, keepdims=True)
                for piece, part in enumerate(_split3(kmean_scr[g, 0:nbp, :])):
                    kmean3_scr[g, piece * nbp:(piece + 1) * nbp, :] = part.astype(BF16)

    lane = lax.broadcasted_iota(jnp.int32, (MOBA_BLOCK, LANES), 1)
    for rb in range(tq // MOBA_BLOCK):
        rows = pl.ds(rb * MOBA_BLOCK, MOBA_BLOCK)
        for g in range(group):
            q = q_ref[rows, head_cols(g)]
            if mode == "fox":
                start = pl.multiple_of(i * tq + rb * MOBA_BLOCK, MOBA_BLOCK)
                f = _pick_lane(ct_ref[pl.ds(start, MOBA_BLOCK), :], hg * group + g) * LOG2E
                hi, mid, lo = _split3(f)
                ext = jnp.where(lane == 0, hi, jnp.where(lane == 1, mid, jnp.where(
                    lane == 2, lo, jnp.where(lane < 6, 1.0, 0.0))))
            else:
                g3 = lax.dot_general(kmean3_scr[g], q, (((1,), (1,)), ((), ())),
                                     preferred_element_type=F32)
                gate = g3[0:nbp] + g3[nbp:2 * nbp] + g3[2 * nbp:3 * nbp]
                own = i * (tq // MOBA_BLOCK) + rb
                blk = lax.broadcasted_iota(jnp.int32, gate.shape, 0)
                blk_f = blk.astype(F32)
                gv = jnp.where(blk < own, gate, NEG_INF)
                keep = blk == own
                for _ in range(min(MOBA_TOPK, n_moba_blocks)):
                    m = jnp.max(gv, axis=0, keepdims=True)
                    idx = jnp.min(jnp.where(gv == m, blk_f, float(LANES)), axis=0, keepdims=True)
                    hit = blk_f == idx
                    keep = jnp.logical_or(keep, jnp.logical_and(hit, m > NEG_INF))
                    gv = jnp.where(hit, NEG_INF, gv)
                ext_t = jnp.where(keep, 0.0, NEG_BIG)
                ext_t = jnp.concatenate(
                    [ext_t, jnp.zeros((LANES - nbp, MOBA_BLOCK), F32)], axis=0)
                ext = ext_t.T
            qaug[g, rows, 0:HEAD_DIM] = q
            qaug[g, rows, HEAD_DIM:2 * HEAD_DIM] = ext.astype(BF16)

    def scores(g, j):
        st = pl.multiple_of(j * tk, tk)
        return lax.dot_general(qaug[g], kaug[g, pl.ds(st, tk), :], (((1,), (1,)), ((), ())),
                               preferred_element_type=F32)

    def update(g, s, j, rows=slice(0, tq), causal=False):
        st = pl.multiple_of(j * tk, tk)
        if causal:
            s = s + bias_scr[...]
        m_old = m_scr[g, rows]
        m_new = jnp.maximum(m_old, jnp.max(s, axis=-1, keepdims=True))
        p = jnp.exp2(s - m_new).astype(BF16)
        pv = jnp.dot(p, vaug[g, pl.ds(st, tk), :], preferred_element_type=F32)
        acc_scr[g, rows] = jnp.exp2(m_old - m_new) * acc_scr[g, rows] + pv
        m_scr[g, rows] = m_new

    heads = range(group)
    m_scr[...] = jnp.full(m_scr.shape, NEG_INF, F32)
    acc_scr[...] = jnp.zeros(acc_scr.shape, F32)
    for g in heads:
        s_a[g] = scores(g, 0)

    def body(jj, carry):
        for g in heads:
            s_b[g] = scores(g, 2 * jj + 1)
        for g in heads:
            update(g, s_a[g], 2 * jj)
        for g in heads:
            s_a[g] = scores(g, 2 * jj + 2)
        for g in heads:
            update(g, s_b[g], 2 * jj + 1)
        return carry

    lax.fori_loop(0, i, body, 0)
    upper, lower = slice(0, tk), slice(tk, tq)
    for g in heads:
        st = pl.multiple_of((2 * i + 1) * tk, tk)
        s_b[g, upper] = lax.dot_general(qaug[g, lower], kaug[g, pl.ds(st, tk), :],
                                        (((1,), (1,)), ((), ())), preferred_element_type=F32)
    for g in heads:
        update(g, s_a[g, upper], 2 * i, upper, causal=True)
        update(g, s_a[g, lower], 2 * i, lower)
    for g in heads:
        update(g, s_b[g, upper], 2 * i + 1, lower, causal=True)
    for g in heads:
        o_ref[:, head_cols(g)] = (acc_scr[g, :, 0:HEAD_DIM]
                                  / acc_scr[g, :, HEAD_DIM:2 * HEAD_DIM]).astype(o_ref.dtype)


def attention(proj, batch, seq, heads, q_col, k_col, v_col, cum_t=None, tq=1024, group=2):
    n = proj.shape[0]
    tq = _tile(seq, tq)
    tk = tq // 2
    nq = seq // tq
    mode = "moba" if cum_t is None else "fox"
    assert tk % MOBA_BLOCK == 0 and seq // MOBA_BLOCK <= LANES
    gw = group * HEAD_DIM
    assert heads % group == 0 and q_col % group == 0 and k_col % group == 0 and v_col % group == 0
    in_specs = [
        pl.BlockSpec((tq, gw), lambda b, h, i: (b * nq + i, q_col // group + h)),
        pl.BlockSpec((seq, gw), lambda b, h, i: (b, k_col // group + h)),
        pl.BlockSpec((seq, gw), lambda b, h, i: (b, v_col // group + h)),
    ]
    args = [proj, proj, proj]
    wide = 2 * HEAD_DIM
    scratch = [pltpu.VMEM((group, seq, wide), BF16), pltpu.VMEM((group, seq, wide), BF16),
               pltpu.VMEM((group, tq, wide), BF16),
               pltpu.VMEM((group, tq, tk), F32), pltpu.VMEM((group, tq, tk), F32),
               pltpu.VMEM((tk, tk), F32),
               pltpu.VMEM((group, tq, 1), F32), pltpu.VMEM((group, tq, wide), F32)]
    if mode == "fox":
        in_specs.append(pl.BlockSpec((None, seq, heads), lambda b, h, i: (b, 0, 0)))
        args.append(cum_t)
    else:
        scratch += [pltpu.VMEM((group, LANES, HEAD_DIM), F32),
                    pltpu.VMEM((group, 3 * (-(-(seq // MOBA_BLOCK) // 16) * 16), HEAD_DIM), BF16)]
    return pl.pallas_call(
        functools.partial(_attn_kernel, mode=mode, tq=tq, tk=tk, seq=seq),
        grid=(batch, heads // group, nq),
        in_specs=in_specs,
        out_specs=pl.BlockSpec((tq, gw), lambda b, h, i: (b * nq + i, h)),
        out_shape=jax.ShapeDtypeStruct((n, heads * HEAD_DIM), BF16),
        scratch_shapes=scratch,
        compiler_params=_params(("arbitrary", "arbitrary", "arbitrary")),
        name=mode + "_attention",
    )(*args)


def _out_proj_kernel(*refs, gate_row, sh_row, n_experts):
    if n_experts:
        (mo_ref, fo_ref, g1_ref, g2_ref, w_ref, x_ref, mod_ref, gf_ref, wr_ref,
         o_ref, h_ref, route_ref) = refs
    else:
        mo_ref, fo_ref, g1_ref, g2_ref, w_ref, x_ref, mod_ref, gf_ref, o_ref, h_ref = refs
    half = mo_ref.shape[1]
    m1 = _rms(mo_ref[...].astype(F32), g1_ref[...]).astype(BF16)
    m2 = _rms(fo_ref[...].astype(F32), g2_ref[...]).astype(BF16)
    acc = jnp.dot(m1, w_ref[0:half, :], preferred_element_type=F32)
    acc = acc + jnp.dot(m2, w_ref[half:2 * half, :], preferred_element_type=F32)
    x_new = x_ref[...] + mod_ref[gate_row:gate_row + 1, :] * acc
    o_ref[...] = x_new
    sh = mod_ref[sh_row:sh_row + 1, :]
    sc = mod_ref[sh_row + 1:sh_row + 2, :]
    h = _rms(x_new, gf_ref[...]) * (1.0 + sc) + sh
    h_hi = h.astype(BF16)
    h_ref[...] = h_hi
    if n_experts:
        h_lo = (h - h_hi.astype(F32)).astype(BF16)
        l_hi = jnp.dot(h_hi, wr_ref[...], preferred_element_type=F32)
        l_lo = jnp.dot(h_lo, wr_ref[:, 0:LANES], preferred_element_type=F32)
        logits = l_hi[:, 0:LANES] + l_hi[:, LANES:2 * LANES] + l_lo
        route_ref[...] = _top2_route(logits, n_experts)


def out_proj(moba_o, fox_o, g1, g2, w_o, x2, mod, ffn_g, seq, gate_row, sh_row, router_w=None, tm=512):
    n, d = x2.shape
    half = moba_o.shape[1]
    tm = _tile(seq, tm)
    per_b = seq // tm
    n_experts = 0 if router_w is None else router_w.shape[1]
    in_specs = [
        pl.BlockSpec((tm, half), lambda i: (i, 0)),
        pl.BlockSpec((tm, half), lambda i: (i, 0)),
        pl.BlockSpec((1, half), lambda i: (0, 0)),
        pl.BlockSpec((1, half), lambda i: (0, 0)),
        pl.BlockSpec((2 * half, d), lambda i: (0, 0)),
        pl.BlockSpec((tm, d), lambda i: (i, 0)),
        pl.BlockSpec((None, 6, d), lambda i: (i // per_b, 0, 0)),
        pl.BlockSpec((1, d), lambda i: (0, 0)),
    ]
    args = [moba_o, fox_o, g1.reshape(1, half), g2.reshape(1, half), w_o, x2, mod, ffn_g.reshape(1, d)]
    out_specs = [pl.BlockSpec((tm, d), lambda i: (i, 0)), pl.BlockSpec((tm, d), lambda i: (i, 0))]
    out_shape = [jax.ShapeDtypeStruct((n, d), F32), jax.ShapeDtypeStruct((n, d), BF16)]
    if n_experts:
        in_specs.append(pl.BlockSpec((d, 2 * LANES), lambda i: (0, 0)))
        w_hi = router_w.astype(BF16)
        w_lo = (router_w - w_hi.astype(F32)).astype(BF16)
        args.append(jnp.zeros((d, 2 * LANES), BF16).at[:, :n_experts].set(w_hi)
                    .at[:, LANES:LANES + n_experts].set(w_lo))
        out_specs.append(pl.BlockSpec((tm, LANES), lambda i: (i, 0)))
        out_shape.append(jax.ShapeDtypeStruct((n, LANES), F32))
    return pl.pallas_call(
        functools.partial(_out_proj_kernel, gate_row=gate_row, sh_row=sh_row, n_experts=n_experts),
        grid=(n // tm,),
        in_specs=in_specs,
        out_specs=out_specs,
        out_shape=out_shape,
        compiler_params=_params(("arbitrary",)),
        name="out_proj",
    )(*args)


def _swiglu_tile(h, wg, wu):
    a = jnp.dot(h, wg, preferred_element_type=F32)
    b = jnp.dot(h, wu, preferred_element_type=F32)
    return a * jax.nn.sigmoid(a) * b


def _ffn_kernel(h_ref, wg_ref, wu_ref, wd_ref, x_ref, mod_ref, o_ref, *, gate_row):
    f = pl.program_id(1)

    @pl.when(f == 0)
    def _():
        o_ref[...] = jnp.zeros(o_ref.shape, o_ref.dtype)

    t = _swiglu_tile(h_ref[...], wg_ref[...], wu_ref[...]).astype(BF16)
    o_ref[...] += jnp.dot(t, wd_ref[...], preferred_element_type=F32)

    @pl.when(f == pl.num_programs(1) - 1)
    def _():
        o_ref[...] = x_ref[...] + mod_ref[gate_row:gate_row + 1, :] * o_ref[...]


def ffn_dense(h, wg, wu, wd, x2, mod, seq, gate_row, tm=1024, tf=512):
    n, d = x2.shape
    ff = wg.shape[1]
    tm = _tile(seq, tm)
    tf = _tile(ff, tf)
    per_b = seq // tm
    return pl.pallas_call(
        functools.partial(_ffn_kernel, gate_row=gate_row),
        grid=(n // tm, ff // tf),
        in_specs=[
            pl.BlockSpec((tm, d), lambda i, f: (i, 0)),
            pl.BlockSpec((d, tf), lambda i, f: (0, f)),
            pl.BlockSpec((d, tf), lambda i, f: (0, f)),
            pl.BlockSpec((tf, d), lambda i, f: (f, 0)),
            pl.BlockSpec((tm, d), lambda i, f: (i, 0), pipeline_mode=pl.Buffered(1)),
            pl.BlockSpec((None, 6, d), lambda i, f: (i // per_b, 0, 0)),
        ],
        out_specs=pl.BlockSpec((tm, d), lambda i, f: (i, 0)),
        out_shape=jax.ShapeDtypeStruct((n, d), F32),
        compiler_params=_params(("arbitrary", "arbitrary")),
        name="ffn_dense",
    )(h, wg, wu, wd, x2, mod)


def _route_plan(route, n_experts, tm):
    n = route.shape[0]
    pairs = TOP_K * n
    e_flat = route[:, :TOP_K].astype(jnp.int32).reshape(pairs)
    onehot = (e_flat[:, None] == jnp.arange(n_experts, dtype=jnp.int32)[None, :]).astype(jnp.int32)
    csum = jnp.cumsum(onehot, axis=0)
    counts = csum[-1]
    padded = (counts + tm - 1) // tm * tm
    pend = jnp.cumsum(padded)
    pstart = pend - padded
    rank = jnp.sum(csum * onehot, axis=1) - 1
    pos = (jnp.sum(pstart[None, :] * onehot, axis=1) + rank).astype(jnp.int32)

    n_tiles = pairs // tm + n_experts
    first_row = jnp.arange(n_tiles, dtype=jnp.int32) * tm
    tile_expert = jnp.minimum(jnp.sum((pend[None, :] <= first_row[:, None]).astype(jnp.int32), axis=1),
                              n_experts - 1)
    valid = first_row < pend[-1]
    return pos, (tile_expert.astype(jnp.int32), valid.astype(jnp.int32),
                 (pstart + counts).astype(jnp.int32))


def _dispatch_kernel(pos_ref, h_ref, init_ref, xs_ref, sem):
    del init_ref
    i = pl.program_id(0)
    tm = h_ref.shape[0]

    def issue(r, carry):
        for k in range(TOP_K):
            dst = pos_ref[(i * tm + r) * TOP_K + k]
            pltpu.make_async_copy(h_ref.at[r], xs_ref.at[dst], sem).start()
        return carry

    lax.fori_loop(0, tm, issue, 0)
    for k in range(TOP_K):
        pltpu.make_async_copy(h_ref, xs_ref.at[pl.ds(0, tm)], sem).wait()


def moe_dispatch(h3, pos, rows, tm=512):
    n, sub, lanes = h3.shape
    tm = _tile(n, tm)
    return pl.pallas_call(
        _dispatch_kernel,
        grid_spec=pltpu.PrefetchScalarGridSpec(
            num_scalar_prefetch=1,
            grid=(n // tm,),
            in_specs=[pl.BlockSpec((tm, sub, lanes), lambda i, pos: (i, 0, 0)),
                      pl.BlockSpec(memory_space=pl.ANY)],
            out_specs=pl.BlockSpec(memory_space=pl.ANY),
            scratch_shapes=[pltpu.SemaphoreType.DMA(())],
        ),
        out_shape=jax.ShapeDtypeStruct((rows, sub, lanes), h3.dtype),
        input_output_aliases={2: 0},
        compiler_params=_params(("arbitrary",), disable_bounds_checks=True),
        name="moe_dispatch",
    )(pos, h3, jnp.zeros((rows, sub, lanes), h3.dtype))


def _moe_ffn_kernel(exp_ref, valid_ref, rowend_ref, xs_ref, wg_ref, wu_ref, wd_ref, o_ref, *, sub):
    t = pl.program_id(0)
    f = pl.program_id(1)
    tm = xs_ref.shape[0]

    @pl.when(valid_ref[t] == 1)
    def _():
        rows = t * tm + lax.broadcasted_iota(jnp.int32, (tm, 1), 0)
        real = rows < rowend_ref[exp_ref[t]]

        @pl.when(f == 0)
        def _():
            o_ref[...] = jnp.zeros(o_ref.shape, o_ref.dtype)

        for c in range(wg_ref.shape[1] // sub):
            cols = slice(c * sub, (c + 1) * sub)
            a = _swiglu_tile(xs_ref[...], wg_ref[:, cols].astype(BF16), wu_ref[:, cols].astype(BF16))
            a = jnp.where(real, a, 0.0).astype(BF16)
            o_ref[...] += jnp.dot(a, wd_ref[cols, :].astype(BF16), preferred_element_type=F32)

    @pl.when(jnp.logical_and(valid_ref[t] == 0, f == 0))
    def _():
        o_ref[...] = jnp.zeros(o_ref.shape, o_ref.dtype)


def moe_ffn(xs, plan, wg, wu, wd, tm, tf=512, sub=256):
    rows, d = xs.shape
    n_experts, _, ff = wg.shape
    tf = _tile(ff, tf)
    sub = _tile(tf, sub)
    nf = ff // tf

    def f_idx(t, f, va):
        return jnp.where(va[t] == 1, f, nf - 1)

    return pl.pallas_call(
        functools.partial(_moe_ffn_kernel, sub=sub),
        grid_spec=pltpu.PrefetchScalarGridSpec(
            num_scalar_prefetch=3,
            grid=(rows // tm, nf),
            in_specs=[
                pl.BlockSpec((tm, d), lambda t, f, ex, va, re: (t, 0)),
                pl.BlockSpec((None, d, tf), lambda t, f, ex, va, re: (ex[t], 0, f_idx(t, f, va))),
                pl.BlockSpec((None, d, tf), lambda t, f, ex, va, re: (ex[t], 0, f_idx(t, f, va))),
                pl.BlockSpec((None, tf, d), lambda t, f, ex, va, re: (ex[t], f_idx(t, f, va), 0)),
            ],
            out_specs=pl.BlockSpec((tm, d), lambda t, f, ex, va, re: (t, 0)),
        ),
        out_shape=jax.ShapeDtypeStruct((rows, d), F32),
        compiler_params=_params(("arbitrary", "arbitrary")),
        name="moe_ffn",
    )(*plan, xs, wg, wu, wd)


def _combine_kernel(pos_ref, ys_ref, route_ref, x_ref, mod_ref, g_ref, o_ref, buf, sem,
                    *, gate_row, final):
    i = pl.program_id(0)
    tm = x_ref.shape[0]
    slot = i % 2

    def gather(step, into):
        def issue(r, carry):
            for k in range(TOP_K):
                src = pos_ref[(step * tm + r) * TOP_K + k]
                pltpu.make_async_copy(ys_ref.at[pl.ds(src, 1), :], buf.at[into, k, pl.ds(r, 1), :],
                                      sem.at[into]).start()
            return carry

        lax.fori_loop(0, tm, issue, 0, unroll=2)

    @pl.when(i == 0)
    def _():
        gather(0, 0)

    more = i + 1 < pl.num_programs(0)
    for parity in range(2):
        @pl.when(jnp.logical_and(more, slot == parity))
        def _(parity=parity):
            gather(i + 1, 1 - parity)

    for k in range(TOP_K):
        pltpu.make_async_copy(ys_ref.at[pl.ds(0, tm), :], buf.at[slot, k], sem.at[slot]).wait()
    route = route_ref[...]
    y = buf[slot, 0] * route[:, TOP_K:TOP_K + 1]
    for k in range(1, TOP_K):
        y = y + buf[slot, k] * route[:, TOP_K + k:TOP_K + k + 1]
    x_new = x_ref[...] + mod_ref[gate_row:gate_row + 1, :] * y
    o_ref[...] = _rms(x_new, g_ref[...]) if final else x_new


def moe_combine(ys, pos, route, x2, mod, seq, gate_row, final_g=None, tm=256):
    n, d = x2.shape
    tm = _tile(seq, tm)
    per_b = seq // tm
    g = jnp.ones((1, d), F32) if final_g is None else final_g.reshape(1, d)
    return pl.pallas_call(
        functools.partial(_combine_kernel, gate_row=gate_row, final=final_g is not None),
        grid_spec=pltpu.PrefetchScalarGridSpec(
            num_scalar_prefetch=1,
            grid=(n // tm,),
            in_specs=[
                pl.BlockSpec(memory_space=pl.ANY),
                pl.BlockSpec((tm, LANES), lambda i, pos: (i, 0)),
                pl.BlockSpec((tm, d), lambda i, pos: (i, 0)),
                pl.BlockSpec((None, 6, d), lambda i, pos: (i // per_b, 0, 0)),
                pl.BlockSpec((1, d), lambda i, pos: (0, 0)),
            ],
            out_specs=pl.BlockSpec((tm, d), lambda i, pos: (i, 0)),
            scratch_shapes=[pltpu.VMEM((2, TOP_K, tm, d), F32), pltpu.SemaphoreType.DMA((2,))],
        ),
        out_shape=jax.ShapeDtypeStruct((n, d), F32),
        compiler_params=_params(("arbitrary",), disable_bounds_checks=True),
        name="moe_combine",
    )(pos, ys, route, x2, mod, g)


def ffn_moe(h, route, wg, wu, wd, x2, mod, seq, gate_row, final_g=None, tm=1024):
    n, d = x2.shape
    n_experts = wg.shape[0]
    tm = _tile(TOP_K * n, tm)
    pos, plan = _route_plan(route, n_experts, tm)
    rows = TOP_K * n + n_experts * tm
    xs3 = moe_dispatch(h.reshape(n, d // LANES, LANES), pos, rows)
    ys = moe_ffn(xs3.reshape(rows, d), plan, wg, wu, wd, tm)
    return moe_combine(ys, pos, route, x2, mod, seq, gate_row, final_g)


def _rope_tables(seq):
    inv = ROPE_THETA ** (-jnp.arange(0, HEAD_DIM, 2, dtype=F32) / HEAD_DIM)
    ang = jnp.arange(seq, dtype=F32)[:, None] * inv[None, :]
    cos, sin = jnp.cos(ang), jnp.sin(ang)
    return jnp.concatenate([cos, cos], axis=-1), jnp.concatenate([-sin, sin], axis=-1)


def kernel(x, c, attn_norm_g, ada_w, ada_b, w_in, forget_b, moba_out_g, fox_out_g, w_o, ffn_norm_g,
           dense_w_gate, dense_w_up, dense_w_down, router_w, moe_w_gate, moe_w_up, moe_w_down,
           final_norm_g):
    batch, seq, d = x.shape
    depth = attn_norm_g.shape[0]
    moba_w = moba_out_g.shape[1]
    fox_w = fox_out_g.shape[1]
    moba_h = moba_w // HEAD_DIM
    fox_h = fox_w // HEAD_DIM
    qkv_w = 3 * moba_w + 3 * fox_w
    assert moba_w == fox_w
    kinds = (ROPE_SCALED, ROPE, PLAIN, SCALED, PLAIN, PLAIN)
    hb = HEAD_DIM

    cos_t, sin_t = _rope_tables(seq)
    mods = ada_modulation(c, ada_w, ada_b)
    x2 = x.reshape(batch * seq, d)
    out = None

    for l in range(depth):
        mod = mods[l]
        w_qkv = w_in[l][:, :qkv_w].astype(BF16)
        wz = jnp.zeros((d, LANES), BF16).at[:, :fox_h].set(w_in[l][:, qkv_w:].astype(BF16))
        proj, fz = in_proj(x2, attn_norm_g[l], mod, w_qkv, wz, cos_t, sin_t, seq, kinds,
                           tn=moba_w, sh_row=0)
        cum_t = logf_cumsum(fz[:, :fox_h].reshape(batch, seq, fox_h), forget_b[l])
        moba_o = attention(proj, batch, seq, moba_h, q_col=0, k_col=moba_w // hb,
                           v_col=2 * moba_w // hb)
        fc = 3 * moba_w // hb
        fox_o = attention(proj, batch, seq, fox_h, q_col=fc, k_col=fc + fox_w // hb,
                          v_col=fc + 2 * fox_w // hb, cum_t=cum_t)
        i = l // 2
        last = l == depth - 1
        w_o_l = w_o[l].astype(BF16)
        if l % 2 == 0:
            x2, h = out_proj(moba_o, fox_o, moba_out_g[l], fox_out_g[l], w_o_l, x2, mod,
                             ffn_norm_g[l], seq, gate_row=2, sh_row=3)
            x2 = ffn_dense(h, dense_w_gate[i].astype(BF16), dense_w_up[i].astype(BF16),
                           dense_w_down[i].astype(BF16), x2, mod, seq, gate_row=5)
        else:
            x2, h, route = out_proj(moba_o, fox_o, moba_out_g[l], fox_out_g[l], w_o_l, x2, mod,
                                    ffn_norm_g[l], seq, gate_row=2, sh_row=3, router_w=router_w[i])
            x2 = ffn_moe(h, route, moe_w_gate[i], moe_w_up[i], moe_w_down[i], x2, mod, seq,
                         gate_row=5, final_g=final_norm_g if last else None)
            if last:
                out = x2
    if out is None:
        out = final_norm(x2, final_norm_g)
    return out.reshape(batch, seq, d)
```

```python
import functools
import math

import jax
import jax.numpy as jnp
import numpy as np
from jax import lax
from jax.experimental import pallas as pl
from jax.experimental.pallas import tpu as pltpu

F32 = jnp.float32
BF16 = jnp.bfloat16

HEAD_DIM = 128
MOBA_BLOCK = 256
MOBA_TOPK = 3
ROPE_THETA = 10000.0
EPS = 1e-6
TOP_K = 2
LANES = 128
VMEM_LIMIT = 56 * 1024 * 1024

NEG_INF = float("-inf")
NEG_BIG = -(2.0 ** 126)
LOG2E = math.log2(math.e)
Q_SCALE = HEAD_DIM ** -0.5 * LOG2E


def _params(semantics, **kw):
    return pltpu.CompilerParams(dimension_semantics=semantics, vmem_limit_bytes=VMEM_LIMIT, **kw)


def _tile(n, want):
    t = min(n, want)
    assert n % t == 0, (n, want)
    return t


def _ada_kernel(c_ref, w_ref, b_ref, o_ref):
    c = c_ref[...]
    c_act = c * jax.nn.sigmoid(c)
    o_ref[...] = jnp.dot(c_act, w_ref[...], preferred_element_type=F32,
                         precision=lax.Precision.HIGHEST) + b_ref[...]


def ada_modulation(c, ada_w, ada_b):
    n_layers, d, d6 = ada_w.shape
    b = c.shape[0]
    rows = -(-b // 8) * 8
    c_pad = jnp.zeros((rows, d), F32).at[:b].set(c)
    tn = _tile(d6, 1024)
    out = pl.pallas_call(
        _ada_kernel,
        grid=(n_layers, d6 // tn),
        in_specs=[
            pl.BlockSpec((rows, d), lambda l, j: (0, 0)),
            pl.BlockSpec((None, d, tn), lambda l, j: (l, 0, j)),
            pl.BlockSpec((None, 1, tn), lambda l, j: (l, 0, j)),
        ],
        out_specs=pl.BlockSpec((None, rows, tn), lambda l, j: (l, 0, j)),
        out_shape=jax.ShapeDtypeStruct((n_layers, rows, d6), F32),
        compiler_params=_params(("arbitrary", "arbitrary")),
        name="ada_modulation",
    )(c_pad, ada_w, ada_b.reshape(n_layers, 1, d6))
    return out[:, :b].reshape(n_layers, b, 6, d)


def _rms(x, g):
    return x * lax.rsqrt(jnp.mean(x * x, axis=-1, keepdims=True) + EPS) * g


def _top2_route(logits, n_experts):
    lane = lax.broadcasted_iota(jnp.int32, logits.shape, 1).astype(F32)
    lg = jnp.where(lane < n_experts, logits, NEG_INF)
    m1 = jnp.max(lg, axis=-1, keepdims=True)
    i1 = jnp.min(jnp.where(lg == m1, lane, float(LANES)), axis=-1, keepdims=True)
    lg2 = jnp.where(lane == i1, NEG_INF, lg)
    m2 = jnp.max(lg2, axis=-1, keepdims=True)
    i2 = jnp.min(jnp.where(lg2 == m2, lane, float(LANES)), axis=-1, keepdims=True)
    e2 = jnp.exp(m2 - m1)
    w1 = 1.0 / (1.0 + e2)
    w2 = e2 / (1.0 + e2)
    return jnp.where(lane == 0.0, i1, jnp.where(lane == 1.0, i2,
                     jnp.where(lane == 2.0, w1, jnp.where(lane == 3.0, w2, 0.0))))


def _final_norm_kernel(x_ref, g_ref, o_ref):
    o_ref[...] = _rms(x_ref[...], g_ref[...])


def final_norm(x2, g, tm=512):
    n, d = x2.shape
    tm = _tile(n, tm)
    return pl.pallas_call(
        _final_norm_kernel,
        grid=(n // tm,),
        in_specs=[pl.BlockSpec((tm, d), lambda i: (i, 0)),
                  pl.BlockSpec((1, d), lambda i: (0, 0))],
        out_specs=pl.BlockSpec((tm, d), lambda i: (i, 0)),
        out_shape=jax.ShapeDtypeStruct((n, d), F32),
        compiler_params=_params(("arbitrary",)),
        name="final_norm",
    )(x2, g.reshape(1, d))


PLAIN, ROPE, ROPE_SCALED, SCALED = 0, 1, 2, 3


def _in_proj_kernel(x_ref, g_ref, mod_ref, w_ref, wz_ref, cos_ref, sin_ref, o_ref, fz_ref, h_scr,
                    *, kinds, sh_row):
    j = pl.program_id(1)
    tn = w_ref.shape[1]

    @pl.when(j == 0)
    def _():
        y = _rms(x_ref[...], g_ref[...])
        sh = mod_ref[sh_row:sh_row + 1, :]
        sc = mod_ref[sh_row + 1:sh_row + 2, :]
        h_scr[...] = (y * (1.0 + sc) + sh).astype(h_scr.dtype)
        fz_ref[...] = jnp.dot(h_scr[...], wz_ref[...], preferred_element_type=F32)

    def is_kind(kind):
        hits = [jj for jj, k in enumerate(kinds) if k == kind]
        cond = j == hits[0]
        for jj in hits[1:]:
            cond = jnp.logical_or(cond, j == jj)
        return cond

    for kind in sorted(set(kinds)):
        @pl.when(is_kind(kind))
        def _(kind=kind):
            acc = jnp.dot(h_scr[...], w_ref[...], preferred_element_type=F32)
            mult = Q_SCALE if kind in (ROPE_SCALED, SCALED) else 1.0
            if kind in (ROPE, ROPE_SCALED):
                c = cos_ref[...] * mult
                s = sin_ref[...] * mult
                for hh in range(tn // HEAD_DIM):
                    t = acc[:, hh * HEAD_DIM:(hh + 1) * HEAD_DIM]
                    rot = pltpu.roll(t, HEAD_DIM // 2, 1)
                    o_ref[:, hh * HEAD_DIM:(hh + 1) * HEAD_DIM] = (t * c + rot * s).astype(o_ref.dtype)
            elif kind == SCALED:
                o_ref[...] = (acc * mult).astype(o_ref.dtype)
            else:
                o_ref[...] = acc.astype(o_ref.dtype)


def in_proj(x2, g, mod, w_qkv, wz, cos_t, sin_t, seq, kinds, tn, sh_row, tm=1024):
    n, d = x2.shape
    width = w_qkv.shape[1]
    assert width == tn * len(kinds)
    tm = _tile(seq, tm)
    per_b = seq // tm
    return pl.pallas_call(
        functools.partial(_in_proj_kernel, kinds=tuple(kinds), sh_row=sh_row),
        grid=(n // tm, width // tn),
        in_specs=[
            pl.BlockSpec((tm, d), lambda i, j: (i, 0)),
            pl.BlockSpec((1, d), lambda i, j: (0, 0)),
            pl.BlockSpec((None, 6, d), lambda i, j: (i // per_b, 0, 0)),
            pl.BlockSpec((d, tn), lambda i, j: (0, j)),
            pl.BlockSpec((d, LANES), lambda i, j: (0, 0)),
            pl.BlockSpec((tm, HEAD_DIM), lambda i, j: (i % per_b, 0)),
            pl.BlockSpec((tm, HEAD_DIM), lambda i, j: (i % per_b, 0)),
        ],
        out_specs=[pl.BlockSpec((tm, tn), lambda i, j: (i, j)),
                   pl.BlockSpec((tm, LANES), lambda i, j: (i, 0))],
        out_shape=[jax.ShapeDtypeStruct((n, width), BF16),
                   jax.ShapeDtypeStruct((n, LANES), F32)],
        scratch_shapes=[pltpu.VMEM((tm, d), BF16)],
        compiler_params=_params(("arbitrary", "arbitrary")),
        name="in_proj",
    )(x2, g.reshape(1, d), mod, w_qkv, wz, cos_t, sin_t)


def _logf_cumsum_kernel(z_ref, b_ref, o_ref):
    z = z_ref[...] + b_ref[...]
    lf = jnp.minimum(z, 0.0) - jnp.log1p(jnp.exp(-jnp.abs(z)))
    r = lax.broadcasted_iota(jnp.int32, (LANES, LANES), 0)
    c = lax.broadcasted_iota(jnp.int32, (LANES, LANES), 1)
    tri = jnp.where(c <= r, 1.0, 0.0).astype(F32)
    carry = jnp.zeros((1, z.shape[1]), F32)
    for ch in range(z.shape[0] // LANES):
        seg = lf[ch * LANES:(ch + 1) * LANES, :]
        cs = jnp.dot(tri, seg, preferred_element_type=F32, precision=lax.Precision.HIGHEST) + carry
        o_ref[ch * LANES:(ch + 1) * LANES, :] = cs
        carry = cs[LANES - 1:LANES, :]


def logf_cumsum(z, forget_b):
    b, s, h = z.shape
    return pl.pallas_call(
        _logf_cumsum_kernel,
        grid=(b,),
        in_specs=[pl.BlockSpec((None, s, h), lambda i: (i, 0, 0)),
                  pl.BlockSpec((1, h), lambda i: (0, 0))],
        out_specs=pl.BlockSpec((None, s, h), lambda i: (i, 0, 0)),
        out_shape=jax.ShapeDtypeStruct((b, s, h), F32),
        compiler_params=_params(("arbitrary",)),
        name="logf_cumsum",
    )(z, forget_b.reshape(1, h))


def _split3(f):
    hi = f.astype(BF16).astype(F32)
    r = f - hi
    mid = r.astype(BF16).astype(F32)
    lo = (r - mid).astype(BF16).astype(F32)
    return hi, mid, lo


def _pick_lane(x, idx):
    lane = lax.broadcasted_iota(jnp.int32, x.shape, 1)
    return jnp.sum(jnp.where(lane == idx, x, 0.0), axis=-1, keepdims=True)


def _attn_kernel(*refs, mode, tq, tk, seq):
    if mode == "fox":
        q_ref, k_ref, v_ref, ct_ref, o_ref, kaug, vaug, qaug, s_a, s_b, bias_scr, m_scr, acc_scr = refs
    else:
        (q_ref, k_ref, v_ref, o_ref, kaug, vaug, qaug, s_a, s_b, bias_scr, m_scr, acc_scr,
         kmean_scr, kmean3_scr) = refs
    hg = pl.program_id(1)
    i = pl.program_id(2)
    n_moba_blocks = seq // MOBA_BLOCK
    nbp = -(-n_moba_blocks // 16) * 16
    group = q_ref.shape[1] // HEAD_DIM

    def head_cols(g):
        return slice(g * HEAD_DIM, (g + 1) * HEAD_DIM)

    @pl.when(jnp.logical_and(jnp.logical_and(pl.program_id(0) == 0, hg == 0), i == 0))
    def _():
        r2 = lax.broadcasted_iota(jnp.int32, (tk, tk), 0)
        c2 = lax.broadcasted_iota(jnp.int32, (tk, tk), 1)
        ok = c2 <= r2
        if mode == "moba":
            ok = jnp.logical_or(ok, c2 // MOBA_BLOCK != r2 // MOBA_BLOCK)
        bias_scr[...] = jnp.where(ok, 0.0, NEG_INF)

    @pl.when(i == 0)
    def _():
        lane = lax.broadcasted_iota(jnp.int32, (tk, LANES), 1)
        row = lax.broadcasted_iota(jnp.int32, (tk, LANES), 0)

        def build(cidx, carry):
            st = pl.multiple_of(cidx * tk, tk)
            for g in range(group):
                kaug[g, pl.ds(st, tk), 0:HEAD_DIM] = k_ref[pl.ds(st, tk), head_cols(g)]
                vaug[g, pl.ds(st, tk), 0:HEAD_DIM] = v_ref[pl.ds(st, tk), head_cols(g)]
                vaug[g, pl.ds(st, tk), HEAD_DIM:2 * HEAD_DIM] = jnp.ones((tk, HEAD_DIM), BF16)
                if mode == "fox":
                    c = _pick_lane(ct_ref[pl.ds(st, tk), :], hg * group + g) * LOG2E
                    hi, mid, lo = _split3(c)
                    ext = jnp.where(lane < 3, 1.0, jnp.where(lane == 3, -hi, jnp.where(
                        lane == 4, -mid, jnp.where(lane == 5, -lo, 0.0))))
                else:
                    ext = jnp.where(lane == (st + row) // MOBA_BLOCK, 1.0, 0.0)
                kaug[g, pl.ds(st, tk), HEAD_DIM:2 * HEAD_DIM] = ext.astype(BF16)
            return carry

        lax.fori_loop(0, seq // tk, build, 0)
        if mode == "moba":
            kmean_scr[...] = jnp.zeros_like(kmean_scr)
            for g in range(group):
                for b in range(n_moba_blocks):
                    kb = k_ref[b * MOBA_BLOCK:(b + 1) * MOBA_BLOCK, head_cols(g)].astype(F32)
                    kmean_scr[g, b:b + 1, :] = jnp.mean(kb, axis=0, keepdims=True)
                for piece, part in enumerate(_split3(kmean_scr[g, 0:nbp, :])):
                    kmean3_scr[g, piece * nbp:(piece + 1) * nbp, :] = part.astype(BF16)

    lane = lax.broadcasted_iota(jnp.int32, (MOBA_BLOCK, LANES), 1)
    for rb in range(tq // MOBA_BLOCK):
        rows = pl.ds(rb * MOBA_BLOCK, MOBA_BLOCK)
        for g in range(group):
            q = q_ref[rows, head_cols(g)]
            if mode == "fox":
                start = pl.multiple_of(i * tq + rb * MOBA_BLOCK, MOBA_BLOCK)
                f = _pick_lane(ct_ref[pl.ds(start, MOBA_BLOCK), :], hg * group + g) * LOG2E
                hi, mid, lo = _split3(f)
                ext = jnp.where(lane == 0, hi, jnp.where(lane == 1, mid, jnp.where(
                    lane == 2, lo, jnp.where(lane < 6, 1.0, 0.0))))
            else:
                g3 = lax.dot_general(kmean3_scr[g], q, (((1,), (1,)), ((), ())),
                                     preferred_element_type=F32)
                gate = g3[0:nbp] + g3[nbp:2 * nbp] + g3[2 * nbp:3 * nbp]
                own = i * (tq // MOBA_BLOCK) + rb
                blk = lax.broadcasted_iota(jnp.int32, gate.shape, 0)
                blk_f = blk.astype(F32)
                gv = jnp.where(blk < own, gate, NEG_INF)
                keep = blk == own
                for _ in range(min(MOBA_TOPK, n_moba_blocks)):
                    m = jnp.max(gv, axis=0, keepdims=True)
                    idx = jnp.min(jnp.where(gv == m, blk_f, float(LANES)), axis=0, keepdims=True)
                    hit = blk_f == idx
                    keep = jnp.logical_or(keep, jnp.logical_and(hit, m > NEG_INF))
                    gv = jnp.where(hit, NEG_INF, gv)
                ext_t = jnp.where(keep, 0.0, NEG_BIG)
                ext_t = jnp.concatenate(
                    [ext_t, jnp.zeros((LANES - nbp, MOBA_BLOCK), F32)], axis=0)
                ext = ext_t.T
            qaug[g, rows, 0:HEAD_DIM] = q
            qaug[g, rows, HEAD_DIM:2 * HEAD_DIM] = ext.astype(BF16)

    def scores(g, j):
        st = pl.multiple_of(j * tk, tk)
        return lax.dot_general(qaug[g], kaug[g, pl.ds(st, tk), :], (((1,), (1,)), ((), ())),
                               preferred_element_type=F32)

    def update(g, s, j, rows=slice(0, tq), causal=False):
        st = pl.multiple_of(j * tk, tk)
        if causal:
            s = s + bias_scr[...]
        m_old = m_scr[g, rows]
        m_new = jnp.maximum(m_old, jnp.max(s, axis=-1, keepdims=True))
        p = jnp.exp2(s - m_new).astype(BF16)
        pv = jnp.dot(p, vaug[g, pl.ds(st, tk), :], preferred_element_type=F32)
        acc_scr[g, rows] = jnp.exp2(m_old - m_new) * acc_scr[g, rows] + pv
        m_scr[g, rows] = m_new

    heads = range(group)
    m_scr[...] = jnp.full(m_scr.shape, NEG_INF, F32)
    acc_scr[...] = jnp.zeros(acc_scr.shape, F32)
    for g in heads:
        s_a[g] = scores(g, 0)

    def body(jj, carry):
        for g in heads:
            s_b[g] = scores(g, 2 * jj + 1)
        for g in heads:
            update(g, s_a[g], 2 * jj)
        for g in heads:
            s_a[g] = scores(g, 2 * jj + 2)
        for g in heads:
            update(g, s_b[g], 2 * jj + 1)
        return carry

    lax.fori_loop(0, i, body, 0)
    upper, lower = slice(0, tk), slice(tk, tq)
    for g in heads:
        st = pl.multiple_of((2 * i + 1) * tk, tk)
        s_b[g, upper] = lax.dot_general(qaug[g, lower], kaug[g, pl.ds(st, tk), :],
                                        (((1,), (1,)), ((), ())), preferred_element_type=F32)
    for g in heads:
        update(g, s_a[g, upper], 2 * i, upper, causal=True)
        update(g, s_a[g, lower], 2 * i, lower)
    for g in heads:
        update(g, s_b[g, upper], 2 * i + 1, lower, causal=True)
    for g in heads:
        o_ref[:, head_cols(g)] = (acc_scr[g, :, 0:HEAD_DIM]
                                  / acc_scr[g, :, HEAD_DIM:2 * HEAD_DIM]).astype(o_ref.dtype)


def attention(proj, batch, seq, heads, q_col, k_col, v_col, cum_t=None, tq=1024, group=2):
    n = proj.shape[0]
    tq = _tile(seq, tq)
    tk = tq // 2
    nq = seq // tq
    mode = "moba" if cum_t is None else "fox"
    assert tk % MOBA_BLOCK == 0 and seq // MOBA_BLOCK <= LANES
    gw = group * HEAD_DIM
    assert heads % group == 0 and q_col % group == 0 and k_col % group == 0 and v_col % group == 0
    in_specs = [
        pl.BlockSpec((tq, gw), lambda b, h, i: (b * nq + i, q_col // group + h)),
        pl.BlockSpec((seq, gw), lambda b, h, i: (b, k_col // group + h)),
        pl.BlockSpec((seq, gw), lambda b, h, i: (b, v_col // group + h)),
    ]
    args = [proj, proj, proj]
    wide = 2 * HEAD_DIM
    scratch = [pltpu.VMEM((group, seq, wide), BF16), pltpu.VMEM((group, seq, wide), BF16),
               pltpu.VMEM((group, tq, wide), BF16),
               pltpu.VMEM((group, tq, tk), F32), pltpu.VMEM((group, tq, tk), F32),
               pltpu.VMEM((tk, tk), F32),
               pltpu.VMEM((group, tq, 1), F32), pltpu.VMEM((group, tq, wide), F32)]
    if mode == "fox":
        in_specs.append(pl.BlockSpec((None, seq, heads), lambda b, h, i: (b, 0, 0)))
        args.append(cum_t)
    else:
        scratch += [pltpu.VMEM((group, LANES, HEAD_DIM), F32),
                    pltpu.VMEM((group, 3 * (-(-(seq // MOBA_BLOCK) // 16) * 16), HEAD_DIM), BF16)]
    return pl.pallas_call(
        functools.partial(_attn_kernel, mode=mode, tq=tq, tk=tk, seq=seq),
        grid=(batch, heads // group, nq),
        in_specs=in_specs,
        out_specs=pl.BlockSpec((tq, gw), lambda b, h, i: (b * nq + i, h)),
        out_shape=jax.ShapeDtypeStruct((n, heads * HEAD_DIM), BF16),
        scratch_shapes=scratch,
        compiler_params=_params(("arbitrary", "arbitrary", "arbitrary")),
        name=mode + "_attention",
    )(*args)


def _out_proj_kernel(*refs, gate_row, sh_row, n_experts):
    if n_experts:
        (mo_ref, fo_ref, g1_ref, g2_ref, w_ref, x_ref, mod_ref, gf_ref, wr_ref,
         o_ref, h_ref, route_ref) = refs
    else:
        mo_ref, fo_ref, g1_ref, g2_ref, w_ref, x_ref, mod_ref, gf_ref, o_ref, h_ref = refs
    half = mo_ref.shape[1]
    m1 = _rms(mo_ref[...].astype(F32), g1_ref[...]).astype(BF16)
    m2 = _rms(fo_ref[...].astype(F32), g2_ref[...]).astype(BF16)
    acc = jnp.dot(m1, w_ref[0:half, :], preferred_element_type=F32)
    acc = acc + jnp.dot(m2, w_ref[half:2 * half, :], preferred_element_type=F32)
    x_new = x_ref[...] + mod_ref[gate_row:gate_row + 1, :] * acc
    o_ref[...] = x_new
    sh = mod_ref[sh_row:sh_row + 1, :]
    sc = mod_ref[sh_row + 1:sh_row + 2, :]
    h = _rms(x_new, gf_ref[...]) * (1.0 + sc) + sh
    h_hi = h.astype(BF16)
    h_ref[...] = h_hi
    if n_experts:
        h_lo = (h - h_hi.astype(F32)).astype(BF16)
        l_hi = jnp.dot(h_hi, wr_ref[...], preferred_element_type=F32)
        l_lo = jnp.dot(h_lo, wr_ref[:, 0:LANES], preferred_element_type=F32)
        logits = l_hi[:, 0:LANES] + l_hi[:, LANES:2 * LANES] + l_lo
        route_ref[...] = _top2_route(logits, n_experts)


def out_proj(moba_o, fox_o, g1, g2, w_o, x2, mod, ffn_g, seq, gate_row, sh_row, router_w=None, tm=512):
    n, d = x2.shape
    half = moba_o.shape[1]
    tm = _tile(seq, tm)
    per_b = seq // tm
    n_experts = 0 if router_w is None else router_w.shape[1]
    in_specs = [
        pl.BlockSpec((tm, half), lambda i: (i, 0)),
        pl.BlockSpec((tm, half), lambda i: (i, 0)),
        pl.BlockSpec((1, half), lambda i: (0, 0)),
        pl.BlockSpec((1, half), lambda i: (0, 0)),
        pl.BlockSpec((2 * half, d), lambda i: (0, 0)),
        pl.BlockSpec((tm, d), lambda i: (i, 0)),
        pl.BlockSpec((None, 6, d), lambda i: (i // per_b, 0, 0)),
        pl.BlockSpec((1, d), lambda i: (0, 0)),
    ]
    args = [moba_o, fox_o, g1.reshape(1, half), g2.reshape(1, half), w_o, x2, mod, ffn_g.reshape(1, d)]
    out_specs = [pl.BlockSpec((tm, d), lambda i: (i, 0)), pl.BlockSpec((tm, d), lambda i: (i, 0))]
    out_shape = [jax.ShapeDtypeStruct((n, d), F32), jax.ShapeDtypeStruct((n, d), BF16)]
    if n_experts:
        in_specs.append(pl.BlockSpec((d, 2 * LANES), lambda i: (0, 0)))
        w_hi = router_w.astype(BF16)
        w_lo = (router_w - w_hi.astype(F32)).astype(BF16)
        args.append(jnp.zeros((d, 2 * LANES), BF16).at[:, :n_experts].set(w_hi)
                    .at[:, LANES:LANES + n_experts].set(w_lo))
        out_specs.append(pl.BlockSpec((tm, LANES), lambda i: (i, 0)))
        out_shape.append(jax.ShapeDtypeStruct((n, LANES), F32))
    return pl.pallas_call(
        functools.partial(_out_proj_kernel, gate_row=gate_row, sh_row=sh_row, n_experts=n_experts),
        grid=(n // tm,),
        in_specs=in_specs,
        out_specs=out_specs,
        out_shape=out_shape,
        compiler_params=_params(("arbitrary",)),
        name="out_proj",
    )(*args)


def _swiglu_tile(h, wg, wu):
    a = jnp.dot(h, wg, preferred_element_type=F32)
    b = jnp.dot(h, wu, preferred_element_type=F32)
    return a * jax.nn.sigmoid(a) * b


def _ffn_kernel(h_ref, wg_ref, wu_ref, wd_ref, x_ref, mod_ref, o_ref, *, gate_row):
    f = pl.program_id(1)

    @pl.when(f == 0)
    def _():
        o_ref[...] = jnp.zeros(o_ref.shape, o_ref.dtype)

    t = _swiglu_tile(h_ref[...], wg_ref[...], wu_ref[...]).astype(BF16)
    o_ref[...] += jnp.dot(t, wd_ref[...], preferred_element_type=F32)

    @pl.when(f == pl.num_programs(1) - 1)
    def _():
        o_ref[...] = x_ref[...] + mod_ref[gate_row:gate_row + 1, :] * o_ref[...]


def ffn_dense(h, wg, wu, wd, x2, mod, seq, gate_row, tm=1024, tf=512):
    n, d = x2.shape
    ff = wg.shape[1]
    tm = _tile(seq, tm)
    tf = _tile(ff, tf)
    per_b = seq // tm
    return pl.pallas_call(
        functools.partial(_ffn_kernel, gate_row=gate_row),
        grid=(n // tm, ff // tf),
        in_specs=[
            pl.BlockSpec((tm, d), lambda i, f: (i, 0)),
            pl.BlockSpec((d, tf), lambda i, f: (0, f)),
            pl.BlockSpec((d, tf), lambda i, f: (0, f)),
            pl.BlockSpec((tf, d), lambda i, f: (f, 0)),
            pl.BlockSpec((tm, d), lambda i, f: (i, 0), pipeline_mode=pl.Buffered(1)),
            pl.BlockSpec((None, 6, d), lambda i, f: (i // per_b, 0, 0)),
        ],
        out_specs=pl.BlockSpec((tm, d), lambda i, f: (i, 0)),
        out_shape=jax.ShapeDtypeStruct((n, d), F32),
        compiler_params=_params(("arbitrary", "arbitrary")),
        name="ffn_dense",
    )(h, wg, wu, wd, x2, mod)


def _route_plan(route, n_experts, tm):
    n = route.shape[0]
    pairs = TOP_K * n
    e_flat = route[:, :TOP_K].astype(jnp.int32).reshape(pairs)
    onehot = (e_flat[:, None] == jnp.arange(n_experts, dtype=jnp.int32)[None, :]).astype(jnp.int32)
    csum = jnp.cumsum(onehot, axis=0)
    counts = csum[-1]
    padded = (counts + tm - 1) // tm * tm
    pend = jnp.cumsum(padded)
    pstart = pend - padded
    rank = jnp.sum(csum * onehot, axis=1) - 1
    pos = (jnp.sum(pstart[None, :] * onehot, axis=1) + rank).astype(jnp.int32)

    n_tiles = pairs // tm + n_experts
    first_row = jnp.arange(n_tiles, dtype=jnp.int32) * tm
    tile_expert = jnp.minimum(jnp.sum((pend[None, :] <= first_row[:, None]).astype(jnp.int32), axis=1),
                              n_experts - 1)
    valid = first_row < pend[-1]
    return pos, (tile_expert.astype(jnp.int32), valid.astype(jnp.int32),
                 (pstart + counts).astype(jnp.int32))


def _dispatch_kernel(pos_ref, h_ref, init_ref, xs_ref, sem):
    del init_ref
    i = pl.program_id(0)
    tm = h_ref.shape[0]

    def issue(r, carry):
        for k in range(TOP_K):
            dst = pos_ref[(i * tm + r) * TOP_K + k]
            pltpu.make_async_copy(h_ref.at[r], xs_ref.at[dst], sem).start()
        return carry

    lax.fori_loop(0, tm, issue, 0)
    for k in range(TOP_K):
        pltpu.make_async_copy(h_ref, xs_ref.at[pl.ds(0, tm)], sem).wait()


def moe_dispatch(h3, pos, rows, tm=1024):
    n, sub, lanes = h3.shape
    tm = _tile(n, tm)
    return pl.pallas_call(
        _dispatch_kernel,
        grid_spec=pltpu.PrefetchScalarGridSpec(
            num_scalar_prefetch=1,
            grid=(n // tm,),
            in_specs=[pl.BlockSpec((tm, sub, lanes), lambda i, pos: (i, 0, 0)),
                      pl.BlockSpec(memory_space=pl.ANY)],
            out_specs=pl.BlockSpec(memory_space=pl.ANY),
            scratch_shapes=[pltpu.SemaphoreType.DMA(())],
        ),
        out_shape=jax.ShapeDtypeStruct((rows, sub, lanes), h3.dtype),
        input_output_aliases={2: 0},
        compiler_params=_params(("arbitrary",), disable_bounds_checks=True),
        name="moe_dispatch",
    )(pos, h3, jnp.zeros((rows, sub, lanes), h3.dtype))


def _moe_ffn_kernel(exp_ref, valid_ref, rowend_ref, xs_ref, wg_ref, wu_ref, wd_ref, o_ref, *, sub):
    t = pl.program_id(0)
    f = pl.program_id(1)
    tm = xs_ref.shape[0]

    @pl.when(valid_ref[t] == 1)
    def _():
        rows = t * tm + lax.broadcasted_iota(jnp.int32, (tm, 1), 0)
        real = rows < rowend_ref[exp_ref[t]]

        @pl.when(f == 0)
        def _():
            o_ref[...] = jnp.zeros(o_ref.shape, o_ref.dtype)

        for c in range(wg_ref.shape[1] // sub):
            cols = slice(c * sub, (c + 1) * sub)
            a = _swiglu_tile(xs_ref[...], wg_ref[:, cols].astype(BF16), wu_ref[:, cols].astype(BF16))
            a = jnp.where(real, a, 0.0).astype(BF16)
            o_ref[...] += jnp.dot(a, wd_ref[cols, :].astype(BF16), preferred_element_type=F32)

    @pl.when(jnp.logical_and(valid_ref[t] == 0, f == 0))
    def _():
        o_ref[...] = jnp.zeros(o_ref.shape, o_ref.dtype)


def moe_ffn(xs, plan, wg, wu, wd, tm, tf=512, sub=256):
    rows, d = xs.shape
    n_experts, _, ff = wg.shape
    tf = _tile(ff, tf)
    sub = _tile(tf, sub)
    nf = ff // tf

    def f_idx(t, f, va):
        return jnp.where(va[t] == 1, f, nf - 1)

    return pl.pallas_call(
        functools.partial(_moe_ffn_kernel, sub=sub),
        grid_spec=pltpu.PrefetchScalarGridSpec(
            num_scalar_prefetch=3,
            grid=(rows // tm, nf),
            in_specs=[
                pl.BlockSpec((tm, d), lambda t, f, ex, va, re: (t, 0)),
                pl.BlockSpec((None, d, tf), lambda t, f, ex, va, re: (ex[t], 0, f_idx(t, f, va))),
                pl.BlockSpec((None, d, tf), lambda t, f, ex, va, re: (ex[t], 0, f_idx(t, f, va))),
                pl.BlockSpec((None, tf, d), lambda t, f, ex, va, re: (ex[t], f_idx(t, f, va), 0)),
            ],
            out_specs=pl.BlockSpec((tm, d), lambda t, f, ex, va, re: (t, 0)),
        ),
        out_shape=jax.ShapeDtypeStruct((rows, d), F32),
        compiler_params=_params(("arbitrary", "arbitrary")),
        name="moe_ffn",
    )(*plan, xs, wg, wu, wd)


def _combine_kernel(pos_ref, ys_ref, route_ref, x_ref, mod_ref, g_ref, o_ref, buf, sem,
                    *, gate_row, final):
    i = pl.program_id(0)
    tm = x_ref.shape[0]
    slot = i % 2

    def gather(step, into):
        def issue(r, carry):
            for k in range(TOP_K):
                src = pos_ref[(step * tm + r) * TOP_K + k]
                pltpu.make_async_copy(ys_ref.at[pl.ds(src, 1), :], buf.at[into, k, pl.ds(r, 1), :],
                                      sem.at[into]).start()
            return carry

        lax.fori_loop(0, tm, issue, 0, unroll=2)

    @pl.when(i == 0)
    def _():
        gather(0, 0)

    more = i + 1 < pl.num_programs(0)
    for parity in range(2):
        @pl.when(jnp.logical_and(more, slot == parity))
        def _(parity=parity):
            gather(i + 1, 1 - parity)

    for k in range(TOP_K):
        pltpu.make_async_copy(ys_ref.at[pl.ds(0, tm), :], buf.at[slot, k], sem.at[slot]).wait()
    route = route_ref[...]
    y = buf[slot, 0] * route[:, TOP_K:TOP_K + 1]
    for k in range(1, TOP_K):
        y = y + buf[slot, k] * route[:, TOP_K + k:TOP_K + k + 1]
    x_new = x_ref[...] + mod_ref[gate_row:gate_row + 1, :] * y
    o_ref[...] = _rms(x_new, g_ref[...]) if final else x_new


def moe_combine(ys, pos, route, x2, mod, seq, gate_row, final_g=None, tm=256):
    n, d = x2.shape
    tm = _tile(seq, tm)
    per_b = seq // tm
    g = jnp.ones((1, d), F32) if final_g is None else final_g.reshape(1, d)
    return pl.pallas_call(
        functools.partial(_combine_kernel, gate_row=gate_row, final=final_g is not None),
        grid_spec=pltpu.PrefetchScalarGridSpec(
            num_scalar_prefetch=1,
            grid=(n // tm,),
            in_specs=[
                pl.BlockSpec(memory_space=pl.ANY),
                pl.BlockSpec((tm, LANES), lambda i, pos: (i, 0)),
                pl.BlockSpec((tm, d), lambda i, pos: (i, 0)),
                pl.BlockSpec((None, 6, d), lambda i, pos: (i // per_b, 0, 0)),
                pl.BlockSpec((1, d), lambda i, pos: (0, 0)),
            ],
            out_specs=pl.BlockSpec((tm, d), lambda i, pos: (i, 0)),
            scratch_shapes=[pltpu.VMEM((2, TOP_K, tm, d), F32), pltpu.SemaphoreType.DMA((2,))],
        ),
        out_shape=jax.ShapeDtypeStruct((n, d), F32),
        compiler_params=_params(("arbitrary",), disable_bounds_checks=True),
        name="moe_combine",
    )(pos, ys, route, x2, mod, g)


def ffn_moe(h, route, wg, wu, wd, x2, mod, seq, gate_row, final_g=None, tm=1024):
    n, d = x2.shape
    n_experts = wg.shape[0]
    tm = _tile(TOP_K * n, tm)
    pos, plan = _route_plan(route, n_experts, tm)
    rows = TOP_K * n + n_experts * tm
    xs3 = moe_dispatch(h.reshape(n, d // LANES, LANES), pos, rows)
    ys = moe_ffn(xs3.reshape(rows, d), plan, wg, wu, wd, tm)
    return moe_combine(ys, pos, route, x2, mod, seq, gate_row, final_g)


def _rope_tables(seq):
    f32 = np.float32
    inv = np.power(f32(ROPE_THETA), -np.arange(0, HEAD_DIM, 2, dtype=f32) / f32(HEAD_DIM)).astype(f32)
    ang = np.arange(seq, dtype=f32)[:, None] * inv[None, :]
    cos, sin = np.cos(ang).astype(f32), np.sin(ang).astype(f32)
    return (jnp.asarray(np.concatenate([cos, cos], axis=-1)),
            jnp.asarray(np.concatenate([-sin, sin], axis=-1)))


def kernel(x, c, attn_norm_g, ada_w, ada_b, w_in, forget_b, moba_out_g, fox_out_g, w_o, ffn_norm_g,
           dense_w_gate, dense_w_up, dense_w_down, router_w, moe_w_gate, moe_w_up, moe_w_down,
           final_norm_g):
    batch, seq, d = x.shape
    depth = attn_norm_g.shape[0]
    moba_w = moba_out_g.shape[1]
    fox_w = fox_out_g.shape[1]
    moba_h = moba_w // HEAD_DIM
    fox_h = fox_w // HEAD_DIM
    qkv_w = 3 * moba_w + 3 * fox_w
    assert moba_w == fox_w
    kinds = (ROPE_SCALED, ROPE, PLAIN, SCALED, PLAIN, PLAIN)
    hb = HEAD_DIM

    cos_t, sin_t = _rope_tables(seq)
    mods = ada_modulation(c, ada_w, ada_b)
    x2 = x.reshape(batch * seq, d)
    out = None

    for l in range(depth):
        mod = mods[l]
        w_qkv = w_in[l][:, :qkv_w].astype(BF16)
        wz = jnp.zeros((d, LANES), BF16).at[:, :fox_h].set(w_in[l][:, qkv_w:].astype(BF16))
        proj, fz = in_proj(x2, attn_norm_g[l], mod, w_qkv, wz, cos_t, sin_t, seq, kinds,
                           tn=moba_w, sh_row=0)
        cum_t = logf_cumsum(fz[:, :fox_h].reshape(batch, seq, fox_h), forget_b[l])
        moba_o = attention(proj, batch, seq, moba_h, q_col=0, k_col=moba_w // hb,
                           v_col=2 * moba_w // hb)
        fc = 3 * moba_w // hb
        fox_o = attention(proj, batch, seq, fox_h, q_col=fc, k_col=fc + fox_w // hb,
                          v_col=fc + 2 * fox_w // hb, cum_t=cum_t)
        i = l // 2
        last = l == depth - 1
        w_o_l = w_o[l].astype(BF16)
        if l % 2 == 0:
            x2, h = out_proj(moba_o, fox_o, moba_out_g[l], fox_out_g[l], w_o_l, x2, mod,
                             ffn_norm_g[l], seq, gate_row=2, sh_row=3)
            x2 = ffn_dense(h, dense_w_gate[i].astype(BF16), dense_w_up[i].astype(BF16),
                           dense_w_down[i].astype(BF16), x2, mod, seq, gate_row=5)
        else:
            x2, h, route = out_proj(moba_o, fox_o, moba_out_g[l], fox_out_g[l], w_o_l, x2, mod,
                                    ffn_norm_g[l], seq, gate_row=2, sh_row=3, router_w=router_w[i])
            x2 = ffn_moe(h, route, moe_w_gate[i], moe_w_up[i], moe_w_down[i], x2, mod, seq,
                         gate_row=5, final_g=final_norm_g if last else None)
            if last:
                out = x2
    if out is None:
        out = final_norm(x2, final_norm_g)
    return out.reshape(batch, seq, d)
```

```python
import functools
import math

import jax
import jax.numpy as jnp
import numpy as np
from jax import lax
from jax.experimental import pallas as pl
from jax.experimental.pallas import tpu as pltpu

F32 = jnp.float32
BF16 = jnp.bfloat16

HEAD_DIM = 128
MOBA_BLOCK = 256
MOBA_TOPK = 3
ROPE_THETA = 10000.0
EPS = 1e-6
TOP_K = 2
LANES = 128
VMEM_LIMIT = 60 * 1024 * 1024

NEG_INF = float("-inf")
NEG_BIG = -(2.0 ** 126)
LOG2E = math.log2(math.e)
Q_SCALE = HEAD_DIM ** -0.5 * LOG2E


def _params(semantics, **kw):
    return pltpu.CompilerParams(dimension_semantics=semantics, vmem_limit_bytes=VMEM_LIMIT, **kw)


def _tile(n, want):
    t = min(n, want)
    assert n % t == 0, (n, want)
    return t


def _ada_kernel(c_ref, w_ref, b_ref, o_ref):
    c = c_ref[...]
    c_act = c * jax.nn.sigmoid(c)
    o_ref[...] = jnp.dot(c_act, w_ref[...], preferred_element_type=F32,
                         precision=lax.Precision.HIGHEST) + b_ref[...]


def ada_modulation(c, ada_w, ada_b):
    n_layers, d, d6 = ada_w.shape
    b = c.shape[0]
    rows = -(-b // 8) * 8
    c_pad = jnp.zeros((rows, d), F32).at[:b].set(c)
    tn = _tile(d6, 1024)
    out = pl.pallas_call(
        _ada_kernel,
        grid=(n_layers, d6 // tn),
        in_specs=[
            pl.BlockSpec((rows, d), lambda l, j: (0, 0)),
            pl.BlockSpec((None, d, tn), lambda l, j: (l, 0, j)),
            pl.BlockSpec((None, 1, tn), lambda l, j: (l, 0, j)),
        ],
        out_specs=pl.BlockSpec((None, rows, tn), lambda l, j: (l, 0, j)),
        out_shape=jax.ShapeDtypeStruct((n_layers, rows, d6), F32),
        compiler_params=_params(("arbitrary", "arbitrary")),
        name="ada_modulation",
    )(c_pad, ada_w, ada_b.reshape(n_layers, 1, d6))
    return out[:, :b].reshape(n_layers, b, 6, d)


def _rms(x, g):
    return x * lax.rsqrt(jnp.mean(x * x, axis=-1, keepdims=True) + EPS) * g


def _top2_route(logits, n_experts):
    lane = lax.broadcasted_iota(jnp.int32, logits.shape, 1).astype(F32)
    lg = jnp.where(lane < n_experts, logits, NEG_INF)
    m1 = jnp.max(lg, axis=-1, keepdims=True)
    i1 = jnp.min(jnp.where(lg == m1, lane, float(LANES)), axis=-1, keepdims=True)
    lg2 = jnp.where(lane == i1, NEG_INF, lg)
    m2 = jnp.max(lg2, axis=-1, keepdims=True)
    i2 = jnp.min(jnp.where(lg2 == m2, lane, float(LANES)), axis=-1, keepdims=True)
    e2 = jnp.exp(m2 - m1)
    w1 = 1.0 / (1.0 + e2)
    w2 = e2 / (1.0 + e2)
    return jnp.where(lane == 0.0, i1, jnp.where(lane == 1.0, i2,
                     jnp.where(lane == 2.0, w1, jnp.where(lane == 3.0, w2, 0.0))))


def _final_norm_kernel(x_ref, g_ref, o_ref):
    o_ref[...] = _rms(x_ref[...], g_ref[...])


def final_norm(x2, g, tm=512):
    n, d = x2.shape
    tm = _tile(n, tm)
    return pl.pallas_call(
        _final_norm_kernel,
        grid=(n // tm,),
        in_specs=[pl.BlockSpec((tm, d), lambda i: (i, 0)),
                  pl.BlockSpec((1, d), lambda i: (0, 0))],
        out_specs=pl.BlockSpec((tm, d), lambda i: (i, 0)),
        out_shape=jax.ShapeDtypeStruct((n, d), F32),
        compiler_params=_params(("arbitrary",)),
        name="final_norm",
    )(x2, g.reshape(1, d))


PLAIN, ROPE, ROPE_SCALED, SCALED = 0, 1, 2, 3


def _in_proj_kernel(x_ref, g_ref, mod_ref, w_ref, wz_ref, cos_ref, sin_ref, o_ref, fz_ref, h_scr,
                    *, kinds, sh_row):
    j = pl.program_id(1)
    tn = w_ref.shape[1]

    @pl.when(j == 0)
    def _():
        y = _rms(x_ref[...], g_ref[...])
        sh = mod_ref[sh_row:sh_row + 1, :]
        sc = mod_ref[sh_row + 1:sh_row + 2, :]
        h_scr[...] = (y * (1.0 + sc) + sh).astype(h_scr.dtype)
        fz_ref[...] = jnp.dot(h_scr[...], wz_ref[...], preferred_element_type=F32)

    def is_kind(kind):
        hits = [jj for jj, k in enumerate(kinds) if k == kind]
        cond = j == hits[0]
        for jj in hits[1:]:
            cond = jnp.logical_or(cond, j == jj)
        return cond

    for kind in sorted(set(kinds)):
        @pl.when(is_kind(kind))
        def _(kind=kind):
            acc = jnp.dot(h_scr[...], w_ref[...], preferred_element_type=F32)
            mult = Q_SCALE if kind in (ROPE_SCALED, SCALED) else 1.0
            if kind in (ROPE, ROPE_SCALED):
                c = cos_ref[...] * mult
                s = sin_ref[...] * mult
                for hh in range(tn // HEAD_DIM):
                    t = acc[:, hh * HEAD_DIM:(hh + 1) * HEAD_DIM]
                    rot = pltpu.roll(t, HEAD_DIM // 2, 1)
                    o_ref[:, hh * HEAD_DIM:(hh + 1) * HEAD_DIM] = (t * c + rot * s).astype(o_ref.dtype)
            elif kind == SCALED:
                o_ref[...] = (acc * mult).astype(o_ref.dtype)
            else:
                o_ref[...] = acc.astype(o_ref.dtype)


def in_proj(x2, g, mod, w_qkv, wz, cos_t, sin_t, seq, kinds, tn, sh_row, tm=1024):
    n, d = x2.shape
    width = w_qkv.shape[1]
    assert width == tn * len(kinds)
    tm = _tile(seq, tm)
    per_b = seq // tm
    return pl.pallas_call(
        functools.partial(_in_proj_kernel, kinds=tuple(kinds), sh_row=sh_row),
        grid=(n // tm, width // tn),
        in_specs=[
            pl.BlockSpec((tm, d), lambda i, j: (i, 0)),
            pl.BlockSpec((1, d), lambda i, j: (0, 0)),
            pl.BlockSpec((None, 6, d), lambda i, j: (i // per_b, 0, 0)),
            pl.BlockSpec((d, tn), lambda i, j: (0, j)),
            pl.BlockSpec((d, LANES), lambda i, j: (0, 0)),
            pl.BlockSpec((tm, HEAD_DIM), lambda i, j: (i % per_b, 0)),
            pl.BlockSpec((tm, HEAD_DIM), lambda i, j: (i % per_b, 0)),
        ],
        out_specs=[pl.BlockSpec((tm, tn), lambda i, j: (i, j)),
                   pl.BlockSpec((tm, LANES), lambda i, j: (i, 0))],
        out_shape=[jax.ShapeDtypeStruct((n, width), BF16),
                   jax.ShapeDtypeStruct((n, LANES), F32)],
        scratch_shapes=[pltpu.VMEM((tm, d), BF16)],
        compiler_params=_params(("arbitrary", "arbitrary")),
        name="in_proj",
    )(x2, g.reshape(1, d), mod, w_qkv, wz, cos_t, sin_t)


def _logf_cumsum_kernel(z_ref, b_ref, o_ref):
    z = z_ref[...] + b_ref[...]
    lf = jnp.minimum(z, 0.0) - jnp.log1p(jnp.exp(-jnp.abs(z)))
    r = lax.broadcasted_iota(jnp.int32, (LANES, LANES), 0)
    c = lax.broadcasted_iota(jnp.int32, (LANES, LANES), 1)
    tri = jnp.where(c <= r, 1.0, 0.0).astype(F32)
    carry = jnp.zeros((1, z.shape[1]), F32)
    for ch in range(z.shape[0] // LANES):
        seg = lf[ch * LANES:(ch + 1) * LANES, :]
        cs = jnp.dot(tri, seg, preferred_element_type=F32, precision=lax.Precision.HIGHEST) + carry
        o_ref[ch * LANES:(ch + 1) * LANES, :] = cs
        carry = cs[LANES - 1:LANES, :]


def logf_cumsum(z, forget_b):
    b, s, h = z.shape
    return pl.pallas_call(
        _logf_cumsum_kernel,
        grid=(b,),
        in_specs=[pl.BlockSpec((None, s, h), lambda i: (i, 0, 0)),
                  pl.BlockSpec((1, h), lambda i: (0, 0))],
        out_specs=pl.BlockSpec((None, s, h), lambda i: (i, 0, 0)),
        out_shape=jax.ShapeDtypeStruct((b, s, h), F32),
        compiler_params=_params(("arbitrary",)),
        name="logf_cumsum",
    )(z, forget_b.reshape(1, h))


def _split3(f):
    hi = f.astype(BF16).astype(F32)
    r = f - hi
    mid = r.astype(BF16).astype(F32)
    lo = (r - mid).astype(BF16).astype(F32)
    return hi, mid, lo


def _fox_ext(cw, head, keys):
    n = cw.shape[1]
    mine = lax.broadcasted_iota(jnp.int32, cw.shape, 0) == head
    c = jnp.sum(jnp.where(mine, cw, 0.0), axis=0, keepdims=True) * LOG2E
    hi, mid, lo = _split3(c)
    r = lax.broadcasted_iota(jnp.int32, (8, n), 0)
    if keys:
        e8 = jnp.where(r < 3, 1.0, jnp.where(r == 3, -hi, jnp.where(
            r == 4, -mid, jnp.where(r == 5, -lo, 0.0))))
    else:
        e8 = jnp.where(r == 0, hi, jnp.where(r == 1, mid, jnp.where(
            r == 2, lo, jnp.where(r < 6, 1.0, 0.0))))
    return jnp.concatenate([e8, jnp.zeros((LANES - 8, n), F32)], axis=0).T


def _attn_kernel(*refs, mode, tq, tk, seq):
    if mode == "fox":
        q_ref, k_ref, v_ref, cw_ref, o_ref, kaug, vaug, qaug, s_a, s_b, bias_scr, m_scr, acc_scr = refs
    else:
        (q_ref, k_ref, v_ref, o_ref, kaug, vaug, qaug, s_a, s_b, bias_scr, m_scr, acc_scr,
         kmean_scr, kmean3_scr) = refs
    hg = pl.program_id(1)
    i = pl.program_id(2)
    n_moba_blocks = seq // MOBA_BLOCK
    nbp = -(-n_moba_blocks // 16) * 16
    group = q_ref.shape[1] // HEAD_DIM

    def head_cols(g):
        return slice(g * HEAD_DIM, (g + 1) * HEAD_DIM)

    @pl.when(jnp.logical_and(jnp.logical_and(pl.program_id(0) == 0, hg == 0), i == 0))
    def _():
        r2 = lax.broadcasted_iota(jnp.int32, (tk, tk), 0)
        c2 = lax.broadcasted_iota(jnp.int32, (tk, tk), 1)
        ok = c2 <= r2
        if mode == "moba":
            ok = jnp.logical_or(ok, c2 // MOBA_BLOCK != r2 // MOBA_BLOCK)
        bias_scr[...] = jnp.where(ok, 0.0, NEG_INF)

    @pl.when(i == 0)
    def _():
        lane = lax.broadcasted_iota(jnp.int32, (tk, LANES), 1)
        row = lax.broadcasted_iota(jnp.int32, (tk, LANES), 0)

        def build(cidx, carry):
            st = pl.multiple_of(cidx * tk, tk)
            for g in range(group):
                kaug[g, pl.ds(st, tk), 0:HEAD_DIM] = k_ref[pl.ds(st, tk), head_cols(g)]
                vaug[g, pl.ds(st, tk), 0:HEAD_DIM] = v_ref[pl.ds(st, tk), head_cols(g)]
                vaug[g, pl.ds(st, tk), HEAD_DIM:2 * HEAD_DIM] = jnp.ones((tk, HEAD_DIM), BF16)
                if mode == "fox":
                    ext = _fox_ext(cw_ref[:, pl.ds(st, tk)], hg * group + g, keys=True)
                else:
                    ext = jnp.where(lane == (st + row) // MOBA_BLOCK, 1.0, 0.0)
                kaug[g, pl.ds(st, tk), HEAD_DIM:2 * HEAD_DIM] = ext.astype(BF16)
            return carry

        lax.fori_loop(0, seq // tk, build, 0)
        if mode == "moba":
            kmean_scr[...] = jnp.zeros_like(kmean_scr)
            for g in range(group):
                for b in range(n_moba_blocks):
                    kb = k_ref[b * MOBA_BLOCK:(b + 1) * MOBA_BLOCK, head_cols(g)].astype(F32)
                    kmean_scr[g, b:b + 1, :] = jnp.mean(kb, axis=0, keepdims=True)
                for piece, part in enumerate(_split3(kmean_scr[g, 0:nbp, :])):
                    kmean3_scr[g, piece * nbp:(piece + 1) * nbp, :] = part.astype(BF16)

    for rb in range(tq // MOBA_BLOCK):
        rows = pl.ds(rb * MOBA_BLOCK, MOBA_BLOCK)
        for g in range(group):
            q = q_ref[rows, head_cols(g)]
            if mode == "fox":
                start = pl.multiple_of(i * tq + rb * MOBA_BLOCK, MOBA_BLOCK)
                ext = _fox_ext(cw_ref[:, pl.ds(start, MOBA_BLOCK)], hg * group + g, keys=False)
            else:
                g3 = lax.dot_general(kmean3_scr[g], q, (((1,), (1,)), ((), ())),
                                     preferred_element_type=F32)
                gate = g3[0:nbp] + g3[nbp:2 * nbp] + g3[2 * nbp:3 * nbp]
                own = i * (tq // MOBA_BLOCK) + rb
                blk = lax.broadcasted_iota(jnp.int32, gate.shape, 0)
                blk_f = blk.astype(F32)
                gv = jnp.where(blk < own, gate, NEG_INF)
                keep = blk == own
                for _ in range(min(MOBA_TOPK, n_moba_blocks)):
                    m = jnp.max(gv, axis=0, keepdims=True)
                    idx = jnp.min(jnp.where(gv == m, blk_f, float(LANES)), axis=0, keepdims=True)
                    hit = blk_f == idx
                    keep = jnp.logical_or(keep, jnp.logical_and(hit, m > NEG_INF))
                    gv = jnp.where(hit, NEG_INF, gv)
                ext_t = jnp.where(keep, 0.0, NEG_BIG)
                ext_t = jnp.concatenate(
                    [ext_t, jnp.zeros((LANES - nbp, MOBA_BLOCK), F32)], axis=0)
                ext = ext_t.T
            qaug[g, rows, 0:HEAD_DIM] = q
            qaug[g, rows, HEAD_DIM:2 * HEAD_DIM] = ext.astype(BF16)

    def scores(g, j):
        st = pl.multiple_of(j * tk, tk)
        return lax.dot_general(qaug[g], kaug[g, pl.ds(st, tk), :], (((1,), (1,)), ((), ())),
                               preferred_element_type=F32)

    def update(g, s, j, rows=slice(0, tq), causal=False):
        st = pl.multiple_of(j * tk, tk)
        if causal:
            s = s + bias_scr[...]
        m_old = m_scr[g, rows]
        m_new = jnp.maximum(m_old, jnp.max(s, axis=-1, keepdims=True))
        p = jnp.exp2(s - m_new).astype(BF16)
        pv = jnp.dot(p, vaug[g, pl.ds(st, tk), :], preferred_element_type=F32)
        acc_scr[g, rows] = jnp.exp2(m_old - m_new) * acc_scr[g, rows] + pv
        m_scr[g, rows] = m_new

    heads = range(group)
    m_scr[...] = jnp.full(m_scr.shape, NEG_INF, F32)
    acc_scr[...] = jnp.zeros(acc_scr.shape, F32)
    for g in heads:
        s_a[g] = scores(g, 0)

    def body(jj, carry):
        for g in heads:
            s_b[g] = scores(g, 2 * jj + 1)
        for g in heads:
            update(g, s_a[g], 2 * jj)
        for g in heads:
            s_a[g] = scores(g, 2 * jj + 2)
        for g in heads:
            update(g, s_b[g], 2 * jj + 1)
        return carry

    lax.fori_loop(0, i, body, 0)
    upper, lower = slice(0, tk), slice(tk, tq)
    for g in heads:
        st = pl.multiple_of((2 * i + 1) * tk, tk)
        s_b[g, upper] = lax.dot_general(qaug[g, lower], kaug[g, pl.ds(st, tk), :],
                                        (((1,), (1,)), ((), ())), preferred_element_type=F32)
    for g in heads:
        update(g, s_a[g, upper], 2 * i, upper, causal=True)
        update(g, s_a[g, lower], 2 * i, lower)
    for g in heads:
        update(g, s_b[g, upper], 2 * i + 1, lower, causal=True)
    for g in heads:
        o_ref[:, head_cols(g)] = (acc_scr[g, :, 0:HEAD_DIM]
                                  / acc_scr[g, :, HEAD_DIM:2 * HEAD_DIM]).astype(o_ref.dtype)


def attention(proj, batch, seq, heads, q_col, k_col, v_col, cum=None, tq=1024, group=2):
    n = proj.shape[0]
    tq = _tile(seq, tq)
    tk = tq // 2
    nq = seq // tq
    mode = "moba" if cum is None else "fox"
    assert tk % MOBA_BLOCK == 0 and seq // MOBA_BLOCK <= LANES
    gw = group * HEAD_DIM
    assert heads % group == 0 and q_col % group == 0 and k_col % group == 0 and v_col % group == 0
    in_specs = [
        pl.BlockSpec((tq, gw), lambda b, h, i: (b * nq + i, q_col // group + h)),
        pl.BlockSpec((seq, gw), lambda b, h, i: (b, k_col // group + h)),
        pl.BlockSpec((seq, gw), lambda b, h, i: (b, v_col // group + h)),
    ]
    args = [proj, proj, proj]
    wide = 2 * HEAD_DIM
    scratch = [pltpu.VMEM((group, seq, wide), BF16), pltpu.VMEM((group, seq, wide), BF16),
               pltpu.VMEM((group, tq, wide), BF16),
               pltpu.VMEM((group, tq, tk), F32), pltpu.VMEM((group, tq, tk), F32),
               pltpu.VMEM((tk, tk), F32),
               pltpu.VMEM((group, tq, 1), F32), pltpu.VMEM((group, tq, wide), F32)]
    if mode == "fox":
        in_specs.append(pl.BlockSpec((None, heads, seq), lambda b, h, i: (b, 0, 0)))
        args.append(cum)
    else:
        scratch += [pltpu.VMEM((group, LANES, HEAD_DIM), F32),
                    pltpu.VMEM((group, 3 * (-(-(seq // MOBA_BLOCK) // 16) * 16), HEAD_DIM), BF16)]
    return pl.pallas_call(
        functools.partial(_attn_kernel, mode=mode, tq=tq, tk=tk, seq=seq),
        grid=(batch, heads // group, nq),
        in_specs=in_specs,
        out_specs=pl.BlockSpec((tq, gw), lambda b, h, i: (b * nq + i, h)),
        out_shape=jax.ShapeDtypeStruct((n, heads * HEAD_DIM), BF16),
        scratch_shapes=scratch,
        compiler_params=_params(("arbitrary", "arbitrary", "arbitrary")),
        name=mode + "_attention",
    )(*args)


def _out_proj_kernel(*refs, gate_row, sh_row, n_experts):
    if n_experts:
        (mo_ref, fo_ref, g1_ref, g2_ref, w_ref, x_ref, mod_ref, gf_ref, wr_ref,
         o_ref, h_ref, route_ref) = refs
    else:
        mo_ref, fo_ref, g1_ref, g2_ref, w_ref, x_ref, mod_ref, gf_ref, o_ref, h_ref = refs
    half = mo_ref.shape[1]
    m1 = _rms(mo_ref[...].astype(F32), g1_ref[...]).astype(BF16)
    m2 = _rms(fo_ref[...].astype(F32), g2_ref[...]).astype(BF16)
    acc = jnp.dot(m1, w_ref[0:half, :], preferred_element_type=F32)
    acc = acc + jnp.dot(m2, w_ref[half:2 * half, :], preferred_element_type=F32)
    x_new = x_ref[...] + mod_ref[gate_row:gate_row + 1, :] * acc
    o_ref[...] = x_new
    sh = mod_ref[sh_row:sh_row + 1, :]
    sc = mod_ref[sh_row + 1:sh_row + 2, :]
    h = _rms(x_new, gf_ref[...]) * (1.0 + sc) + sh
    h_hi = h.astype(BF16)
    h_ref[...] = h_hi
    if n_experts:
        h_lo = (h - h_hi.astype(F32)).astype(BF16)
        l_hi = jnp.dot(h_hi, wr_ref[...], preferred_element_type=F32)
        l_lo = jnp.dot(h_lo, wr_ref[:, 0:LANES], preferred_element_type=F32)
        logits = l_hi[:, 0:LANES] + l_hi[:, LANES:2 * LANES] + l_lo
        route_ref[...] = _top2_route(logits, n_experts)


def out_proj(moba_o, fox_o, g1, g2, w_o, x2, mod, ffn_g, seq, gate_row, sh_row, router_w=None, tm=512):
    n, d = x2.shape
    half = moba_o.shape[1]
    tm = _tile(seq, tm)
    per_b = seq // tm
    n_experts = 0 if router_w is None else router_w.shape[1]
    in_specs = [
        pl.BlockSpec((tm, half), lambda i: (i, 0)),
        pl.BlockSpec((tm, half), lambda i: (i, 0)),
        pl.BlockSpec((1, half), lambda i: (0, 0)),
        pl.BlockSpec((1, half), lambda i: (0, 0)),
        pl.BlockSpec((2 * half, d), lambda i: (0, 0)),
        pl.BlockSpec((tm, d), lambda i: (i, 0)),
        pl.BlockSpec((None, 6, d), lambda i: (i // per_b, 0, 0)),
        pl.BlockSpec((1, d), lambda i: (0, 0)),
    ]
    args = [moba_o, fox_o, g1.reshape(1, half), g2.reshape(1, half), w_o, x2, mod, ffn_g.reshape(1, d)]
    out_specs = [pl.BlockSpec((tm, d), lambda i: (i, 0)), pl.BlockSpec((tm, d), lambda i: (i, 0))]
    out_shape = [jax.ShapeDtypeStruct((n, d), F32), jax.ShapeDtypeStruct((n, d), BF16)]
    if n_experts:
        in_specs.append(pl.BlockSpec((d, 2 * LANES), lambda i: (0, 0)))
        w_hi = router_w.astype(BF16)
        w_lo = (router_w - w_hi.astype(F32)).astype(BF16)
        args.append(jnp.zeros((d, 2 * LANES), BF16).at[:, :n_experts].set(w_hi)
                    .at[:, LANES:LANES + n_experts].set(w_lo))
        out_specs.append(pl.BlockSpec((tm, LANES), lambda i: (i, 0)))
        out_shape.append(jax.ShapeDtypeStruct((n, LANES), F32))
    return pl.pallas_call(
        functools.partial(_out_proj_kernel, gate_row=gate_row, sh_row=sh_row, n_experts=n_experts),
        grid=(n // tm,),
        in_specs=in_specs,
        out_specs=out_specs,
        out_shape=out_shape,
        compiler_params=_params(("arbitrary",)),
        name="out_proj",
    )(*args)


def _swiglu_tile(h, wg, wu):
    a = jnp.dot(h, wg, preferred_element_type=F32)
    b = jnp.dot(h, wu, preferred_element_type=F32)
    return a * jax.nn.sigmoid(a) * b


def _ffn_kernel(h_ref, wg_ref, wu_ref, wd_ref, x_ref, mod_ref, o_ref, *, gate_row):
    f = pl.program_id(1)

    @pl.when(f == 0)
    def _():
        o_ref[...] = jnp.zeros(o_ref.shape, o_ref.dtype)

    t = _swiglu_tile(h_ref[...], wg_ref[...], wu_ref[...]).astype(BF16)
    o_ref[...] += jnp.dot(t, wd_ref[...], preferred_element_type=F32)

    @pl.when(f == pl.num_programs(1) - 1)
    def _():
        o_ref[...] = x_ref[...] + mod_ref[gate_row:gate_row + 1, :] * o_ref[...]


def ffn_dense(h, wg, wu, wd, x2, mod, seq, gate_row, tm=1024, tf=512):
    n, d = x2.shape
    ff = wg.shape[1]
    tm = _tile(seq, tm)
    tf = _tile(ff, tf)
    per_b = seq // tm
    return pl.pallas_call(
        functools.partial(_ffn_kernel, gate_row=gate_row),
        grid=(n // tm, ff // tf),
        in_specs=[
            pl.BlockSpec((tm, d), lambda i, f: (i, 0)),
            pl.BlockSpec((d, tf), lambda i, f: (0, f)),
            pl.BlockSpec((d, tf), lambda i, f: (0, f)),
            pl.BlockSpec((tf, d), lambda i, f: (f, 0)),
            pl.BlockSpec((tm, d), lambda i, f: (i, 0)),
            pl.BlockSpec((None, 6, d), lambda i, f: (i // per_b, 0, 0)),
        ],
        out_specs=pl.BlockSpec((tm, d), lambda i, f: (i, 0)),
        out_shape=jax.ShapeDtypeStruct((n, d), F32),
        compiler_params=_params(("arbitrary", "arbitrary")),
        name="ffn_dense",
    )(h, wg, wu, wd, x2, mod)


def _route_plan(route, n_experts, tm):
    n = route.shape[0]
    pairs = TOP_K * n
    e_flat = route[:, :TOP_K].astype(jnp.int32).reshape(pairs)
    onehot = (e_flat[:, None] == jnp.arange(n_experts, dtype=jnp.int32)[None, :]).astype(jnp.int32)
    csum = jnp.cumsum(onehot, axis=0)
    counts = csum[-1]
    padded = (counts + tm - 1) // tm * tm
    pend = jnp.cumsum(padded)
    pstart = pend - padded
    rank = jnp.sum(csum * onehot, axis=1) - 1
    pos = (jnp.sum(pstart[None, :] * onehot, axis=1) + rank).astype(jnp.int32)

    n_tiles = pairs // tm + n_experts
    first_row = jnp.arange(n_tiles, dtype=jnp.int32) * tm
    tile_expert = jnp.minimum(jnp.sum((pend[None, :] <= first_row[:, None]).astype(jnp.int32), axis=1),
                              n_experts - 1)
    valid = first_row < pend[-1]
    return pos, (tile_expert.astype(jnp.int32), valid.astype(jnp.int32),
                 (pstart + counts).astype(jnp.int32))


def _dispatch_kernel(pos_ref, h_ref, init_ref, xs_ref, sem):
    del init_ref
    i = pl.program_id(0)
    tm = h_ref.shape[0]

    def issue(r, carry):
        for k in range(TOP_K):
            dst = pos_ref[(i * tm + r) * TOP_K + k]
            pltpu.make_async_copy(h_ref.at[r], xs_ref.at[dst], sem).start()
        return carry

    lax.fori_loop(0, tm, issue, 0)
    for k in range(TOP_K):
        pltpu.make_async_copy(h_ref, xs_ref.at[pl.ds(0, tm)], sem).wait()


def moe_dispatch(h3, pos, rows, tm=1024):
    n, sub, lanes = h3.shape
    tm = _tile(n, tm)
    return pl.pallas_call(
        _dispatch_kernel,
        grid_spec=pltpu.PrefetchScalarGridSpec(
            num_scalar_prefetch=1,
            grid=(n // tm,),
            in_specs=[pl.BlockSpec((tm, sub, lanes), lambda i, pos: (i, 0, 0)),
                      pl.BlockSpec(memory_space=pl.ANY)],
            out_specs=pl.BlockSpec(memory_space=pl.ANY),
            scratch_shapes=[pltpu.SemaphoreType.DMA(())],
        ),
        out_shape=jax.ShapeDtypeStruct((rows, sub, lanes), h3.dtype),
        input_output_aliases={2: 0},
        compiler_params=_params(("arbitrary",), disable_bounds_checks=True),
        name="moe_dispatch",
    )(pos, h3, jnp.zeros((rows, sub, lanes), h3.dtype))


def _moe_ffn_kernel(exp_ref, valid_ref, rowend_ref, xs_ref, wg_ref, wu_ref, wd_ref, o_ref, *, sub):
    t = pl.program_id(0)
    f = pl.program_id(1)
    tm = xs_ref.shape[0]

    @pl.when(valid_ref[t] == 1)
    def _():
        rows = t * tm + lax.broadcasted_iota(jnp.int32, (tm, 1), 0)
        real = rows < rowend_ref[exp_ref[t]]

        @pl.when(f == 0)
        def _():
            o_ref[...] = jnp.zeros(o_ref.shape, o_ref.dtype)

        for c in range(wg_ref.shape[1] // sub):
            cols = slice(c * sub, (c + 1) * sub)
            a = _swiglu_tile(xs_ref[...], wg_ref[:, cols].astype(BF16), wu_ref[:, cols].astype(BF16))
            a = jnp.where(real, a, 0.0).astype(BF16)
            o_ref[...] += jnp.dot(a, wd_ref[cols, :].astype(BF16), preferred_element_type=F32)

    @pl.when(jnp.logical_and(valid_ref[t] == 0, f == 0))
    def _():
        o_ref[...] = jnp.zeros(o_ref.shape, o_ref.dtype)


def moe_ffn(xs, plan, wg, wu, wd, tm, tf=512, sub=256):
    rows, d = xs.shape
    n_experts, _, ff = wg.shape
    tf = _tile(ff, tf)
    sub = _tile(tf, sub)
    nf = ff // tf

    def f_idx(t, f, va):
        return jnp.where(va[t] == 1, f, nf - 1)

    return pl.pallas_call(
        functools.partial(_moe_ffn_kernel, sub=sub),
        grid_spec=pltpu.PrefetchScalarGridSpec(
            num_scalar_prefetch=3,
            grid=(rows // tm, nf),
            in_specs=[
                pl.BlockSpec((tm, d), lambda t, f, ex, va, re: (t, 0)),
                pl.BlockSpec((None, d, tf), lambda t, f, ex, va, re: (ex[t], 0, f_idx(t, f, va))),
                pl.BlockSpec((None, d, tf), lambda t, f, ex, va, re: (ex[t], 0, f_idx(t, f, va))),
                pl.BlockSpec((None, tf, d), lambda t, f, ex, va, re: (ex[t], f_idx(t, f, va), 0)),
            ],
            out_specs=pl.BlockSpec((tm, d), lambda t, f, ex, va, re: (t, 0)),
        ),
        out_shape=jax.ShapeDtypeStruct((rows, d), F32),
        compiler_params=_params(("arbitrary", "arbitrary")),
        name="moe_ffn",
    )(*plan, xs, wg, wu, wd)


def _combine_kernel(pos_ref, ys_ref, route_ref, x_ref, mod_ref, g_ref, o_ref, buf, sem,
                    *, gate_row, final):
    i = pl.program_id(0)
    tm = x_ref.shape[0]
    slot = i % 2

    def gather(step, into):
        def issue(r, carry):
            for k in range(TOP_K):
                src = pos_ref[(step * tm + r) * TOP_K + k]
                pltpu.make_async_copy(ys_ref.at[pl.ds(src, 1), :], buf.at[into, k, pl.ds(r, 1), :],
                                      sem.at[into]).start()
            return carry

        lax.fori_loop(0, tm, issue, 0, unroll=2)

    @pl.when(i == 0)
    def _():
        gather(0, 0)

    more = i + 1 < pl.num_programs(0)
    for parity in range(2):
        @pl.when(jnp.logical_and(more, slot == parity))
        def _(parity=parity):
            gather(i + 1, 1 - parity)

    for k in range(TOP_K):
        pltpu.make_async_copy(ys_ref.at[pl.ds(0, tm), :], buf.at[slot, k], sem.at[slot]).wait()
    route = route_ref[...]
    y = buf[slot, 0] * route[:, TOP_K:TOP_K + 1]
    for k in range(1, TOP_K):
        y = y + buf[slot, k] * route[:, TOP_K + k:TOP_K + k + 1]
    x_new = x_ref[...] + mod_ref[gate_row:gate_row + 1, :] * y
    o_ref[...] = _rms(x_new, g_ref[...]) if final else x_new


def moe_combine(ys, pos, route, x2, mod, seq, gate_row, final_g=None, tm=256):
    n, d = x2.shape
    tm = _tile(seq, tm)
    per_b = seq // tm
    g = jnp.ones((1, d), F32) if final_g is None else final_g.reshape(1, d)
    return pl.pallas_call(
        functools.partial(_combine_kernel, gate_row=gate_row, final=final_g is not None),
        grid_spec=pltpu.PrefetchScalarGridSpec(
            num_scalar_prefetch=1,
            grid=(n // tm,),
            in_specs=[
                pl.BlockSpec(memory_space=pl.ANY),
                pl.BlockSpec((tm, LANES), lambda i, pos: (i, 0)),
                pl.BlockSpec((tm, d), lambda i, pos: (i, 0)),
                pl.BlockSpec((None, 6, d), lambda i, pos: (i // per_b, 0, 0)),
                pl.BlockSpec((1, d), lambda i, pos: (0, 0)),
            ],
            out_specs=pl.BlockSpec((tm, d), lambda i, pos: (i, 0)),
            scratch_shapes=[pltpu.VMEM((2, TOP_K, tm, d), F32), pltpu.SemaphoreType.DMA((2,))],
        ),
        out_shape=jax.ShapeDtypeStruct((n, d), F32),
        compiler_params=_params(("arbitrary",), disable_bounds_checks=True),
        name="moe_combine",
    )(pos, ys, route, x2, mod, g)


def ffn_moe(h, route, wg, wu, wd, x2, mod, seq, gate_row, final_g=None, tm=1024):
    n, d = x2.shape
    n_experts = wg.shape[0]
    tm = _tile(TOP_K * n, tm)
    pos, plan = _route_plan(route, n_experts, tm)
    rows = TOP_K * n + n_experts * tm
    xs3 = moe_dispatch(h.reshape(n, d // LANES, LANES), pos, rows)
    ys = moe_ffn(xs3.reshape(rows, d), plan, wg, wu, wd, tm)
    return moe_combine(ys, pos, route, x2, mod, seq, gate_row, final_g)


def _rope_tables(seq):
    f32 = np.float32
    inv = np.power(f32(ROPE_THETA), -np.arange(0, HEAD_DIM, 2, dtype=f32) / f32(HEAD_DIM)).astype(f32)
    ang = np.arange(seq, dtype=f32)[:, None] * inv[None, :]
    cos, sin = np.cos(ang).astype(f32), np.sin(ang).astype(f32)
    return (jnp.asarray(np.concatenate([cos, cos], axis=-1)),
            jnp.asarray(np.concatenate([-sin, sin], axis=-1)))


def kernel(x, c, attn_norm_g, ada_w, ada_b, w_in, forget_b, moba_out_g, fox_out_g, w_o, ffn_norm_g,
           dense_w_gate, dense_w_up, dense_w_down, router_w, moe_w_gate, moe_w_up, moe_w_down,
           final_norm_g):
    batch, seq, d = x.shape
    depth = attn_norm_g.shape[0]
    moba_w = moba_out_g.shape[1]
    fox_w = fox_out_g.shape[1]
    moba_h = moba_w // HEAD_DIM
    fox_h = fox_w // HEAD_DIM
    qkv_w = 3 * moba_w + 3 * fox_w
    assert moba_w == fox_w
    kinds = (ROPE_SCALED, ROPE, PLAIN, SCALED, PLAIN, PLAIN)
    hb = HEAD_DIM

    cos_t, sin_t = _rope_tables(seq)
    mods = ada_modulation(c, ada_w, ada_b)
    x2 = x.reshape(batch * seq, d)
    out = None

    for l in range(depth):
        mod = mods[l]
        w_qkv = w_in[l][:, :qkv_w].astype(BF16)
        wz = jnp.zeros((d, LANES), BF16).at[:, :fox_h].set(w_in[l][:, qkv_w:].astype(BF16))
        proj, fz = in_proj(x2, attn_norm_g[l], mod, w_qkv, wz, cos_t, sin_t, seq, kinds,
                           tn=moba_w, sh_row=0)
        cum = logf_cumsum(fz[:, :fox_h].reshape(batch, seq, fox_h), forget_b[l])
        cum = cum.transpose(0, 2, 1)
        moba_o = attention(proj, batch, seq, moba_h, q_col=0, k_col=moba_w // hb,
                           v_col=2 * moba_w // hb)
        fc = 3 * moba_w // hb
        fox_o = attention(proj, batch, seq, fox_h, q_col=fc, k_col=fc + fox_w // hb,
                          v_col=fc + 2 * fox_w // hb, cum=cum)
        i = l // 2
        last = l == depth - 1
        w_o_l = w_o[l].astype(BF16)
        if l % 2 == 0:
            x2, h = out_proj(moba_o, fox_o, moba_out_g[l], fox_out_g[l], w_o_l, x2, mod,
                             ffn_norm_g[l], seq, gate_row=2, sh_row=3)
            x2 = ffn_dense(h, dense_w_gate[i].astype(BF16), dense_w_up[i].astype(BF16),
                           dense_w_down[i].astype(BF16), x2, mod, seq, gate_row=5)
        else:
            x2, h, route = out_proj(moba_o, fox_o, moba_out_g[l], fox_out_g[l], w_o_l, x2, mod,
                                    ffn_norm_g[l], seq, gate_row=2, sh_row=3, router_w=router_w[i])
            x2 = ffn_moe(h, route, moe_w_gate[i], moe_w_up[i], moe_w_down[i], x2, mod, seq,
                         gate_row=5, final_g=final_norm_g if last else None)
            if last:
                out = x2
    if out is None:
        out = final_norm(x2, final_norm_g)
    return out.reshape(batch, seq, d)
```

```python
import functools
import math

import jax
import jax.numpy as jnp
import numpy as np
from jax import lax
from jax.experimental import pallas as pl
from jax.experimental.pallas import tpu as pltpu

F32 = jnp.float32
BF16 = jnp.bfloat16

HEAD_DIM = 128
MOBA_BLOCK = 256
MOBA_TOPK = 3
ROPE_THETA = 10000.0
EPS = 1e-6
TOP_K = 2
LANES = 128
VMEM_LIMIT = 60 * 1024 * 1024

NEG_INF = float("-inf")
NEG_BIG = -(2.0 ** 126)
LOG2E = math.log2(math.e)
Q_SCALE = HEAD_DIM ** -0.5 * LOG2E


def _params(semantics, **kw):
    return pltpu.CompilerParams(dimension_semantics=semantics, vmem_limit_bytes=VMEM_LIMIT, **kw)


def _tile(n, want):
    t = min(n, want)
    assert n % t == 0, (n, want)
    return t


def _ada_kernel(c_ref, w_ref, b_ref, o_ref):
    c = c_ref[...]
    c_act = c * jax.nn.sigmoid(c)
    o_ref[...] = jnp.dot(c_act, w_ref[...], preferred_element_type=F32,
                         precision=lax.Precision.HIGHEST) + b_ref[...]


def ada_modulation(c, ada_w, ada_b):
    n_layers, d, d6 = ada_w.shape
    b = c.shape[0]
    rows = -(-b // 8) * 8
    c_pad = jnp.zeros((rows, d), F32).at[:b].set(c)
    tn = _tile(d6, 1024)
    out = pl.pallas_call(
        _ada_kernel,
        grid=(n_layers, d6 // tn),
        in_specs=[
            pl.BlockSpec((rows, d), lambda l, j: (0, 0)),
            pl.BlockSpec((None, d, tn), lambda l, j: (l, 0, j)),
            pl.BlockSpec((None, 1, tn), lambda l, j: (l, 0, j)),
        ],
        out_specs=pl.BlockSpec((None, rows, tn), lambda l, j: (l, 0, j)),
        out_shape=jax.ShapeDtypeStruct((n_layers, rows, d6), F32),
        compiler_params=_params(("arbitrary", "arbitrary")),
        name="ada_modulation",
    )(c_pad, ada_w, ada_b.reshape(n_layers, 1, d6))
    return out[:, :b].reshape(n_layers, b, 6, d)


def _rms(x, g):
    return x * lax.rsqrt(jnp.mean(x * x, axis=-1, keepdims=True) + EPS) * g


def _top2_route(logits, n_experts):
    lane = lax.broadcasted_iota(jnp.int32, logits.shape, 1).astype(F32)
    lg = jnp.where(lane < n_experts, logits, NEG_INF)
    m1 = jnp.max(lg, axis=-1, keepdims=True)
    i1 = jnp.min(jnp.where(lg == m1, lane, float(LANES)), axis=-1, keepdims=True)
    lg2 = jnp.where(lane == i1, NEG_INF, lg)
    m2 = jnp.max(lg2, axis=-1, keepdims=True)
    i2 = jnp.min(jnp.where(lg2 == m2, lane, float(LANES)), axis=-1, keepdims=True)
    e2 = jnp.exp(m2 - m1)
    w1 = 1.0 / (1.0 + e2)
    w2 = e2 / (1.0 + e2)
    return jnp.where(lane == 0.0, i1, jnp.where(lane == 1.0, i2,
                     jnp.where(lane == 2.0, w1, jnp.where(lane == 3.0, w2, 0.0))))


def _final_norm_kernel(x_ref, g_ref, o_ref):
    o_ref[...] = _rms(x_ref[...], g_ref[...])


def final_norm(x2, g, tm=512):
    n, d = x2.shape
    tm = _tile(n, tm)
    return pl.pallas_call(
        _final_norm_kernel,
        grid=(n // tm,),
        in_specs=[pl.BlockSpec((tm, d), lambda i: (i, 0)),
                  pl.BlockSpec((1, d), lambda i: (0, 0))],
        out_specs=pl.BlockSpec((tm, d), lambda i: (i, 0)),
        out_shape=jax.ShapeDtypeStruct((n, d), F32),
        compiler_params=_params(("arbitrary",)),
        name="final_norm",
    )(x2, g.reshape(1, d))


PLAIN, ROPE, ROPE_SCALED, SCALED = 0, 1, 2, 3


def _in_proj_kernel(x_ref, g_ref, mod_ref, w_ref, wz_ref, cos_ref, sin_ref, o_ref, fz_ref, h_scr,
                    *, kinds, unit, sh_row):
    j = pl.program_id(1)
    tn = w_ref.shape[1]

    @pl.when(j == 0)
    def _():
        y = _rms(x_ref[...], g_ref[...])
        sh = mod_ref[sh_row:sh_row + 1, :]
        sc = mod_ref[sh_row + 1:sh_row + 2, :]
        h_scr[...] = (y * (1.0 + sc) + sh).astype(h_scr.dtype)
        fz_ref[...] = jnp.dot(h_scr[...], wz_ref[...], preferred_element_type=F32)

    per_tile = tn // unit
    tile_kinds = [tuple(kinds[t * per_tile:(t + 1) * per_tile]) for t in range(len(kinds) // per_tile)]

    def is_tile_kind(tk_):
        hits = [jj for jj, k in enumerate(tile_kinds) if k == tk_]
        cond = j == hits[0]
        for jj in hits[1:]:
            cond = jnp.logical_or(cond, j == jj)
        return cond

    for tk_ in sorted(set(tile_kinds)):
        @pl.when(is_tile_kind(tk_))
        def _(tk_=tk_):
            acc = jnp.dot(h_scr[...], w_ref[...], preferred_element_type=F32)
            for u, kind in enumerate(tk_):
                mult = Q_SCALE if kind in (ROPE_SCALED, SCALED) else 1.0
                if kind in (ROPE, ROPE_SCALED):
                    c = cos_ref[...] * mult
                    s = sin_ref[...] * mult
                    for hh in range(unit // HEAD_DIM):
                        cols = slice(u * unit + hh * HEAD_DIM, u * unit + (hh + 1) * HEAD_DIM)
                        t = acc[:, cols]
                        rot = pltpu.roll(t, HEAD_DIM // 2, 1)
                        o_ref[:, cols] = (t * c + rot * s).astype(o_ref.dtype)
                else:
                    cols = slice(u * unit, (u + 1) * unit)
                    o_ref[:, cols] = (acc[:, cols] * mult).astype(o_ref.dtype)


def in_proj(x2, g, mod, w_qkv, wz, cos_t, sin_t, seq, kinds, unit, sh_row, tm=1024, units_per_tile=2):
    n, d = x2.shape
    width = w_qkv.shape[1]
    assert width == unit * len(kinds) and len(kinds) % units_per_tile == 0
    tn = unit * units_per_tile
    tm = _tile(seq, tm)
    per_b = seq // tm
    return pl.pallas_call(
        functools.partial(_in_proj_kernel, kinds=tuple(kinds), unit=unit, sh_row=sh_row),
        grid=(n // tm, width // tn),
        in_specs=[
            pl.BlockSpec((tm, d), lambda i, j: (i, 0)),
            pl.BlockSpec((1, d), lambda i, j: (0, 0)),
            pl.BlockSpec((None, 6, d), lambda i, j: (i // per_b, 0, 0)),
            pl.BlockSpec((d, tn), lambda i, j: (0, j)),
            pl.BlockSpec((d, LANES), lambda i, j: (0, 0)),
            pl.BlockSpec((tm, HEAD_DIM), lambda i, j: (i % per_b, 0)),
            pl.BlockSpec((tm, HEAD_DIM), lambda i, j: (i % per_b, 0)),
        ],
        out_specs=[pl.BlockSpec((tm, tn), lambda i, j: (i, j)),
                   pl.BlockSpec((tm, LANES), lambda i, j: (i, 0))],
        out_shape=[jax.ShapeDtypeStruct((n, width), BF16),
                   jax.ShapeDtypeStruct((n, LANES), F32)],
        scratch_shapes=[pltpu.VMEM((tm, d), BF16)],
        compiler_params=_params(("arbitrary", "arbitrary")),
        name="in_proj",
    )(x2, g.reshape(1, d), mod, w_qkv, wz, cos_t, sin_t)


def _logf_cumsum_kernel(z_ref, b_ref, o_ref):
    z = z_ref[...] + b_ref[...]
    lf = jnp.minimum(z, 0.0) - jnp.log1p(jnp.exp(-jnp.abs(z)))
    r = lax.broadcasted_iota(jnp.int32, (LANES, LANES), 0)
    c = lax.broadcasted_iota(jnp.int32, (LANES, LANES), 1)
    tri = jnp.where(c <= r, 1.0, 0.0).astype(F32)
    carry = jnp.zeros((1, z.shape[1]), F32)
    for ch in range(z.shape[0] // LANES):
        seg = lf[ch * LANES:(ch + 1) * LANES, :]
        cs = jnp.dot(tri, seg, preferred_element_type=F32, precision=lax.Precision.HIGHEST) + carry
        o_ref[ch * LANES:(ch + 1) * LANES, :] = cs
        carry = cs[LANES - 1:LANES, :]


def logf_cumsum(z, forget_b):
    b, s, h = z.shape
    return pl.pallas_call(
        _logf_cumsum_kernel,
        grid=(b,),
        in_specs=[pl.BlockSpec((None, s, h), lambda i: (i, 0, 0)),
                  pl.BlockSpec((1, h), lambda i: (0, 0))],
        out_specs=pl.BlockSpec((None, s, h), lambda i: (i, 0, 0)),
        out_shape=jax.ShapeDtypeStruct((b, s, h), F32),
        compiler_params=_params(("arbitrary",)),
        name="logf_cumsum",
    )(z, forget_b.reshape(1, h))


def _split3(f):
    hi = f.astype(BF16).astype(F32)
    r = f - hi
    mid = r.astype(BF16).astype(F32)
    lo = (r - mid).astype(BF16).astype(F32)
    return hi, mid, lo


def _fox_ext(cw, head, keys):
    n = cw.shape[1]
    mine = lax.broadcasted_iota(jnp.int32, cw.shape, 0) == head
    c = jnp.sum(jnp.where(mine, cw, 0.0), axis=0, keepdims=True) * LOG2E
    hi, mid, lo = _split3(c)
    r = lax.broadcasted_iota(jnp.int32, (8, n), 0)
    if keys:
        e8 = jnp.where(r < 3, 1.0, jnp.where(r == 3, -hi, jnp.where(
            r == 4, -mid, jnp.where(r == 5, -lo, 0.0))))
    else:
        e8 = jnp.where(r == 0, hi, jnp.where(r == 1, mid, jnp.where(
            r == 2, lo, jnp.where(r < 6, 1.0, 0.0))))
    return jnp.concatenate([e8, jnp.zeros((LANES - 8, n), F32)], axis=0).T


def _attn_kernel(*refs, mode, tq, tk, seq):
    if mode == "fox":
        q_ref, k_ref, v_ref, cw_ref, o_ref, kaug, vaug, qaug, s_a, s_b, bias_scr, m_scr, acc_scr = refs
    else:
        (q_ref, k_ref, v_ref, o_ref, kaug, vaug, qaug, s_a, s_b, bias_scr, m_scr, acc_scr,
         kmean_scr, kmean3_scr) = refs
    hg = pl.program_id(1)
    i = pl.program_id(2)
    n_moba_blocks = seq // MOBA_BLOCK
    nbp = -(-n_moba_blocks // 16) * 16
    group = q_ref.shape[1] // HEAD_DIM

    def head_cols(g):
        return slice(g * HEAD_DIM, (g + 1) * HEAD_DIM)

    @pl.when(jnp.logical_and(jnp.logical_and(pl.program_id(0) == 0, hg == 0), i == 0))
    def _():
        r2 = lax.broadcasted_iota(jnp.int32, (tk, tk), 0)
        c2 = lax.broadcasted_iota(jnp.int32, (tk, tk), 1)
        ok = c2 <= r2
        if mode == "moba":
            ok = jnp.logical_or(ok, c2 // MOBA_BLOCK != r2 // MOBA_BLOCK)
        bias_scr[...] = jnp.where(ok, 0.0, NEG_INF)

    @pl.when(i == 0)
    def _():
        lane = lax.broadcasted_iota(jnp.int32, (tk, LANES), 1)
        row = lax.broadcasted_iota(jnp.int32, (tk, LANES), 0)

        def build(cidx, carry):
            st = pl.multiple_of(cidx * tk, tk)
            for g in range(group):
                kaug[g, pl.ds(st, tk), 0:HEAD_DIM] = k_ref[pl.ds(st, tk), head_cols(g)]
                vaug[g, pl.ds(st, tk), 0:HEAD_DIM] = v_ref[pl.ds(st, tk), head_cols(g)]
                vaug[g, pl.ds(st, tk), HEAD_DIM:2 * HEAD_DIM] = jnp.ones((tk, HEAD_DIM), BF16)
                if mode == "fox":
                    ext = _fox_ext(cw_ref[:, pl.ds(st, tk)], hg * group + g, keys=True)
                else:
                    ext = jnp.where(lane == (st + row) // MOBA_BLOCK, 1.0, 0.0)
                kaug[g, pl.ds(st, tk), HEAD_DIM:2 * HEAD_DIM] = ext.astype(BF16)
            return carry

        lax.fori_loop(0, seq // tk, build, 0)
        if mode == "moba":
            kmean_scr[...] = jnp.zeros_like(kmean_scr)
            for g in range(group):
                for b in range(n_moba_blocks):
                    kb = k_ref[b * MOBA_BLOCK:(b + 1) * MOBA_BLOCK, head_cols(g)].astype(F32)
                    kmean_scr[g, b:b + 1, :] = jnp.mean(kb, axis=0, keepdims=True)
                for piece, part in enumerate(_split3(kmean_scr[g, 0:nbp, :])):
                    kmean3_scr[g, piece * nbp:(piece + 1) * nbp, :] = part.astype(BF16)

    for rb in range(tq // MOBA_BLOCK):
        rows = pl.ds(rb * MOBA_BLOCK, MOBA_BLOCK)
        for g in range(group):
            q = q_ref[rows, head_cols(g)]
            if mode == "fox":
                start = pl.multiple_of(i * tq + rb * MOBA_BLOCK, MOBA_BLOCK)
                ext = _fox_ext(cw_ref[:, pl.ds(start, MOBA_BLOCK)], hg * group + g, keys=False)
            else:
                g3 = lax.dot_general(kmean3_scr[g], q, (((1,), (1,)), ((), ())),
                                     preferred_element_type=F32)
                gate = g3[0:nbp] + g3[nbp:2 * nbp] + g3[2 * nbp:3 * nbp]
                own = i * (tq // MOBA_BLOCK) + rb
                blk = lax.broadcasted_iota(jnp.int32, gate.shape, 0)
                blk_f = blk.astype(F32)
                gv = jnp.where(blk < own, gate, NEG_INF)
                keep = blk == own
                for _ in range(min(MOBA_TOPK, n_moba_blocks)):
                    m = jnp.max(gv, axis=0, keepdims=True)
                    idx = jnp.min(jnp.where(gv == m, blk_f, float(LANES)), axis=0, keepdims=True)
                    hit = blk_f == idx
                    keep = jnp.logical_or(keep, jnp.logical_and(hit, m > NEG_INF))
                    gv = jnp.where(hit, NEG_INF, gv)
                ext_t = jnp.where(keep, 0.0, NEG_BIG)
                ext_t = jnp.concatenate(
                    [ext_t, jnp.zeros((LANES - nbp, MOBA_BLOCK), F32)], axis=0)
                ext = ext_t.T
            qaug[g, rows, 0:HEAD_DIM] = q
            qaug[g, rows, HEAD_DIM:2 * HEAD_DIM] = ext.astype(BF16)

    def scores(g, j):
        st = pl.multiple_of(j * tk, tk)
        return lax.dot_general(qaug[g], kaug[g, pl.ds(st, tk), :], (((1,), (1,)), ((), ())),
                               preferred_element_type=F32)

    def update(g, s, j, rows=slice(0, tq), causal=False):
        st = pl.multiple_of(j * tk, tk)
        if causal:
            s = s + bias_scr[...]
        m_old = m_scr[g, rows]
        m_new = jnp.maximum(m_old, jnp.max(s, axis=-1, keepdims=True))
        p = jnp.exp2(s - m_new).astype(BF16)
        pv = jnp.dot(p, vaug[g, pl.ds(st, tk), :], preferred_element_type=F32)
        acc_scr[g, rows] = jnp.exp2(m_old - m_new) * acc_scr[g, rows] + pv
        m_scr[g, rows] = m_new

    heads = range(group)
    m_scr[...] = jnp.full(m_scr.shape, NEG_INF, F32)
    acc_scr[...] = jnp.zeros(acc_scr.shape, F32)
    for g in heads:
        s_a[g] = scores(g, 0)

    def body(jj, carry):
        for g in heads:
            s_b[g] = scores(g, 2 * jj + 1)
        for g in heads:
            update(g, s_a[g], 2 * jj)
        for g in heads:
            s_a[g] = scores(g, 2 * jj + 2)
        for g in heads:
            update(g, s_b[g], 2 * jj + 1)
        return carry

    lax.fori_loop(0, i, body, 0)
    upper, lower = slice(0, tk), slice(tk, tq)
    for g in heads:
        st = pl.multiple_of((2 * i + 1) * tk, tk)
        s_b[g, upper] = lax.dot_general(qaug[g, lower], kaug[g, pl.ds(st, tk), :],
                                        (((1,), (1,)), ((), ())), preferred_element_type=F32)
    for g in heads:
        update(g, s_a[g, upper], 2 * i, upper, causal=True)
        update(g, s_a[g, lower], 2 * i, lower)
    for g in heads:
        update(g, s_b[g, upper], 2 * i + 1, lower, causal=True)
    for g in heads:
        o_ref[:, head_cols(g)] = (acc_scr[g, :, 0:HEAD_DIM]
                                  / acc_scr[g, :, HEAD_DIM:2 * HEAD_DIM]).astype(o_ref.dtype)


def attention(proj, batch, seq, heads, q_col, k_col, v_col, cum=None, tq=1024, group=2):
    n = proj.shape[0]
    tq = _tile(seq, tq)
    tk = tq // 2
    nq = seq // tq
    mode = "moba" if cum is None else "fox"
    assert tk % MOBA_BLOCK == 0 and seq // MOBA_BLOCK <= LANES
    gw = group * HEAD_DIM
    assert heads % group == 0 and q_col % group == 0 and k_col % group == 0 and v_col % group == 0
    in_specs = [
        pl.BlockSpec((tq, gw), lambda b, h, i: (b * nq + i, q_col // group + h)),
        pl.BlockSpec((seq, gw), lambda b, h, i: (b, k_col // group + h)),
        pl.BlockSpec((seq, gw), lambda b, h, i: (b, v_col // group + h)),
    ]
    args = [proj, proj, proj]
    wide = 2 * HEAD_DIM
    scratch = [pltpu.VMEM((group, seq, wide), BF16), pltpu.VMEM((group, seq, wide), BF16),
               pltpu.VMEM((group, tq, wide), BF16),
               pltpu.VMEM((group, tq, tk), F32), pltpu.VMEM((group, tq, tk), F32),
               pltpu.VMEM((tk, tk), F32),
               pltpu.VMEM((group, tq, 1), F32), pltpu.VMEM((group, tq, wide), F32)]
    if mode == "fox":
        in_specs.append(pl.BlockSpec((None, heads, seq), lambda b, h, i: (b, 0, 0)))
        args.append(cum)
    else:
        scratch += [pltpu.VMEM((group, LANES, HEAD_DIM), F32),
                    pltpu.VMEM((group, 3 * (-(-(seq // MOBA_BLOCK) // 16) * 16), HEAD_DIM), BF16)]
    return pl.pallas_call(
        functools.partial(_attn_kernel, mode=mode, tq=tq, tk=tk, seq=seq),
        grid=(batch, heads // group, nq),
        in_specs=in_specs,
        out_specs=pl.BlockSpec((tq, gw), lambda b, h, i: (b * nq + i, h)),
        out_shape=jax.ShapeDtypeStruct((n, heads * HEAD_DIM), BF16),
        scratch_shapes=scratch,
        compiler_params=_params(("arbitrary", "arbitrary", "arbitrary")),
        name=mode + "_attention",
    )(*args)


def _out_proj_kernel(*refs, gate_row, sh_row, n_experts):
    if n_experts:
        (mo_ref, fo_ref, g1_ref, g2_ref, w_ref, x_ref, mod_ref, gf_ref, wr_ref,
         o_ref, h_ref, route_ref) = refs
    else:
        mo_ref, fo_ref, g1_ref, g2_ref, w_ref, x_ref, mod_ref, gf_ref, o_ref, h_ref = refs
    half = mo_ref.shape[1]
    m1 = _rms(mo_ref[...].astype(F32), g1_ref[...]).astype(BF16)
    m2 = _rms(fo_ref[...].astype(F32), g2_ref[...]).astype(BF16)
    acc = jnp.dot(m1, w_ref[0:half, :], preferred_element_type=F32)
    acc = acc + jnp.dot(m2, w_ref[half:2 * half, :], preferred_element_type=F32)
    x_new = x_ref[...] + mod_ref[gate_row:gate_row + 1, :] * acc
    o_ref[...] = x_new
    sh = mod_ref[sh_row:sh_row + 1, :]
    sc = mod_ref[sh_row + 1:sh_row + 2, :]
    h = _rms(x_new, gf_ref[...]) * (1.0 + sc) + sh
    h_hi = h.astype(BF16)
    h_ref[...] = h_hi
    if n_experts:
        h_lo = (h - h_hi.astype(F32)).astype(BF16)
        l_hi = jnp.dot(h_hi, wr_ref[...], preferred_element_type=F32)
        l_lo = jnp.dot(h_lo, wr_ref[:, 0:LANES], preferred_element_type=F32)
        logits = l_hi[:, 0:LANES] + l_hi[:, LANES:2 * LANES] + l_lo
        route_ref[...] = _top2_route(logits, n_experts)


def out_proj(moba_o, fox_o, g1, g2, w_o, x2, mod, ffn_g, seq, gate_row, sh_row, router_w=None, tm=512):
    n, d = x2.shape
    half = moba_o.shape[1]
    tm = _tile(seq, tm)
    per_b = seq // tm
    n_experts = 0 if router_w is None else router_w.shape[1]
    in_specs = [
        pl.BlockSpec((tm, half), lambda i: (i, 0)),
        pl.BlockSpec((tm, half), lambda i: (i, 0)),
        pl.BlockSpec((1, half), lambda i: (0, 0)),
        pl.BlockSpec((1, half), lambda i: (0, 0)),
        pl.BlockSpec((2 * half, d), lambda i: (0, 0)),
        pl.BlockSpec((tm, d), lambda i: (i, 0)),
        pl.BlockSpec((None, 6, d), lambda i: (i // per_b, 0, 0)),
        pl.BlockSpec((1, d), lambda i: (0, 0)),
    ]
    args = [moba_o, fox_o, g1.reshape(1, half), g2.reshape(1, half), w_o, x2, mod, ffn_g.reshape(1, d)]
    out_specs = [pl.BlockSpec((tm, d), lambda i: (i, 0)), pl.BlockSpec((tm, d), lambda i: (i, 0))]
    out_shape = [jax.ShapeDtypeStruct((n, d), F32), jax.ShapeDtypeStruct((n, d), BF16)]
    if n_experts:
        in_specs.append(pl.BlockSpec((d, 2 * LANES), lambda i: (0, 0)))
        w_hi = router_w.astype(BF16)
        w_lo = (router_w - w_hi.astype(F32)).astype(BF16)
        args.append(jnp.zeros((d, 2 * LANES), BF16).at[:, :n_experts].set(w_hi)
                    .at[:, LANES:LANES + n_experts].set(w_lo))
        out_specs.append(pl.BlockSpec((tm, LANES), lambda i: (i, 0)))
        out_shape.append(jax.ShapeDtypeStruct((n, LANES), F32))
    return pl.pallas_call(
        functools.partial(_out_proj_kernel, gate_row=gate_row, sh_row=sh_row, n_experts=n_experts),
        grid=(n // tm,),
        in_specs=in_specs,
        out_specs=out_specs,
        out_shape=out_shape,
        compiler_params=_params(("arbitrary",)),
        name="out_proj",
    )(*args)


def _swiglu_tile(h, wg, wu):
    a = jnp.dot(h, wg, preferred_element_type=F32)
    b = jnp.dot(h, wu, preferred_element_type=F32)
    return a * jax.nn.sigmoid(a) * b


def _ffn_kernel(h_ref, wg_ref, wu_ref, wd_ref, x_ref, mod_ref, o_ref, *, gate_row):
    f = pl.program_id(1)

    @pl.when(f == 0)
    def _():
        o_ref[...] = jnp.zeros(o_ref.shape, o_ref.dtype)

    t = _swiglu_tile(h_ref[...], wg_ref[...], wu_ref[...]).astype(BF16)
    o_ref[...] += jnp.dot(t, wd_ref[...], preferred_element_type=F32)

    @pl.when(f == pl.num_programs(1) - 1)
    def _():
        o_ref[...] = x_ref[...] + mod_ref[gate_row:gate_row + 1, :] * o_ref[...]


def ffn_dense(h, wg, wu, wd, x2, mod, seq, gate_row, tm=1024, tf=512):
    n, d = x2.shape
    ff = wg.shape[1]
    tm = _tile(seq, tm)
    tf = _tile(ff, tf)
    per_b = seq // tm
    return pl.pallas_call(
        functools.partial(_ffn_kernel, gate_row=gate_row),
        grid=(n // tm, ff // tf),
        in_specs=[
            pl.BlockSpec((tm, d), lambda i, f: (i, 0)),
            pl.BlockSpec((d, tf), lambda i, f: (0, f)),
            pl.BlockSpec((d, tf), lambda i, f: (0, f)),
            pl.BlockSpec((tf, d), lambda i, f: (f, 0)),
            pl.BlockSpec((tm, d), lambda i, f: (i, 0)),
            pl.BlockSpec((None, 6, d), lambda i, f: (i // per_b, 0, 0)),
        ],
        out_specs=pl.BlockSpec((tm, d), lambda i, f: (i, 0)),
        out_shape=jax.ShapeDtypeStruct((n, d), F32),
        compiler_params=_params(("arbitrary", "arbitrary")),
        name="ffn_dense",
    )(h, wg, wu, wd, x2, mod)


def _route_plan(route, n_experts, tm):
    n = route.shape[0]
    pairs = TOP_K * n
    e_flat = route[:, :TOP_K].astype(jnp.int32).reshape(pairs)
    onehot = (e_flat[:, None] == jnp.arange(n_experts, dtype=jnp.int32)[None, :]).astype(jnp.int32)
    csum = jnp.cumsum(onehot, axis=0)
    counts = csum[-1]
    padded = (counts + tm - 1) // tm * tm
    pend = jnp.cumsum(padded)
    pstart = pend - padded
    rank = jnp.sum(csum * onehot, axis=1) - 1
    pos = (jnp.sum(pstart[None, :] * onehot, axis=1) + rank).astype(jnp.int32)

    n_tiles = pairs // tm + n_experts
    first_row = jnp.arange(n_tiles, dtype=jnp.int32) * tm
    tile_expert = jnp.minimum(jnp.sum((pend[None, :] <= first_row[:, None]).astype(jnp.int32), axis=1),
                              n_experts - 1)
    valid = first_row < pend[-1]
    return pos, (tile_expert.astype(jnp.int32), valid.astype(jnp.int32),
                 (pstart + counts).astype(jnp.int32))


def _dispatch_kernel(pos_ref, h_ref, init_ref, xs_ref, sem):
    del init_ref
    i = pl.program_id(0)
    tm = h_ref.shape[0]

    def issue(r, carry):
        for k in range(TOP_K):
            dst = pos_ref[(i * tm + r) * TOP_K + k]
            pltpu.make_async_copy(h_ref.at[r], xs_ref.at[dst], sem).start()
        return carry

    lax.fori_loop(0, tm, issue, 0)
    for k in range(TOP_K):
        pltpu.make_async_copy(h_ref, xs_ref.at[pl.ds(0, tm)], sem).wait()


def moe_dispatch(h3, pos, rows, tm=1024):
    n, sub, lanes = h3.shape
    tm = _tile(n, tm)
    return pl.pallas_call(
        _dispatch_kernel,
        grid_spec=pltpu.PrefetchScalarGridSpec(
            num_scalar_prefetch=1,
            grid=(n // tm,),
            in_specs=[pl.BlockSpec((tm, sub, lanes), lambda i, pos: (i, 0, 0)),
                      pl.BlockSpec(memory_space=pl.ANY)],
            out_specs=pl.BlockSpec(memory_space=pl.ANY),
            scratch_shapes=[pltpu.SemaphoreType.DMA(())],
        ),
        out_shape=jax.ShapeDtypeStruct((rows, sub, lanes), h3.dtype),
        input_output_aliases={2: 0},
        compiler_params=_params(("arbitrary",), disable_bounds_checks=True),
        name="moe_dispatch",
    )(pos, h3, jnp.zeros((rows, sub, lanes), h3.dtype))


def _moe_ffn_kernel(exp_ref, valid_ref, rowend_ref, xs_ref, wg_ref, wu_ref, wd_ref, o_ref, *, sub):
    t = pl.program_id(0)
    f = pl.program_id(1)
    tm = xs_ref.shape[0]

    @pl.when(valid_ref[t] == 1)
    def _():
        rows = t * tm + lax.broadcasted_iota(jnp.int32, (tm, 1), 0)
        real = rows < rowend_ref[exp_ref[t]]

        @pl.when(f == 0)
        def _():
            o_ref[...] = jnp.zeros(o_ref.shape, o_ref.dtype)

        for c in range(wg_ref.shape[1] // sub):
            cols = slice(c * sub, (c + 1) * sub)
            a = _swiglu_tile(xs_ref[...], wg_ref[:, cols].astype(BF16), wu_ref[:, cols].astype(BF16))
            a = jnp.where(real, a, 0.0).astype(BF16)
            o_ref[...] += jnp.dot(a, wd_ref[cols, :].astype(BF16), preferred_element_type=F32)

    @pl.when(jnp.logical_and(valid_ref[t] == 0, f == 0))
    def _():
        o_ref[...] = jnp.zeros(o_ref.shape, o_ref.dtype)


def moe_ffn(xs, plan, wg, wu, wd, tm, tf=512, sub=256):
    rows, d = xs.shape
    n_experts, _, ff = wg.shape
    tf = _tile(ff, tf)
    sub = _tile(tf, sub)
    nf = ff // tf

    def f_idx(t, f, va):
        return jnp.where(va[t] == 1, f, nf - 1)

    return pl.pallas_call(
        functools.partial(_moe_ffn_kernel, sub=sub),
        grid_spec=pltpu.PrefetchScalarGridSpec(
            num_scalar_prefetch=3,
            grid=(rows // tm, nf),
            in_specs=[
                pl.BlockSpec((tm, d), lambda t, f, ex, va, re: (t, 0)),
                pl.BlockSpec((None, d, tf), lambda t, f, ex, va, re: (ex[t], 0, f_idx(t, f, va))),
                pl.BlockSpec((None, d, tf), lambda t, f, ex, va, re: (ex[t], 0, f_idx(t, f, va))),
                pl.BlockSpec((None, tf, d), lambda t, f, ex, va, re: (ex[t], f_idx(t, f, va), 0)),
            ],
            out_specs=pl.BlockSpec((tm, d), lambda t, f, ex, va, re: (t, 0)),
        ),
        out_shape=jax.ShapeDtypeStruct((rows, d), F32),
        compiler_params=_params(("arbitrary", "arbitrary")),
        name="moe_ffn",
    )(*plan, xs, wg, wu, wd)


def _combine_kernel(pos_ref, ys_ref, route_ref, x_ref, mod_ref, g_ref, o_ref, buf, sem,
                    *, gate_row, final):
    i = pl.program_id(0)
    tm = x_ref.shape[0]
    slot = i % 2

    def gather(step, into):
        def issue(r, carry):
            for k in range(TOP_K):
                src = pos_ref[(step * tm + r) * TOP_K + k]
                pltpu.make_async_copy(ys_ref.at[pl.ds(src, 1), :], buf.at[into, k, pl.ds(r, 1), :],
                                      sem.at[into]).start()
            return carry

        lax.fori_loop(0, tm, issue, 0, unroll=2)

    @pl.when(i == 0)
    def _():
        gather(0, 0)

    more = i + 1 < pl.num_programs(0)
    for parity in range(2):
        @pl.when(jnp.logical_and(more, slot == parity))
        def _(parity=parity):
            gather(i + 1, 1 - parity)

    for k in range(TOP_K):
        pltpu.make_async_copy(ys_ref.at[pl.ds(0, tm), :], buf.at[slot, k], sem.at[slot]).wait()
    route = route_ref[...]
    y = buf[slot, 0] * route[:, TOP_K:TOP_K + 1]
    for k in range(1, TOP_K):
        y = y + buf[slot, k] * route[:, TOP_K + k:TOP_K + k + 1]
    x_new = x_ref[...] + mod_ref[gate_row:gate_row + 1, :] * y
    o_ref[...] = _rms(x_new, g_ref[...]) if final else x_new


def moe_combine(ys, pos, route, x2, mod, seq, gate_row, final_g=None, tm=512):
    n, d = x2.shape
    tm = _tile(seq, tm)
    per_b = seq // tm
    g = jnp.ones((1, d), F32) if final_g is None else final_g.reshape(1, d)
    return pl.pallas_call(
        functools.partial(_combine_kernel, gate_row=gate_row, final=final_g is not None),
        grid_spec=pltpu.PrefetchScalarGridSpec(
            num_scalar_prefetch=1,
            grid=(n // tm,),
            in_specs=[
                pl.BlockSpec(memory_space=pl.ANY),
                pl.BlockSpec((tm, LANES), lambda i, pos: (i, 0)),
                pl.BlockSpec((tm, d), lambda i, pos: (i, 0)),
                pl.BlockSpec((None, 6, d), lambda i, pos: (i // per_b, 0, 0)),
                pl.BlockSpec((1, d), lambda i, pos: (0, 0)),
            ],
            out_specs=pl.BlockSpec((tm, d), lambda i, pos: (i, 0)),
            scratch_shapes=[pltpu.VMEM((2, TOP_K, tm, d), F32), pltpu.SemaphoreType.DMA((2,))],
        ),
        out_shape=jax.ShapeDtypeStruct((n, d), F32),
        compiler_params=_params(("arbitrary",), disable_bounds_checks=True),
        name="moe_combine",
    )(pos, ys, route, x2, mod, g)


def ffn_moe(h, route, wg, wu, wd, x2, mod, seq, gate_row, final_g=None, tm=1024):
    n, d = x2.shape
    n_experts = wg.shape[0]
    tm = _tile(TOP_K * n, tm)
    pos, plan = _route_plan(route, n_experts, tm)
    rows = TOP_K * n + n_experts * tm
    xs3 = moe_dispatch(h.reshape(n, d // LANES, LANES), pos, rows)
    ys = moe_ffn(xs3.reshape(rows, d), plan, wg, wu, wd, tm)
    return moe_combine(ys, pos, route, x2, mod, seq, gate_row, final_g)


def _rope_tables(seq):
    f32 = np.float32
    inv = np.power(f32(ROPE_THETA), -np.arange(0, HEAD_DIM, 2, dtype=f32) / f32(HEAD_DIM)).astype(f32)
    ang = np.arange(seq, dtype=f32)[:, None] * inv[None, :]
    cos, sin = np.cos(ang).astype(f32), np.sin(ang).astype(f32)
    return (jnp.asarray(np.concatenate([cos, cos], axis=-1)),
            jnp.asarray(np.concatenate([-sin, sin], axis=-1)))


def kernel(x, c, attn_norm_g, ada_w, ada_b, w_in, forget_b, moba_out_g, fox_out_g, w_o, ffn_norm_g,
           dense_w_gate, dense_w_up, dense_w_down, router_w, moe_w_gate, moe_w_up, moe_w_down,
           final_norm_g):
    batch, seq, d = x.shape
    depth = attn_norm_g.shape[0]
    moba_w = moba_out_g.shape[1]
    fox_w = fox_out_g.shape[1]
    moba_h = moba_w // HEAD_DIM
    fox_h = fox_w // HEAD_DIM
    qkv_w = 3 * moba_w + 3 * fox_w
    assert moba_w == fox_w
    kinds = (ROPE_SCALED, ROPE, PLAIN, SCALED, PLAIN, PLAIN)
    hb = HEAD_DIM

    cos_t, sin_t = _rope_tables(seq)
    mods = ada_modulation(c, ada_w, ada_b)
    x2 = x.reshape(batch * seq, d)
    out = None

    for l in range(depth):
        mod = mods[l]
        w_qkv = w_in[l][:, :qkv_w].astype(BF16)
        wz = jnp.zeros((d, LANES), BF16).at[:, :fox_h].set(w_in[l][:, qkv_w:].astype(BF16))
        proj, fz = in_proj(x2, attn_norm_g[l], mod, w_qkv, wz, cos_t, sin_t, seq, kinds,
                           unit=moba_w, sh_row=0)
        cum = logf_cumsum(fz[:, :fox_h].reshape(batch, seq, fox_h), forget_b[l])
        cum = cum.transpose(0, 2, 1)
        moba_o = attention(proj, batch, seq, moba_h, q_col=0, k_col=moba_w // hb,
                           v_col=2 * moba_w // hb)
        fc = 3 * moba_w // hb
        fox_o = attention(proj, batch, seq, fox_h, q_col=fc, k_col=fc + fox_w // hb,
                          v_col=fc + 2 * fox_w // hb, cum=cum)
        i = l // 2
        last = l == depth - 1
        w_o_l = w_o[l].astype(BF16)
        if l % 2 == 0:
            x2, h = out_proj(moba_o, fox_o, moba_out_g[l], fox_out_g[l], w_o_l, x2, mod,
                             ffn_norm_g[l], seq, gate_row=2, sh_row=3)
            x2 = ffn_dense(h, dense_w_gate[i].astype(BF16), dense_w_up[i].astype(BF16),
                           dense_w_down[i].astype(BF16), x2, mod, seq, gate_row=5)
        else:
            x2, h, route = out_proj(moba_o, fox_o, moba_out_g[l], fox_out_g[l], w_o_l, x2, mod,
                                    ffn_norm_g[l], seq, gate_row=2, sh_row=3, router_w=router_w[i])
            x2 = ffn_moe(h, route, moe_w_gate[i], moe_w_up[i], moe_w_down[i], x2, mod, seq,
                         gate_row=5, final_g=final_norm_g if last else None)
            if last:
                out = x2
    if out is None:
        out = final_norm(x2, final_norm_g)
    return out.reshape(batch, seq, d)
```

```python
import functools
import math

import jax
import jax.numpy as jnp
import numpy as np
from jax import lax
from jax.experimental import pallas as pl
from jax.experimental.pallas import tpu as pltpu

F32 = jnp.float32
BF16 = jnp.bfloat16

HEAD_DIM = 128
MOBA_BLOCK = 256
MOBA_TOPK = 3
ROPE_THETA = 10000.0
EPS = 1e-6
TOP_K = 2
LANES = 128
VMEM_LIMIT = 60 * 1024 * 1024

NEG_INF = float("-inf")
NEG_BIG = -(2.0 ** 126)
LOG2E = math.log2(math.e)
Q_SCALE = HEAD_DIM ** -0.5 * LOG2E


def _params(semantics, **kw):
    return pltpu.CompilerParams(dimension_semantics=semantics, vmem_limit_bytes=VMEM_LIMIT, **kw)


def _tile(n, want):
    t = min(n, want)
    assert n % t == 0, (n, want)
    return t


def _ada_kernel(c_ref, w_ref, b_ref, o_ref):
    half = c_ref.shape[0] // 2
    c = c_ref[...]
    act = c * jax.nn.sigmoid(c)
    hi = act.astype(BF16)
    lo = (act - hi.astype(F32)).astype(BF16)
    upper = lax.broadcasted_iota(jnp.int32, act.shape, 0) >= half
    lhs = jnp.where(upper, lo, hi)
    w = w_ref[...]
    w_hi = w.astype(BF16)
    w_lo = (w - w_hi.astype(F32)).astype(BF16)
    r_hi = jnp.dot(lhs, w_hi, preferred_element_type=F32)
    r_lo = jnp.dot(lhs, w_lo, preferred_element_type=F32)
    o_ref[...] = r_hi + pltpu.roll(r_hi, half, 0) + r_lo + b_ref[...]


def ada_modulation(c, ada_w, ada_b):
    n_layers, d, d6 = ada_w.shape
    b = c.shape[0]
    half = -(-b // 8) * 8
    rows = 2 * half
    c_pad = jnp.zeros((rows, d), F32).at[:b].set(c).at[half:half + b].set(c)
    tn = _tile(d6, 1024)
    out = pl.pallas_call(
        _ada_kernel,
        grid=(n_layers, d6 // tn),
        in_specs=[
            pl.BlockSpec((rows, d), lambda l, j: (0, 0)),
            pl.BlockSpec((None, d, tn), lambda l, j: (l, 0, j)),
            pl.BlockSpec((None, 1, tn), lambda l, j: (l, 0, j)),
        ],
        out_specs=pl.BlockSpec((None, rows, tn), lambda l, j: (l, 0, j)),
        out_shape=jax.ShapeDtypeStruct((n_layers, rows, d6), F32),
        compiler_params=_params(("arbitrary", "arbitrary")),
        name="ada_modulation",
    )(c_pad, ada_w, ada_b.reshape(n_layers, 1, d6))
    return out[:, :b].reshape(n_layers, b, 6, d)


def _rms(x, g):
    return x * lax.rsqrt(jnp.mean(x * x, axis=-1, keepdims=True) + EPS) * g


def _top2_route(logits, n_experts):
    lane = lax.broadcasted_iota(jnp.int32, logits.shape, 1).astype(F32)
    lg = jnp.where(lane < n_experts, logits, NEG_INF)
    m1 = jnp.max(lg, axis=-1, keepdims=True)
    i1 = jnp.min(jnp.where(lg == m1, lane, float(LANES)), axis=-1, keepdims=True)
    lg2 = jnp.where(lane == i1, NEG_INF, lg)
    m2 = jnp.max(lg2, axis=-1, keepdims=True)
    i2 = jnp.min(jnp.where(lg2 == m2, lane, float(LANES)), axis=-1, keepdims=True)
    e2 = jnp.exp(m2 - m1)
    w1 = 1.0 / (1.0 + e2)
    w2 = e2 / (1.0 + e2)
    return jnp.where(lane == 0.0, i1, jnp.where(lane == 1.0, i2,
                     jnp.where(lane == 2.0, w1, jnp.where(lane == 3.0, w2, 0.0))))


def _final_norm_kernel(x_ref, g_ref, o_ref):
    o_ref[...] = _rms(x_ref[...], g_ref[...])


def final_norm(x2, g, tm=512):
    n, d = x2.shape
    tm = _tile(n, tm)
    return pl.pallas_call(
        _final_norm_kernel,
        grid=(n // tm,),
        in_specs=[pl.BlockSpec((tm, d), lambda i: (i, 0)),
                  pl.BlockSpec((1, d), lambda i: (0, 0))],
        out_specs=pl.BlockSpec((tm, d), lambda i: (i, 0)),
        out_shape=jax.ShapeDtypeStruct((n, d), F32),
        compiler_params=_params(("arbitrary",)),
        name="final_norm",
    )(x2, g.reshape(1, d))


PLAIN, ROPE, ROPE_SCALED, SCALED = 0, 1, 2, 3


def _in_proj_kernel(x_ref, g_ref, mod_ref, w_ref, wz_ref, cos_ref, sin_ref, o_ref, fz_ref, h_scr,
                    *, kinds, unit, sh_row):
    j = pl.program_id(1)
    tn = w_ref.shape[1]

    @pl.when(j == 0)
    def _():
        y = _rms(x_ref[...], g_ref[...])
        sh = mod_ref[sh_row:sh_row + 1, :]
        sc = mod_ref[sh_row + 1:sh_row + 2, :]
        h_scr[...] = (y * (1.0 + sc) + sh).astype(h_scr.dtype)
        fz_ref[...] = jnp.dot(h_scr[...], wz_ref[...], preferred_element_type=F32)

    per_tile = tn // unit
    tile_kinds = [tuple(kinds[t * per_tile:(t + 1) * per_tile]) for t in range(len(kinds) // per_tile)]

    def is_tile_kind(tk_):
        hits = [jj for jj, k in enumerate(tile_kinds) if k == tk_]
        cond = j == hits[0]
        for jj in hits[1:]:
            cond = jnp.logical_or(cond, j == jj)
        return cond

    for tk_ in sorted(set(tile_kinds)):
        @pl.when(is_tile_kind(tk_))
        def _(tk_=tk_):
            acc = jnp.dot(h_scr[...], w_ref[...], preferred_element_type=F32)
            for u, kind in enumerate(tk_):
                mult = Q_SCALE if kind in (ROPE_SCALED, SCALED) else 1.0
                if kind in (ROPE, ROPE_SCALED):
                    c = cos_ref[...] * mult
                    s = sin_ref[...] * mult
                    for hh in range(unit // HEAD_DIM):
                        cols = slice(u * unit + hh * HEAD_DIM, u * unit + (hh + 1) * HEAD_DIM)
                        t = acc[:, cols]
                        rot = pltpu.roll(t, HEAD_DIM // 2, 1)
                        o_ref[:, cols] = (t * c + rot * s).astype(o_ref.dtype)
                else:
                    cols = slice(u * unit, (u + 1) * unit)
                    o_ref[:, cols] = (acc[:, cols] * mult).astype(o_ref.dtype)


def in_proj(x2, g, mod, w_qkv, wz, cos_t, sin_t, seq, kinds, unit, sh_row, tm=1024, units_per_tile=2):
    n, d = x2.shape
    width = w_qkv.shape[1]
    assert width == unit * len(kinds) and len(kinds) % units_per_tile == 0
    tn = unit * units_per_tile
    tm = _tile(seq, tm)
    per_b = seq // tm
    return pl.pallas_call(
        functools.partial(_in_proj_kernel, kinds=tuple(kinds), unit=unit, sh_row=sh_row),
        grid=(n // tm, width // tn),
        in_specs=[
            pl.BlockSpec((tm, d), lambda i, j: (i, 0)),
            pl.BlockSpec((1, d), lambda i, j: (0, 0)),
            pl.BlockSpec((None, 6, d), lambda i, j: (i // per_b, 0, 0)),
            pl.BlockSpec((d, tn), lambda i, j: (0, j)),
            pl.BlockSpec((d, LANES), lambda i, j: (0, 0)),
            pl.BlockSpec((tm, HEAD_DIM), lambda i, j: (i % per_b, 0)),
            pl.BlockSpec((tm, HEAD_DIM), lambda i, j: (i % per_b, 0)),
        ],
        out_specs=[pl.BlockSpec((tm, tn), lambda i, j: (i, j)),
                   pl.BlockSpec((tm, LANES), lambda i, j: (i, 0))],
        out_shape=[jax.ShapeDtypeStruct((n, width), BF16),
                   jax.ShapeDtypeStruct((n, LANES), F32)],
        scratch_shapes=[pltpu.VMEM((tm, d), BF16)],
        compiler_params=_params(("arbitrary", "arbitrary")),
        name="in_proj",
    )(x2, g.reshape(1, d), mod, w_qkv, wz, cos_t, sin_t)


def _logf_cumsum_kernel(z_ref, b_ref, o_ref):
    z = z_ref[...] + b_ref[...]
    lf = jnp.minimum(z, 0.0) - jnp.log1p(jnp.exp(-jnp.abs(z)))
    r = lax.broadcasted_iota(jnp.int32, (LANES, LANES), 0)
    c = lax.broadcasted_iota(jnp.int32, (LANES, LANES), 1)
    tri = jnp.where(c <= r, 1.0, 0.0).astype(F32)
    carry = jnp.zeros((1, z.shape[1]), F32)
    for ch in range(z.shape[0] // LANES):
        seg = lf[ch * LANES:(ch + 1) * LANES, :]
        cs = jnp.dot(tri, seg, preferred_element_type=F32, precision=lax.Precision.HIGHEST) + carry
        o_ref[ch * LANES:(ch + 1) * LANES, :] = cs
        carry = cs[LANES - 1:LANES, :]


def logf_cumsum(z, forget_b):
    b, s, h = z.shape
    return pl.pallas_call(
        _logf_cumsum_kernel,
        grid=(b,),
        in_specs=[pl.BlockSpec((None, s, h), lambda i: (i, 0, 0)),
                  pl.BlockSpec((1, h), lambda i: (0, 0))],
        out_specs=pl.BlockSpec((None, s, h), lambda i: (i, 0, 0)),
        out_shape=jax.ShapeDtypeStruct((b, s, h), F32),
        compiler_params=_params(("arbitrary",)),
        name="logf_cumsum",
    )(z, forget_b.reshape(1, h))


def _split3(f):
    hi = f.astype(BF16).astype(F32)
    r = f - hi
    mid = r.astype(BF16).astype(F32)
    lo = (r - mid).astype(BF16).astype(F32)
    return hi, mid, lo


def _fox_ext(cw, head, keys):
    n = cw.shape[1]
    mine = lax.broadcasted_iota(jnp.int32, cw.shape, 0) == head
    c = jnp.sum(jnp.where(mine, cw, 0.0), axis=0, keepdims=True) * LOG2E
    hi, mid, lo = _split3(c)
    r = lax.broadcasted_iota(jnp.int32, (8, n), 0)
    if keys:
        e8 = jnp.where(r < 3, 1.0, jnp.where(r == 3, -hi, jnp.where(
            r == 4, -mid, jnp.where(r == 5, -lo, 0.0))))
    else:
        e8 = jnp.where(r == 0, hi, jnp.where(r == 1, mid, jnp.where(
            r == 2, lo, jnp.where(r < 6, 1.0, 0.0))))
    return jnp.concatenate([e8, jnp.zeros((LANES - 8, n), F32)], axis=0).T


def _attn_kernel(*refs, mode, tq, tk, seq):
    if mode == "fox":
        q_ref, k_ref, v_ref, cw_ref, o_ref, kaug, vaug, qaug, s_a, s_b, bias_scr, m_scr, acc_scr = refs
    else:
        (q_ref, k_ref, v_ref, o_ref, kaug, vaug, qaug, s_a, s_b, bias_scr, m_scr, acc_scr,
         kmean_scr, kmean3_scr) = refs
    hg = pl.program_id(1)
    i = pl.program_id(2)
    n_moba_blocks = seq // MOBA_BLOCK
    nbp = -(-n_moba_blocks // 16) * 16
    group = q_ref.shape[1] // HEAD_DIM

    def head_cols(g):
        return slice(g * HEAD_DIM, (g + 1) * HEAD_DIM)

    @pl.when(jnp.logical_and(jnp.logical_and(pl.program_id(0) == 0, hg == 0), i == 0))
    def _():
        r2 = lax.broadcasted_iota(jnp.int32, (tk, tk), 0)
        c2 = lax.broadcasted_iota(jnp.int32, (tk, tk), 1)
        ok = c2 <= r2
        if mode == "moba":
            ok = jnp.logical_or(ok, c2 // MOBA_BLOCK != r2 // MOBA_BLOCK)
        bias_scr[...] = jnp.where(ok, 0.0, NEG_INF)

    @pl.when(i == 0)
    def _():
        lane = lax.broadcasted_iota(jnp.int32, (tk, LANES), 1)
        row = lax.broadcasted_iota(jnp.int32, (tk, LANES), 0)

        def build(cidx, carry):
            st = pl.multiple_of(cidx * tk, tk)
            for g in range(group):
                kaug[g, pl.ds(st, tk), 0:HEAD_DIM] = k_ref[pl.ds(st, tk), head_cols(g)]
                vaug[g, pl.ds(st, tk), 0:HEAD_DIM] = v_ref[pl.ds(st, tk), head_cols(g)]
                vaug[g, pl.ds(st, tk), HEAD_DIM:2 * HEAD_DIM] = jnp.ones((tk, HEAD_DIM), BF16)
                if mode == "fox":
                    ext = _fox_ext(cw_ref[:, pl.ds(st, tk)], hg * group + g, keys=True)
                else:
                    ext = jnp.where(lane == (st + row) // MOBA_BLOCK, 1.0, 0.0)
                kaug[g, pl.ds(st, tk), HEAD_DIM:2 * HEAD_DIM] = ext.astype(BF16)
            return carry

        lax.fori_loop(0, seq // tk, build, 0)
        if mode == "moba":
            kmean_scr[...] = jnp.zeros_like(kmean_scr)
            for g in range(group):
                for b in range(n_moba_blocks):
                    kb = k_ref[b * MOBA_BLOCK:(b + 1) * MOBA_BLOCK, head_cols(g)].astype(F32)
                    kmean_scr[g, b:b + 1, :] = jnp.mean(kb, axis=0, keepdims=True)
                for piece, part in enumerate(_split3(kmean_scr[g, 0:nbp, :])):
                    kmean3_scr[g, piece * nbp:(piece + 1) * nbp, :] = part.astype(BF16)

    for rb in range(tq // MOBA_BLOCK):
        rows = pl.ds(rb * MOBA_BLOCK, MOBA_BLOCK)
        for g in range(group):
            q = q_ref[rows, head_cols(g)]
            if mode == "fox":
                start = pl.multiple_of(i * tq + rb * MOBA_BLOCK, MOBA_BLOCK)
                ext = _fox_ext(cw_ref[:, pl.ds(start, MOBA_BLOCK)], hg * group + g, keys=False)
            else:
                g3 = lax.dot_general(kmean3_scr[g], q, (((1,), (1,)), ((), ())),
                                     preferred_element_type=F32)
                gate = g3[0:nbp] + g3[nbp:2 * nbp] + g3[2 * nbp:3 * nbp]
                own = i * (tq // MOBA_BLOCK) + rb
                blk = lax.broadcasted_iota(jnp.int32, gate.shape, 0)
                blk_f = blk.astype(F32)
                gv = jnp.where(blk < own, gate, NEG_INF)
                keep = blk == own
                for _ in range(min(MOBA_TOPK, n_moba_blocks)):
                    m = jnp.max(gv, axis=0, keepdims=True)
                    idx = jnp.min(jnp.where(gv == m, blk_f, float(LANES)), axis=0, keepdims=True)
                    hit = blk_f == idx
                    keep = jnp.logical_or(keep, jnp.logical_and(hit, m > NEG_INF))
                    gv = jnp.where(hit, NEG_INF, gv)
                ext_t = jnp.where(keep, 0.0, NEG_BIG)
                ext_t = jnp.concatenate(
                    [ext_t, jnp.zeros((LANES - nbp, MOBA_BLOCK), F32)], axis=0)
                ext = ext_t.T
            qaug[g, rows, 0:HEAD_DIM] = q
            qaug[g, rows, HEAD_DIM:2 * HEAD_DIM] = ext.astype(BF16)

    def scores(g, j):
        st = pl.multiple_of(j * tk, tk)
        return lax.dot_general(qaug[g], kaug[g, pl.ds(st, tk), :], (((1,), (1,)), ((), ())),
                               preferred_element_type=F32)

    def update(g, s, j, rows=slice(0, tq), causal=False):
        st = pl.multiple_of(j * tk, tk)
        if causal:
            s = s + bias_scr[...]
        m_old = m_scr[g, rows]
        m_new = jnp.maximum(m_old, jnp.max(s, axis=-1, keepdims=True))
        p = jnp.exp2(s - m_new).astype(BF16)
        pv = jnp.dot(p, vaug[g, pl.ds(st, tk), :], preferred_element_type=F32)
        acc_scr[g, rows] = jnp.exp2(m_old - m_new) * acc_scr[g, rows] + pv
        m_scr[g, rows] = m_new

    heads = range(group)
    m_scr[...] = jnp.full(m_scr.shape, NEG_INF, F32)
    acc_scr[...] = jnp.zeros(acc_scr.shape, F32)
    for g in heads:
        s_a[g] = scores(g, 0)

    def body(jj, carry):
        for g in heads:
            s_b[g] = scores(g, 2 * jj + 1)
        for g in heads:
            update(g, s_a[g], 2 * jj)
        for g in heads:
            s_a[g] = scores(g, 2 * jj + 2)
        for g in heads:
            update(g, s_b[g], 2 * jj + 1)
        return carry

    lax.fori_loop(0, i, body, 0)
    upper, lower = slice(0, tk), slice(tk, tq)
    for g in heads:
        st = pl.multiple_of((2 * i + 1) * tk, tk)
        s_b[g, upper] = lax.dot_general(qaug[g, lower], kaug[g, pl.ds(st, tk), :],
                                        (((1,), (1,)), ((), ())), preferred_element_type=F32)
    for g in heads:
        update(g, s_a[g, upper], 2 * i, upper, causal=True)
        update(g, s_a[g, lower], 2 * i, lower)
    for g in heads:
        update(g, s_b[g, upper], 2 * i + 1, lower, causal=True)
    for g in heads:
        o_ref[:, head_cols(g)] = (acc_scr[g, :, 0:HEAD_DIM]
                                  / acc_scr[g, :, HEAD_DIM:2 * HEAD_DIM]).astype(o_ref.dtype)


def attention(proj, batch, seq, heads, q_col, k_col, v_col, cum=None, tq=1024, group=2):
    n = proj.shape[0]
    tq = _tile(seq, tq)
    tk = tq // 2
    nq = seq // tq
    mode = "moba" if cum is None else "fox"
    assert tk % MOBA_BLOCK == 0 and seq // MOBA_BLOCK <= LANES
    gw = group * HEAD_DIM
    assert heads % group == 0 and q_col % group == 0 and k_col % group == 0 and v_col % group == 0
    in_specs = [
        pl.BlockSpec((tq, gw), lambda b, h, i: (b * nq + i, q_col // group + h)),
        pl.BlockSpec((seq, gw), lambda b, h, i: (b, k_col // group + h)),
        pl.BlockSpec((seq, gw), lambda b, h, i: (b, v_col // group + h)),
    ]
    args = [proj, proj, proj]
    wide = 2 * HEAD_DIM
    scratch = [pltpu.VMEM((group, seq, wide), BF16), pltpu.VMEM((group, seq, wide), BF16),
               pltpu.VMEM((group, tq, wide), BF16),
               pltpu.VMEM((group, tq, tk), F32), pltpu.VMEM((group, tq, tk), F32),
               pltpu.VMEM((tk, tk), F32),
               pltpu.VMEM((group, tq, 1), F32), pltpu.VMEM((group, tq, wide), F32)]
    if mode == "fox":
        in_specs.append(pl.BlockSpec((None, heads, seq), lambda b, h, i: (b, 0, 0)))
        args.append(cum)
    else:
        scratch += [pltpu.VMEM((group, LANES, HEAD_DIM), F32),
                    pltpu.VMEM((group, 3 * (-(-(seq // MOBA_BLOCK) // 16) * 16), HEAD_DIM), BF16)]
    return pl.pallas_call(
        functools.partial(_attn_kernel, mode=mode, tq=tq, tk=tk, seq=seq),
        grid=(batch, heads // group, nq),
        in_specs=in_specs,
        out_specs=pl.BlockSpec((tq, gw), lambda b, h, i: (b * nq + i, h)),
        out_shape=jax.ShapeDtypeStruct((n, heads * HEAD_DIM), BF16),
        scratch_shapes=scratch,
        compiler_params=_params(("arbitrary", "arbitrary", "arbitrary")),
        name=mode + "_attention",
    )(*args)


def _out_proj_kernel(*refs, gate_row, sh_row, n_experts):
    if n_experts:
        (mo_ref, fo_ref, g1_ref, g2_ref, w_ref, x_ref, mod_ref, gf_ref, wr_ref,
         o_ref, h_ref, route_ref) = refs
    else:
        mo_ref, fo_ref, g1_ref, g2_ref, w_ref, x_ref, mod_ref, gf_ref, o_ref, h_ref = refs
    half = mo_ref.shape[1]
    m1 = _rms(mo_ref[...].astype(F32), g1_ref[...]).astype(BF16)
    m2 = _rms(fo_ref[...].astype(F32), g2_ref[...]).astype(BF16)
    acc = jnp.dot(m1, w_ref[0:half, :], preferred_element_type=F32)
    acc = acc + jnp.dot(m2, w_ref[half:2 * half, :], preferred_element_type=F32)
    x_new = x_ref[...] + mod_ref[gate_row:gate_row + 1, :] * acc
    o_ref[...] = x_new
    sh = mod_ref[sh_row:sh_row + 1, :]
    sc = mod_ref[sh_row + 1:sh_row + 2, :]
    h = _rms(x_new, gf_ref[...]) * (1.0 + sc) + sh
    h_hi = h.astype(BF16)
    h_ref[...] = h_hi
    if n_experts:
        h_lo = (h - h_hi.astype(F32)).astype(BF16)
        l_hi = jnp.dot(h_hi, wr_ref[...], preferred_element_type=F32)
        l_lo = jnp.dot(h_lo, wr_ref[:, 0:LANES], preferred_element_type=F32)
        logits = l_hi[:, 0:LANES] + l_hi[:, LANES:2 * LANES] + l_lo
        route_ref[...] = _top2_route(logits, n_experts)


def out_proj(moba_o, fox_o, g1, g2, w_o, x2, mod, ffn_g, seq, gate_row, sh_row, router_w=None, tm=512):
    n, d = x2.shape
    half = moba_o.shape[1]
    tm = _tile(seq, tm)
    per_b = seq // tm
    n_experts = 0 if router_w is None else router_w.shape[1]
    in_specs = [
        pl.BlockSpec((tm, half), lambda i: (i, 0)),
        pl.BlockSpec((tm, half), lambda i: (i, 0)),
        pl.BlockSpec((1, half), lambda i: (0, 0)),
        pl.BlockSpec((1, half), lambda i: (0, 0)),
        pl.BlockSpec((2 * half, d), lambda i: (0, 0)),
        pl.BlockSpec((tm, d), lambda i: (i, 0)),
        pl.BlockSpec((None, 6, d), lambda i: (i // per_b, 0, 0)),
        pl.BlockSpec((1, d), lambda i: (0, 0)),
    ]
    args = [moba_o, fox_o, g1.reshape(1, half), g2.reshape(1, half), w_o, x2, mod, ffn_g.reshape(1, d)]
    out_specs = [pl.BlockSpec((tm, d), lambda i: (i, 0)), pl.BlockSpec((tm, d), lambda i: (i, 0))]
    out_shape = [jax.ShapeDtypeStruct((n, d), F32), jax.ShapeDtypeStruct((n, d), BF16)]
    if n_experts:
        in_specs.append(pl.BlockSpec((d, 2 * LANES), lambda i: (0, 0)))
        w_hi = router_w.astype(BF16)
        w_lo = (router_w - w_hi.astype(F32)).astype(BF16)
        args.append(jnp.zeros((d, 2 * LANES), BF16).at[:, :n_experts].set(w_hi)
                    .at[:, LANES:LANES + n_experts].set(w_lo))
        out_specs.append(pl.BlockSpec((tm, LANES), lambda i: (i, 0)))
        out_shape.append(jax.ShapeDtypeStruct((n, LANES), F32))
    return pl.pallas_call(
        functools.partial(_out_proj_kernel, gate_row=gate_row, sh_row=sh_row, n_experts=n_experts),
        grid=(n // tm,),
        in_specs=in_specs,
        out_specs=out_specs,
        out_shape=out_shape,
        compiler_params=_params(("arbitrary",)),
        name="out_proj",
    )(*args)


def _swiglu_tile(h, wg, wu):
    a = jnp.dot(h, wg, preferred_element_type=F32)
    b = jnp.dot(h, wu, preferred_element_type=F32)
    return a * jax.nn.sigmoid(a) * b


def _ffn_kernel(h_ref, wg_ref, wu_ref, wd_ref, x_ref, mod_ref, o_ref, *, gate_row):
    f = pl.program_id(1)

    @pl.when(f == 0)
    def _():
        o_ref[...] = jnp.zeros(o_ref.shape, o_ref.dtype)

    t = _swiglu_tile(h_ref[...], wg_ref[...], wu_ref[...]).astype(BF16)
    o_ref[...] += jnp.dot(t, wd_ref[...], preferred_element_type=F32)

    @pl.when(f == pl.num_programs(1) - 1)
    def _():
        o_ref[...] = x_ref[...] + mod_ref[gate_row:gate_row + 1, :] * o_ref[...]


def ffn_dense(h, wg, wu, wd, x2, mod, seq, gate_row, tm=1024, tf=512):
    n, d = x2.shape
    ff = wg.shape[1]
    tm = _tile(seq, tm)
    tf = _tile(ff, tf)
    per_b = seq // tm
    return pl.pallas_call(
        functools.partial(_ffn_kernel, gate_row=gate_row),
        grid=(n // tm, ff // tf),
        in_specs=[
            pl.BlockSpec((tm, d), lambda i, f: (i, 0)),
            pl.BlockSpec((d, tf), lambda i, f: (0, f)),
            pl.BlockSpec((d, tf), lambda i, f: (0, f)),
            pl.BlockSpec((tf, d), lambda i, f: (f, 0)),
            pl.BlockSpec((tm, d), lambda i, f: (i, 0)),
            pl.BlockSpec((None, 6, d), lambda i, f: (i // per_b, 0, 0)),
        ],
        out_specs=pl.BlockSpec((tm, d), lambda i, f: (i, 0)),
        out_shape=jax.ShapeDtypeStruct((n, d), F32),
        compiler_params=_params(("arbitrary", "arbitrary")),
        name="ffn_dense",
    )(h, wg, wu, wd, x2, mod)


def _route_plan(route, n_experts, tm):
    n = route.shape[0]
    pairs = TOP_K * n
    e_flat = route[:, :TOP_K].astype(jnp.int32).reshape(pairs)
    onehot = (e_flat[:, None] == jnp.arange(n_experts, dtype=jnp.int32)[None, :]).astype(jnp.int32)
    csum = jnp.cumsum(onehot, axis=0)
    counts = csum[-1]
    padded = (counts + tm - 1) // tm * tm
    pend = jnp.cumsum(padded)
    pstart = pend - padded
    rank = jnp.sum(csum * onehot, axis=1) - 1
    pos = (jnp.sum(pstart[None, :] * onehot, axis=1) + rank).astype(jnp.int32)

    n_tiles = pairs // tm + n_experts
    first_row = jnp.arange(n_tiles, dtype=jnp.int32) * tm
    tile_expert = jnp.minimum(jnp.sum((pend[None, :] <= first_row[:, None]).astype(jnp.int32), axis=1),
                              n_experts - 1)
    valid = first_row < pend[-1]
    return pos, (tile_expert.astype(jnp.int32), valid.astype(jnp.int32),
                 (pstart + counts).astype(jnp.int32))


def _dispatch_kernel(pos_ref, h_ref, init_ref, xs_ref, sem):
    del init_ref
    i = pl.program_id(0)
    tm = h_ref.shape[0]

    def issue(r, carry):
        for k in range(TOP_K):
            dst = pos_ref[(i * tm + r) * TOP_K + k]
            pltpu.make_async_copy(h_ref.at[r], xs_ref.at[dst], sem).start()
        return carry

    lax.fori_loop(0, tm, issue, 0)
    for k in range(TOP_K):
        pltpu.make_async_copy(h_ref, xs_ref.at[pl.ds(0, tm)], sem).wait()


def moe_dispatch(h3, pos, rows, tm=2048):
    n, sub, lanes = h3.shape
    tm = _tile(n, tm)
    return pl.pallas_call(
        _dispatch_kernel,
        grid_spec=pltpu.PrefetchScalarGridSpec(
            num_scalar_prefetch=1,
            grid=(n // tm,),
            in_specs=[pl.BlockSpec((tm, sub, lanes), lambda i, pos: (i, 0, 0)),
                      pl.BlockSpec(memory_space=pl.ANY)],
            out_specs=pl.BlockSpec(memory_space=pl.ANY),
            scratch_shapes=[pltpu.SemaphoreType.DMA(())],
        ),
        out_shape=jax.ShapeDtypeStruct((rows, sub, lanes), h3.dtype),
        input_output_aliases={2: 0},
        compiler_params=_params(("arbitrary",), disable_bounds_checks=True),
        name="moe_dispatch",
    )(pos, h3, jnp.zeros((rows, sub, lanes), h3.dtype))


def _moe_ffn_kernel(exp_ref, valid_ref, rowend_ref, xs_ref, wg_ref, wu_ref, wd_ref, o_ref, *, sub):
    t = pl.program_id(0)
    f = pl.program_id(1)
    tm = xs_ref.shape[0]

    @pl.when(valid_ref[t] == 1)
    def _():
        rows = t * tm + lax.broadcasted_iota(jnp.int32, (tm, 1), 0)
        real = rows < rowend_ref[exp_ref[t]]

        @pl.when(f == 0)
        def _():
            o_ref[...] = jnp.zeros(o_ref.shape, o_ref.dtype)

        for c in range(wg_ref.shape[1] // sub):
            cols = slice(c * sub, (c + 1) * sub)
            a = _swiglu_tile(xs_ref[...], wg_ref[:, cols].astype(BF16), wu_ref[:, cols].astype(BF16))
            a = jnp.where(real, a, 0.0).astype(BF16)
            o_ref[...] += jnp.dot(a, wd_ref[cols, :].astype(BF16), preferred_element_type=F32)

    @pl.when(jnp.logical_and(valid_ref[t] == 0, f == 0))
    def _():
        o_ref[...] = jnp.zeros(o_ref.shape, o_ref.dtype)


def moe_ffn(xs, plan, wg, wu, wd, tm, tf=512, sub=256):
    rows, d = xs.shape
    n_experts, _, ff = wg.shape
    tf = _tile(ff, tf)
    sub = _tile(tf, sub)
    nf = ff // tf

    def f_idx(t, f, va):
        return jnp.where(va[t] == 1, f, nf - 1)

    return pl.pallas_call(
        functools.partial(_moe_ffn_kernel, sub=sub),
        grid_spec=pltpu.PrefetchScalarGridSpec(
            num_scalar_prefetch=3,
            grid=(rows // tm, nf),
            in_specs=[
                pl.BlockSpec((tm, d), lambda t, f, ex, va, re: (t, 0)),
                pl.BlockSpec((None, d, tf), lambda t, f, ex, va, re: (ex[t], 0, f_idx(t, f, va))),
                pl.BlockSpec((None, d, tf), lambda t, f, ex, va, re: (ex[t], 0, f_idx(t, f, va))),
                pl.BlockSpec((None, tf, d), lambda t, f, ex, va, re: (ex[t], f_idx(t, f, va), 0)),
            ],
            out_specs=pl.BlockSpec((tm, d), lambda t, f, ex, va, re: (t, 0)),
        ),
        out_shape=jax.ShapeDtypeStruct((rows, d), F32),
        compiler_params=_params(("arbitrary", "arbitrary")),
        name="moe_ffn",
    )(*plan, xs, wg, wu, wd)


def _combine_kernel(pos_ref, ys_ref, route_ref, x_ref, mod_ref, g_ref, o_ref, buf, sem,
                    *, gate_row, final):
    i = pl.program_id(0)
    tm = x_ref.shape[0]
    slot = i % 2

    def gather(step, into):
        def issue(r, carry):
            for k in range(TOP_K):
                src = pos_ref[(step * tm + r) * TOP_K + k]
                pltpu.make_async_copy(ys_ref.at[pl.ds(src, 1), :], buf.at[into, k, pl.ds(r, 1), :],
                                      sem.at[into]).start()
            return carry

        lax.fori_loop(0, tm, issue, 0, unroll=2)

    @pl.when(i == 0)
    def _():
        gather(0, 0)

    more = i + 1 < pl.num_programs(0)
    for parity in range(2):
        @pl.when(jnp.logical_and(more, slot == parity))
        def _(parity=parity):
            gather(i + 1, 1 - parity)

    for k in range(TOP_K):
        pltpu.make_async_copy(ys_ref.at[pl.ds(0, tm), :], buf.at[slot, k], sem.at[slot]).wait()
    route = route_ref[...]
    y = buf[slot, 0] * route[:, TOP_K:TOP_K + 1]
    for k in range(1, TOP_K):
        y = y + buf[slot, k] * route[:, TOP_K + k:TOP_K + k + 1]
    x_new = x_ref[...] + mod_ref[gate_row:gate_row + 1, :] * y
    o_ref[...] = _rms(x_new, g_ref[...]) if final else x_new


def moe_combine(ys, pos, route, x2, mod, seq, gate_row, final_g=None, tm=512):
    n, d = x2.shape
    tm = _tile(seq, tm)
    per_b = seq // tm
    g = jnp.ones((1, d), F32) if final_g is None else final_g.reshape(1, d)
    return pl.pallas_call(
        functools.partial(_combine_kernel, gate_row=gate_row, final=final_g is not None),
        grid_spec=pltpu.PrefetchScalarGridSpec(
            num_scalar_prefetch=1,
            grid=(n // tm,),
            in_specs=[
                pl.BlockSpec(memory_space=pl.ANY),
                pl.BlockSpec((tm, LANES), lambda i, pos: (i, 0)),
                pl.BlockSpec((tm, d), lambda i, pos: (i, 0)),
                pl.BlockSpec((None, 6, d), lambda i, pos: (i // per_b, 0, 0)),
                pl.BlockSpec((1, d), lambda i, pos: (0, 0)),
            ],
            out_specs=pl.BlockSpec((tm, d), lambda i, pos: (i, 0)),
            scratch_shapes=[pltpu.VMEM((2, TOP_K, tm, d), F32), pltpu.SemaphoreType.DMA((2,))],
        ),
        out_shape=jax.ShapeDtypeStruct((n, d), F32),
        compiler_params=_params(("arbitrary",), disable_bounds_checks=True),
        name="moe_combine",
    )(pos, ys, route, x2, mod, g)


def ffn_moe(h, route, wg, wu, wd, x2, mod, seq, gate_row, final_g=None, tm=1024):
    n, d = x2.shape
    n_experts = wg.shape[0]
    tm = _tile(TOP_K * n, tm)
    pos, plan = _route_plan(route, n_experts, tm)
    rows = TOP_K * n + n_experts * tm
    xs3 = moe_dispatch(h.reshape(n, d // LANES, LANES), pos, rows)
    ys = moe_ffn(xs3.reshape(rows, d), plan, wg, wu, wd, tm)
    return moe_combine(ys, pos, route, x2, mod, seq, gate_row, final_g)


def _rope_tables(seq):
    f32 = np.float32
    inv = np.power(f32(ROPE_THETA), -np.arange(0, HEAD_DIM, 2, dtype=f32) / f32(HEAD_DIM)).astype(f32)
    ang = np.arange(seq, dtype=f32)[:, None] * inv[None, :]
    cos, sin = np.cos(ang).astype(f32), np.sin(ang).astype(f32)
    return (jnp.asarray(np.concatenate([cos, cos], axis=-1)),
            jnp.asarray(np.concatenate([-sin, sin], axis=-1)))


def kernel(x, c, attn_norm_g, ada_w, ada_b, w_in, forget_b, moba_out_g, fox_out_g, w_o, ffn_norm_g,
           dense_w_gate, dense_w_up, dense_w_down, router_w, moe_w_gate, moe_w_up, moe_w_down,
           final_norm_g):
    batch, seq, d = x.shape
    depth = attn_norm_g.shape[0]
    moba_w = moba_out_g.shape[1]
    fox_w = fox_out_g.shape[1]
    moba_h = moba_w // HEAD_DIM
    fox_h = fox_w // HEAD_DIM
    qkv_w = 3 * moba_w + 3 * fox_w
    assert moba_w == fox_w
    kinds = (ROPE_SCALED, ROPE, PLAIN, SCALED, PLAIN, PLAIN)
    hb = HEAD_DIM

    cos_t, sin_t = _rope_tables(seq)
    mods = ada_modulation(c, ada_w, ada_b)
    x2 = x.reshape(batch * seq, d)
    out = None

    for l in range(depth):
        mod = mods[l]
        w_qkv = w_in[l][:, :qkv_w].astype(BF16)
        wz = jnp.zeros((d, LANES), BF16).at[:, :fox_h].set(w_in[l][:, qkv_w:].astype(BF16))
        proj, fz = in_proj(x2, attn_norm_g[l], mod, w_qkv, wz, cos_t, sin_t, seq, kinds,
                           unit=moba_w, sh_row=0)
        cum = logf_cumsum(fz[:, :fox_h].reshape(batch, seq, fox_h), forget_b[l])
        cum = cum.transpose(0, 2, 1)
        moba_o = attention(proj, batch, seq, moba_h, q_col=0, k_col=moba_w // hb,
                           v_col=2 * moba_w // hb)
        fc = 3 * moba_w // hb
        fox_o = attention(proj, batch, seq, fox_h, q_col=fc, k_col=fc + fox_w // hb,
                          v_col=fc + 2 * fox_w // hb, cum=cum)
        i = l // 2
        last = l == depth - 1
        w_o_l = w_o[l].astype(BF16)
        if l % 2 == 0:
            x2, h = out_proj(moba_o, fox_o, moba_out_g[l], fox_out_g[l], w_o_l, x2, mod,
                             ffn_norm_g[l], seq, gate_row=2, sh_row=3)
            x2 = ffn_dense(h, dense_w_gate[i].astype(BF16), dense_w_up[i].astype(BF16),
                           dense_w_down[i].astype(BF16), x2, mod, seq, gate_row=5)
        else:
            x2, h, route = out_proj(moba_o, fox_o, moba_out_g[l], fox_out_g[l], w_o_l, x2, mod,
                                    ffn_norm_g[l], seq, gate_row=2, sh_row=3, router_w=router_w[i])
            x2 = ffn_moe(h, route, moe_w_gate[i], moe_w_up[i], moe_w_down[i], x2, mod, seq,
                         gate_row=5, final_g=final_norm_g if last else None)
            if last:
                out = x2
    if out is None:
        out = final_norm(x2, final_norm_g)
    return out.reshape(batch, seq, d)
```

```python
import functools
import math

import jax
import jax.numpy as jnp
import numpy as np
from jax import lax
from jax.experimental import pallas as pl
from jax.experimental.pallas import tpu as pltpu

F32 = jnp.float32
BF16 = jnp.bfloat16

HEAD_DIM = 128
MOBA_BLOCK = 256
MOBA_TOPK = 3
ROPE_THETA = 10000.0
EPS = 1e-6
TOP_K = 2
LANES = 128
VMEM_LIMIT = 60 * 1024 * 1024

NEG_INF = float("-inf")
NEG_BIG = -(2.0 ** 126)
LOG2E = math.log2(math.e)
Q_SCALE = HEAD_DIM ** -0.5 * LOG2E


def _params(semantics, **kw):
    return pltpu.CompilerParams(dimension_semantics=semantics, vmem_limit_bytes=VMEM_LIMIT, **kw)


def _tile(n, want):
    t = min(n, want)
    assert n % t == 0, (n, want)
    return t


def _ada_kernel(c_ref, w_ref, b_ref, o_ref):
    half = c_ref.shape[0] // 2
    c = c_ref[...]
    act = c * jax.nn.sigmoid(c)
    hi = act.astype(BF16)
    lo = (act - hi.astype(F32)).astype(BF16)
    upper = lax.broadcasted_iota(jnp.int32, act.shape, 0) >= half
    lhs = jnp.where(upper, lo, hi)
    w = w_ref[...]
    w_hi = w.astype(BF16)
    w_lo = (w - w_hi.astype(F32)).astype(BF16)
    r_hi = jnp.dot(lhs, w_hi, preferred_element_type=F32)
    r_lo = jnp.dot(lhs, w_lo, preferred_element_type=F32)
    o_ref[...] = r_hi + pltpu.roll(r_hi, half, 0) + r_lo + b_ref[...]


def ada_modulation(c, ada_w, ada_b):
    n_layers, d, d6 = ada_w.shape
    b = c.shape[0]
    half = -(-b // 8) * 8
    rows = 2 * half
    c_pad = jnp.zeros((rows, d), F32).at[:b].set(c).at[half:half + b].set(c)
    tn = _tile(d6, 1024)
    out = pl.pallas_call(
        _ada_kernel,
        grid=(n_layers, d6 // tn),
        in_specs=[
            pl.BlockSpec((rows, d), lambda l, j: (0, 0)),
            pl.BlockSpec((None, d, tn), lambda l, j: (l, 0, j)),
            pl.BlockSpec((None, 1, tn), lambda l, j: (l, 0, j)),
        ],
        out_specs=pl.BlockSpec((None, rows, tn), lambda l, j: (l, 0, j)),
        out_shape=jax.ShapeDtypeStruct((n_layers, rows, d6), F32),
        compiler_params=_params(("arbitrary", "arbitrary")),
        name="ada_modulation",
    )(c_pad, ada_w, ada_b.reshape(n_layers, 1, d6))
    return out[:, :b].reshape(n_layers, b, 6, d)


def _rms(x, g):
    return x * lax.rsqrt(jnp.mean(x * x, axis=-1, keepdims=True) + EPS) * g


def _top2_route(logits, n_experts):
    lane = lax.broadcasted_iota(jnp.int32, logits.shape, 1).astype(F32)
    lg = jnp.where(lane < n_experts, logits, NEG_INF)
    m1 = jnp.max(lg, axis=-1, keepdims=True)
    i1 = jnp.min(jnp.where(lg == m1, lane, float(LANES)), axis=-1, keepdims=True)
    lg2 = jnp.where(lane == i1, NEG_INF, lg)
    m2 = jnp.max(lg2, axis=-1, keepdims=True)
    i2 = jnp.min(jnp.where(lg2 == m2, lane, float(LANES)), axis=-1, keepdims=True)
    e2 = jnp.exp(m2 - m1)
    w1 = 1.0 / (1.0 + e2)
    w2 = e2 / (1.0 + e2)
    return jnp.where(lane == 0.0, i1, jnp.where(lane == 1.0, i2,
                     jnp.where(lane == 2.0, w1, jnp.where(lane == 3.0, w2, 0.0))))


def _final_norm_kernel(x_ref, g_ref, o_ref):
    o_ref[...] = _rms(x_ref[...], g_ref[...])


def final_norm(x2, g, tm=512):
    n, d = x2.shape
    tm = _tile(n, tm)
    return pl.pallas_call(
        _final_norm_kernel,
        grid=(n // tm,),
        in_specs=[pl.BlockSpec((tm, d), lambda i: (i, 0)),
                  pl.BlockSpec((1, d), lambda i: (0, 0))],
        out_specs=pl.BlockSpec((tm, d), lambda i: (i, 0)),
        out_shape=jax.ShapeDtypeStruct((n, d), F32),
        compiler_params=_params(("arbitrary",)),
        name="final_norm",
    )(x2, g.reshape(1, d))


PLAIN, ROPE, ROPE_SCALED, SCALED = 0, 1, 2, 3


def _in_proj_kernel(x_ref, g_ref, mod_ref, w_ref, wz_ref, cos_ref, sin_ref, o_ref, fz_ref, h_scr,
                    *, kinds, unit, sh_row):
    j = pl.program_id(1)
    tn = w_ref.shape[1]

    @pl.when(j == 0)
    def _():
        y = _rms(x_ref[...], g_ref[...])
        sh = mod_ref[sh_row:sh_row + 1, :]
        sc = mod_ref[sh_row + 1:sh_row + 2, :]
        h_scr[...] = (y * (1.0 + sc) + sh).astype(h_scr.dtype)
        fz_ref[...] = jnp.dot(h_scr[...], wz_ref[...], preferred_element_type=F32)

    per_tile = tn // unit
    tile_kinds = [tuple(kinds[t * per_tile:(t + 1) * per_tile]) for t in range(len(kinds) // per_tile)]

    def is_tile_kind(tk_):
        hits = [jj for jj, k in enumerate(tile_kinds) if k == tk_]
        cond = j == hits[0]
        for jj in hits[1:]:
            cond = jnp.logical_or(cond, j == jj)
        return cond

    for tk_ in sorted(set(tile_kinds)):
        @pl.when(is_tile_kind(tk_))
        def _(tk_=tk_):
            acc = jnp.dot(h_scr[...], w_ref[...], preferred_element_type=F32)
            for u, kind in enumerate(tk_):
                mult = Q_SCALE if kind in (ROPE_SCALED, SCALED) else 1.0
                if kind in (ROPE, ROPE_SCALED):
                    c = cos_ref[...] * mult
                    s = sin_ref[...] * mult
                    for hh in range(unit // HEAD_DIM):
                        cols = slice(u * unit + hh * HEAD_DIM, u * unit + (hh + 1) * HEAD_DIM)
                        t = acc[:, cols]
                        rot = pltpu.roll(t, HEAD_DIM // 2, 1)
                        o_ref[:, cols] = (t * c + rot * s).astype(o_ref.dtype)
                else:
                    cols = slice(u * unit, (u + 1) * unit)
                    o_ref[:, cols] = (acc[:, cols] * mult).astype(o_ref.dtype)


def in_proj(x2, g, mod, w_qkv, wz, cos_t, sin_t, seq, kinds, unit, sh_row, tm=1024, units_per_tile=2):
    n, d = x2.shape
    width = w_qkv.shape[1]
    assert width == unit * len(kinds) and len(kinds) % units_per_tile == 0
    tn = unit * units_per_tile
    tm = _tile(seq, tm)
    per_b = seq // tm
    return pl.pallas_call(
        functools.partial(_in_proj_kernel, kinds=tuple(kinds), unit=unit, sh_row=sh_row),
        grid=(n // tm, width // tn),
        in_specs=[
            pl.BlockSpec((tm, d), lambda i, j: (i, 0)),
            pl.BlockSpec((1, d), lambda i, j: (0, 0)),
            pl.BlockSpec((None, 6, d), lambda i, j: (i // per_b, 0, 0)),
            pl.BlockSpec((d, tn), lambda i, j: (0, j)),
            pl.BlockSpec((d, LANES), lambda i, j: (0, 0)),
            pl.BlockSpec((tm, HEAD_DIM), lambda i, j: (i % per_b, 0)),
            pl.BlockSpec((tm, HEAD_DIM), lambda i, j: (i % per_b, 0)),
        ],
        out_specs=[pl.BlockSpec((tm, tn), lambda i, j: (i, j)),
                   pl.BlockSpec((tm, LANES), lambda i, j: (i, 0))],
        out_shape=[jax.ShapeDtypeStruct((n, width), BF16),
                   jax.ShapeDtypeStruct((n, LANES), F32)],
        scratch_shapes=[pltpu.VMEM((tm, d), BF16)],
        compiler_params=_params(("arbitrary", "arbitrary")),
        name="in_proj",
    )(x2, g.reshape(1, d), mod, w_qkv, wz, cos_t, sin_t)


def _logf_cumsum_kernel(z_ref, b_ref, o_ref):
    z = z_ref[...] + b_ref[...]
    lf = jnp.minimum(z, 0.0) - jnp.log1p(jnp.exp(-jnp.abs(z)))
    r = lax.broadcasted_iota(jnp.int32, (LANES, LANES), 0)
    c = lax.broadcasted_iota(jnp.int32, (LANES, LANES), 1)
    tri = jnp.where(c <= r, 1.0, 0.0).astype(F32)
    carry = jnp.zeros((1, z.shape[1]), F32)
    for ch in range(z.shape[0] // LANES):
        seg = lf[ch * LANES:(ch + 1) * LANES, :]
        cs = jnp.dot(tri, seg, preferred_element_type=F32, precision=lax.Precision.HIGHEST) + carry
        o_ref[ch * LANES:(ch + 1) * LANES, :] = cs
        carry = cs[LANES - 1:LANES, :]


def logf_cumsum(z, forget_b):
    b, s, h = z.shape
    return pl.pallas_call(
        _logf_cumsum_kernel,
        grid=(b,),
        in_specs=[pl.BlockSpec((None, s, h), lambda i: (i, 0, 0)),
                  pl.BlockSpec((1, h), lambda i: (0, 0))],
        out_specs=pl.BlockSpec((None, s, h), lambda i: (i, 0, 0)),
        out_shape=jax.ShapeDtypeStruct((b, s, h), F32),
        compiler_params=_params(("arbitrary",)),
        name="logf_cumsum",
    )(z, forget_b.reshape(1, h))


def _split3(f):
    hi = f.astype(BF16).astype(F32)
    r = f - hi
    mid = r.astype(BF16).astype(F32)
    lo = (r - mid).astype(BF16).astype(F32)
    return hi, mid, lo


def _fox_ext(cw, head, keys):
    n = cw.shape[1]
    mine = lax.broadcasted_iota(jnp.int32, cw.shape, 0) == head
    c = jnp.sum(jnp.where(mine, cw, 0.0), axis=0, keepdims=True) * LOG2E
    hi, mid, lo = _split3(c)
    r = lax.broadcasted_iota(jnp.int32, (8, n), 0)
    if keys:
        e8 = jnp.where(r < 3, 1.0, jnp.where(r == 3, -hi, jnp.where(
            r == 4, -mid, jnp.where(r == 5, -lo, 0.0))))
    else:
        e8 = jnp.where(r == 0, hi, jnp.where(r == 1, mid, jnp.where(
            r == 2, lo, jnp.where(r < 6, 1.0, 0.0))))
    return jnp.concatenate([e8, jnp.zeros((LANES - 8, n), F32)], axis=0).T


def _attn_kernel(*refs, mode, tq, tk, seq):
    if mode == "fox":
        q_ref, k_ref, v_ref, cw_ref, o_ref, kaug, vaug, qaug, s_a, s_b, bias_scr, m_scr, acc_scr = refs
    else:
        (q_ref, k_ref, v_ref, o_ref, kaug, vaug, qaug, s_a, s_b, bias_scr, m_scr, acc_scr,
         kmean_scr, kmean3_scr) = refs
    hg = pl.program_id(1)
    i = pl.program_id(2)
    n_moba_blocks = seq // MOBA_BLOCK
    nbp = -(-n_moba_blocks // 16) * 16
    group = q_ref.shape[1] // HEAD_DIM

    def head_cols(g):
        return slice(g * HEAD_DIM, (g + 1) * HEAD_DIM)

    @pl.when(jnp.logical_and(jnp.logical_and(pl.program_id(0) == 0, hg == 0), i == 0))
    def _():
        r2 = lax.broadcasted_iota(jnp.int32, (tk, tk), 0)
        c2 = lax.broadcasted_iota(jnp.int32, (tk, tk), 1)
        ok = c2 <= r2
        if mode == "moba":
            ok = jnp.logical_or(ok, c2 // MOBA_BLOCK != r2 // MOBA_BLOCK)
        bias_scr[...] = jnp.where(ok, 0.0, NEG_INF)

    @pl.when(i == 0)
    def _():
        lane = lax.broadcasted_iota(jnp.int32, (tk, LANES), 1)
        row = lax.broadcasted_iota(jnp.int32, (tk, LANES), 0)

        def build(cidx, carry):
            st = pl.multiple_of(cidx * tk, tk)
            for g in range(group):
                kaug[g, pl.ds(st, tk), 0:HEAD_DIM] = k_ref[pl.ds(st, tk), head_cols(g)]
                vaug[g, pl.ds(st, tk), 0:HEAD_DIM] = v_ref[pl.ds(st, tk), head_cols(g)]
                vaug[g, pl.ds(st, tk), HEAD_DIM:2 * HEAD_DIM] = jnp.ones((tk, HEAD_DIM), BF16)
                if mode == "fox":
                    ext = _fox_ext(cw_ref[:, pl.ds(st, tk)], hg * group + g, keys=True)
                else:
                    ext = jnp.where(lane == (st + row) // MOBA_BLOCK, 1.0, 0.0)
                kaug[g, pl.ds(st, tk), HEAD_DIM:2 * HEAD_DIM] = ext.astype(BF16)
            return carry

        lax.fori_loop(0, seq // tk, build, 0)
        if mode == "moba":
            kmean_scr[...] = jnp.zeros_like(kmean_scr)
            for g in range(group):
                for b in range(n_moba_blocks):
                    kb = k_ref[b * MOBA_BLOCK:(b + 1) * MOBA_BLOCK, head_cols(g)].astype(F32)
                    kmean_scr[g, b:b + 1, :] = jnp.mean(kb, axis=0, keepdims=True)
                for piece, part in enumerate(_split3(kmean_scr[g, 0:nbp, :])):
                    kmean3_scr[g, piece * nbp:(piece + 1) * nbp, :] = part.astype(BF16)

    for rb in range(tq // MOBA_BLOCK):
        rows = pl.ds(rb * MOBA_BLOCK, MOBA_BLOCK)
        for g in range(group):
            q = q_ref[rows, head_cols(g)]
            if mode == "fox":
                start = pl.multiple_of(i * tq + rb * MOBA_BLOCK, MOBA_BLOCK)
                ext = _fox_ext(cw_ref[:, pl.ds(start, MOBA_BLOCK)], hg * group + g, keys=False)
            else:
                g3 = lax.dot_general(kmean3_scr[g], q, (((1,), (1,)), ((), ())),
                                     preferred_element_type=F32)
                gate = g3[0:nbp] + g3[nbp:2 * nbp] + g3[2 * nbp:3 * nbp]
                own = i * (tq // MOBA_BLOCK) + rb
                blk = lax.broadcasted_iota(jnp.int32, gate.shape, 0)
                blk_f = blk.astype(F32)
                gv = jnp.where(blk < own, gate, NEG_INF)
                keep = blk == own
                for _ in range(min(MOBA_TOPK, n_moba_blocks)):
                    m = jnp.max(gv, axis=0, keepdims=True)
                    idx = jnp.min(jnp.where(gv == m, blk_f, float(LANES)), axis=0, keepdims=True)
                    hit = blk_f == idx
                    keep = jnp.logical_or(keep, jnp.logical_and(hit, m > NEG_INF))
                    gv = jnp.where(hit, NEG_INF, gv)
                ext_t = jnp.where(keep, 0.0, NEG_BIG)
                ext_t = jnp.concatenate(
                    [ext_t, jnp.zeros((LANES - nbp, MOBA_BLOCK), F32)], axis=0)
                ext = ext_t.T
            qaug[g, rows, 0:HEAD_DIM] = q
            qaug[g, rows, HEAD_DIM:2 * HEAD_DIM] = ext.astype(BF16)

    def scores(g, j):
        st = pl.multiple_of(j * tk, tk)
        return lax.dot_general(qaug[g], kaug[g, pl.ds(st, tk), :], (((1,), (1,)), ((), ())),
                               preferred_element_type=F32)

    def update(g, s, j, rows=slice(0, tq), causal=False):
        st = pl.multiple_of(j * tk, tk)
        if causal:
            s = s + bias_scr[...]
        m_old = m_scr[g, rows]
        m_new = jnp.maximum(m_old, jnp.max(s, axis=-1, keepdims=True))
        p = jnp.exp2(s - m_new).astype(BF16)
        pv = jnp.dot(p, vaug[g, pl.ds(st, tk), :], preferred_element_type=F32)
        acc_scr[g, rows] = jnp.exp2(m_old - m_new) * acc_scr[g, rows] + pv
        m_scr[g, rows] = m_new

    heads = range(group)
    m_scr[...] = jnp.full(m_scr.shape, NEG_INF, F32)
    acc_scr[...] = jnp.zeros(acc_scr.shape, F32)
    for g in heads:
        s_a[g] = scores(g, 0)

    def body(jj, carry):
        for g in heads:
            s_b[g] = scores(g, 2 * jj + 1)
        for g in heads:
            update(g, s_a[g], 2 * jj)
        for g in heads:
            s_a[g] = scores(g, 2 * jj + 2)
        for g in heads:
            update(g, s_b[g], 2 * jj + 1)
        return carry

    lax.fori_loop(0, i, body, 0)
    upper, lower = slice(0, tk), slice(tk, tq)
    for g in heads:
        st = pl.multiple_of((2 * i + 1) * tk, tk)
        s_b[g, upper] = lax.dot_general(qaug[g, lower], kaug[g, pl.ds(st, tk), :],
                                        (((1,), (1,)), ((), ())), preferred_element_type=F32)
    for g in heads:
        update(g, s_a[g, upper], 2 * i, upper, causal=True)
        update(g, s_a[g, lower], 2 * i, lower)
    for g in heads:
        update(g, s_b[g, upper], 2 * i + 1, lower, causal=True)
    for g in heads:
        o_ref[:, head_cols(g)] = (acc_scr[g, :, 0:HEAD_DIM]
                                  / acc_scr[g, :, HEAD_DIM:2 * HEAD_DIM]).astype(o_ref.dtype)


def attention(proj, batch, seq, heads, q_col, k_col, v_col, cum=None, tq=1024, group=2):
    n = proj.shape[0]
    tq = _tile(seq, tq)
    tk = tq // 2
    nq = seq // tq
    mode = "moba" if cum is None else "fox"
    assert tk % MOBA_BLOCK == 0 and seq // MOBA_BLOCK <= LANES
    gw = group * HEAD_DIM
    assert heads % group == 0 and q_col % group == 0 and k_col % group == 0 and v_col % group == 0
    in_specs = [
        pl.BlockSpec((tq, gw), lambda b, h, i: (b * nq + i, q_col // group + h)),
        pl.BlockSpec((seq, gw), lambda b, h, i: (b, k_col // group + h)),
        pl.BlockSpec((seq, gw), lambda b, h, i: (b, v_col // group + h)),
    ]
    args = [proj, proj, proj]
    wide = 2 * HEAD_DIM
    scratch = [pltpu.VMEM((group, seq, wide), BF16), pltpu.VMEM((group, seq, wide), BF16),
               pltpu.VMEM((group, tq, wide), BF16),
               pltpu.VMEM((group, tq, tk), F32), pltpu.VMEM((group, tq, tk), F32),
               pltpu.VMEM((tk, tk), F32),
               pltpu.VMEM((group, tq, 1), F32), pltpu.VMEM((group, tq, wide), F32)]
    if mode == "fox":
        in_specs.append(pl.BlockSpec((None, heads, seq), lambda b, h, i: (b, 0, 0)))
        args.append(cum)
    else:
        scratch += [pltpu.VMEM((group, LANES, HEAD_DIM), F32),
                    pltpu.VMEM((group, 3 * (-(-(seq // MOBA_BLOCK) // 16) * 16), HEAD_DIM), BF16)]
    return pl.pallas_call(
        functools.partial(_attn_kernel, mode=mode, tq=tq, tk=tk, seq=seq),
        grid=(batch, heads // group, nq),
        in_specs=in_specs,
        out_specs=pl.BlockSpec((tq, gw), lambda b, h, i: (b * nq + i, h)),
        out_shape=jax.ShapeDtypeStruct((n, heads * HEAD_DIM), BF16),
        scratch_shapes=scratch,
        compiler_params=_params(("arbitrary", "arbitrary", "arbitrary")),
        name=mode + "_attention",
    )(*args)


def _out_proj_kernel(*refs, gate_row, sh_row, n_experts):
    if n_experts:
        (mo_ref, fo_ref, g1_ref, g2_ref, w_ref, x_ref, mod_ref, gf_ref, wr_ref,
         o_ref, h_ref, route_ref) = refs
    else:
        mo_ref, fo_ref, g1_ref, g2_ref, w_ref, x_ref, mod_ref, gf_ref, o_ref, h_ref = refs
    half = mo_ref.shape[1]
    m1 = _rms(mo_ref[...].astype(F32), g1_ref[...]).astype(BF16)
    m2 = _rms(fo_ref[...].astype(F32), g2_ref[...]).astype(BF16)
    acc = jnp.dot(m1, w_ref[0:half, :], preferred_element_type=F32)
    acc = acc + jnp.dot(m2, w_ref[half:2 * half, :], preferred_element_type=F32)
    x_new = x_ref[...] + mod_ref[gate_row:gate_row + 1, :] * acc
    o_ref[...] = x_new
    sh = mod_ref[sh_row:sh_row + 1, :]
    sc = mod_ref[sh_row + 1:sh_row + 2, :]
    h = _rms(x_new, gf_ref[...]) * (1.0 + sc) + sh
    h_hi = h.astype(BF16)
    h_ref[...] = h_hi
    if n_experts:
        h_lo = (h - h_hi.astype(F32)).astype(BF16)
        l_hi = jnp.dot(h_hi, wr_ref[...], preferred_element_type=F32)
        l_lo = jnp.dot(h_lo, wr_ref[:, 0:LANES], preferred_element_type=F32)
        logits = l_hi[:, 0:LANES] + l_hi[:, LANES:2 * LANES] + l_lo
        route_ref[...] = _top2_route(logits, n_experts)


def out_proj(moba_o, fox_o, g1, g2, w_o, x2, mod, ffn_g, seq, gate_row, sh_row, router_w=None, tm=512):
    n, d = x2.shape
    half = moba_o.shape[1]
    tm = _tile(seq, tm)
    per_b = seq // tm
    n_experts = 0 if router_w is None else router_w.shape[1]
    in_specs = [
        pl.BlockSpec((tm, half), lambda i: (i, 0)),
        pl.BlockSpec((tm, half), lambda i: (i, 0)),
        pl.BlockSpec((1, half), lambda i: (0, 0)),
        pl.BlockSpec((1, half), lambda i: (0, 0)),
        pl.BlockSpec((2 * half, d), lambda i: (0, 0)),
        pl.BlockSpec((tm, d), lambda i: (i, 0)),
        pl.BlockSpec((None, 6, d), lambda i: (i // per_b, 0, 0)),
        pl.BlockSpec((1, d), lambda i: (0, 0)),
    ]
    args = [moba_o, fox_o, g1.reshape(1, half), g2.reshape(1, half), w_o, x2, mod, ffn_g.reshape(1, d)]
    out_specs = [pl.BlockSpec((tm, d), lambda i: (i, 0)), pl.BlockSpec((tm, d), lambda i: (i, 0))]
    out_shape = [jax.ShapeDtypeStruct((n, d), F32), jax.ShapeDtypeStruct((n, d), BF16)]
    if n_experts:
        in_specs.append(pl.BlockSpec((d, 2 * LANES), lambda i: (0, 0)))
        w_hi = router_w.astype(BF16)
        w_lo = (router_w - w_hi.astype(F32)).astype(BF16)
        args.append(jnp.zeros((d, 2 * LANES), BF16).at[:, :n_experts].set(w_hi)
                    .at[:, LANES:LANES + n_experts].set(w_lo))
        out_specs.append(pl.BlockSpec((tm, LANES), lambda i: (i, 0)))
        out_shape.append(jax.ShapeDtypeStruct((n, LANES), F32))
    return pl.pallas_call(
        functools.partial(_out_proj_kernel, gate_row=gate_row, sh_row=sh_row, n_experts=n_experts),
        grid=(n // tm,),
        in_specs=in_specs,
        out_specs=out_specs,
        out_shape=out_shape,
        compiler_params=_params(("arbitrary",)),
        name="out_proj",
    )(*args)


def _swiglu_tile(h, wg, wu):
    a = jnp.dot(h, wg, preferred_element_type=F32)
    b = jnp.dot(h, wu, preferred_element_type=F32)
    return a * jax.nn.sigmoid(a) * b


def _ffn_kernel(h_ref, wg_ref, wu_ref, wd_ref, x_ref, mod_ref, o_ref, *, gate_row):
    f = pl.program_id(1)

    @pl.when(f == 0)
    def _():
        o_ref[...] = jnp.zeros(o_ref.shape, o_ref.dtype)

    t = _swiglu_tile(h_ref[...], wg_ref[...], wu_ref[...]).astype(BF16)
    o_ref[...] += jnp.dot(t, wd_ref[...], preferred_element_type=F32)

    @pl.when(f == pl.num_programs(1) - 1)
    def _():
        o_ref[...] = x_ref[...] + mod_ref[gate_row:gate_row + 1, :] * o_ref[...]


def ffn_dense(h, wg, wu, wd, x2, mod, seq, gate_row, tm=1024, tf=512):
    n, d = x2.shape
    ff = wg.shape[1]
    tm = _tile(seq, tm)
    tf = _tile(ff, tf)
    per_b = seq // tm
    return pl.pallas_call(
        functools.partial(_ffn_kernel, gate_row=gate_row),
        grid=(n // tm, ff // tf),
        in_specs=[
            pl.BlockSpec((tm, d), lambda i, f: (i, 0)),
            pl.BlockSpec((d, tf), lambda i, f: (0, f)),
            pl.BlockSpec((d, tf), lambda i, f: (0, f)),
            pl.BlockSpec((tf, d), lambda i, f: (f, 0)),
            pl.BlockSpec((tm, d), lambda i, f: (i, 0)),
            pl.BlockSpec((None, 6, d), lambda i, f: (i // per_b, 0, 0)),
        ],
        out_specs=pl.BlockSpec((tm, d), lambda i, f: (i, 0)),
        out_shape=jax.ShapeDtypeStruct((n, d), F32),
        compiler_params=_params(("arbitrary", "arbitrary")),
        name="ffn_dense",
    )(h, wg, wu, wd, x2, mod)


def _route_plan(route, n_experts, tm):
    n = route.shape[0]
    pairs = TOP_K * n
    e_flat = route[:, :TOP_K].astype(jnp.int32).reshape(pairs)
    onehot = (e_flat[:, None] == jnp.arange(n_experts, dtype=jnp.int32)[None, :]).astype(jnp.int32)
    csum = jnp.cumsum(onehot, axis=0)
    counts = csum[-1]
    padded = (counts + tm - 1) // tm * tm
    pend = jnp.cumsum(padded)
    pstart = pend - padded
    rank = jnp.sum(csum * onehot, axis=1) - 1
    pos = (jnp.sum(pstart[None, :] * onehot, axis=1) + rank).astype(jnp.int32)

    n_tiles = pairs // tm + n_experts
    first_row = jnp.arange(n_tiles, dtype=jnp.int32) * tm
    tile_expert = jnp.minimum(jnp.sum((pend[None, :] <= first_row[:, None]).astype(jnp.int32), axis=1),
                              n_experts - 1)
    valid = first_row < pend[-1]
    return pos, (tile_expert.astype(jnp.int32), valid.astype(jnp.int32),
                 (pstart + counts).astype(jnp.int32))


def _dispatch_kernel(pos_ref, h_ref, init_ref, xs_ref, sem):
    del init_ref
    i = pl.program_id(0)
    tm = h_ref.shape[0]

    def issue(r, carry):
        for k in range(TOP_K):
            dst = pos_ref[(i * tm + r) * TOP_K + k]
            pltpu.make_async_copy(h_ref.at[r], xs_ref.at[dst], sem).start(priority=k % 2)
        return carry

    lax.fori_loop(0, tm, issue, 0)
    for k in range(TOP_K):
        pltpu.make_async_copy(h_ref, xs_ref.at[pl.ds(0, tm)], sem).wait()


def moe_dispatch(h3, pos, rows, tm=2048):
    n, sub, lanes = h3.shape
    tm = _tile(n, tm)
    return pl.pallas_call(
        _dispatch_kernel,
        grid_spec=pltpu.PrefetchScalarGridSpec(
            num_scalar_prefetch=1,
            grid=(n // tm,),
            in_specs=[pl.BlockSpec((tm, sub, lanes), lambda i, pos: (i, 0, 0)),
                      pl.BlockSpec(memory_space=pl.ANY)],
            out_specs=pl.BlockSpec(memory_space=pl.ANY),
            scratch_shapes=[pltpu.SemaphoreType.DMA(())],
        ),
        out_shape=jax.ShapeDtypeStruct((rows, sub, lanes), h3.dtype),
        input_output_aliases={2: 0},
        compiler_params=_params(("arbitrary",), disable_bounds_checks=True),
        name="moe_dispatch",
    )(pos, h3, jnp.zeros((rows, sub, lanes), h3.dtype))


def _moe_ffn_kernel(exp_ref, valid_ref, rowend_ref, xs_ref, wg_ref, wu_ref, wd_ref, o_ref, *, sub):
    t = pl.program_id(0)
    f = pl.program_id(1)
    tm = xs_ref.shape[0]

    @pl.when(valid_ref[t] == 1)
    def _():
        rows = t * tm + lax.broadcasted_iota(jnp.int32, (tm, 1), 0)
        real = rows < rowend_ref[exp_ref[t]]

        @pl.when(f == 0)
        def _():
            o_ref[...] = jnp.zeros(o_ref.shape, o_ref.dtype)

        for c in range(wg_ref.shape[1] // sub):
            cols = slice(c * sub, (c + 1) * sub)
            a = _swiglu_tile(xs_ref[...], wg_ref[:, cols].astype(BF16), wu_ref[:, cols].astype(BF16))
            a = jnp.where(real, a, 0.0).astype(BF16)
            o_ref[...] += jnp.dot(a, wd_ref[cols, :].astype(BF16), preferred_element_type=F32)

    @pl.when(jnp.logical_and(valid_ref[t] == 0, f == 0))
    def _():
        o_ref[...] = jnp.zeros(o_ref.shape, o_ref.dtype)


def moe_ffn(xs, plan, wg, wu, wd, tm, tf=512, sub=256):
    rows, d = xs.shape
    n_experts, _, ff = wg.shape
    tf = _tile(ff, tf)
    sub = _tile(tf, sub)
    nf = ff // tf

    def f_idx(t, f, va):
        return jnp.where(va[t] == 1, f, nf - 1)

    return pl.pallas_call(
        functools.partial(_moe_ffn_kernel, sub=sub),
        grid_spec=pltpu.PrefetchScalarGridSpec(
            num_scalar_prefetch=3,
            grid=(rows // tm, nf),
            in_specs=[
                pl.BlockSpec((tm, d), lambda t, f, ex, va, re: (t, 0)),
                pl.BlockSpec((None, d, tf), lambda t, f, ex, va, re: (ex[t], 0, f_idx(t, f, va))),
                pl.BlockSpec((None, d, tf), lambda t, f, ex, va, re: (ex[t], 0, f_idx(t, f, va))),
                pl.BlockSpec((None, tf, d), lambda t, f, ex, va, re: (ex[t], f_idx(t, f, va), 0)),
            ],
            out_specs=pl.BlockSpec((tm, d), lambda t, f, ex, va, re: (t, 0)),
        ),
        out_shape=jax.ShapeDtypeStruct((rows, d), F32),
        compiler_params=_params(("arbitrary", "arbitrary")),
        name="moe_ffn",
    )(*plan, xs, wg, wu, wd)


def _combine_kernel(pos_ref, ys_ref, route_ref, x_ref, mod_ref, g_ref, o_ref, buf, sem,
                    *, gate_row, final):
    i = pl.program_id(0)
    tm = x_ref.shape[0]
    slot = i % 2

    def gather(step, into):
        def issue(r, carry):
            for k in range(TOP_K):
                src = pos_ref[(step * tm + r) * TOP_K + k]
                pltpu.make_async_copy(ys_ref.at[pl.ds(src, 1), :], buf.at[into, k, pl.ds(r, 1), :],
                                      sem.at[into]).start(priority=k % 2)
            return carry

        lax.fori_loop(0, tm, issue, 0, unroll=2)

    @pl.when(i == 0)
    def _():
        gather(0, 0)

    more = i + 1 < pl.num_programs(0)
    for parity in range(2):
        @pl.when(jnp.logical_and(more, slot == parity))
        def _(parity=parity):
            gather(i + 1, 1 - parity)

    for k in range(TOP_K):
        pltpu.make_async_copy(ys_ref.at[pl.ds(0, tm), :], buf.at[slot, k], sem.at[slot]).wait()
    route = route_ref[...]
    y = buf[slot, 0] * route[:, TOP_K:TOP_K + 1]
    for k in range(1, TOP_K):
        y = y + buf[slot, k] * route[:, TOP_K + k:TOP_K + k + 1]
    x_new = x_ref[...] + mod_ref[gate_row:gate_row + 1, :] * y
    o_ref[...] = _rms(x_new, g_ref[...]) if final else x_new


def moe_combine(ys, pos, route, x2, mod, seq, gate_row, final_g=None, tm=512):
    n, d = x2.shape
    tm = _tile(seq, tm)
    per_b = seq // tm
    g = jnp.ones((1, d), F32) if final_g is None else final_g.reshape(1, d)
    return pl.pallas_call(
        functools.partial(_combine_kernel, gate_row=gate_row, final=final_g is not None),
        grid_spec=pltpu.PrefetchScalarGridSpec(
            num_scalar_prefetch=1,
            grid=(n // tm,),
            in_specs=[
                pl.BlockSpec(memory_space=pl.ANY),
                pl.BlockSpec((tm, LANES), lambda i, pos: (i, 0)),
                pl.BlockSpec((tm, d), lambda i, pos: (i, 0)),
                pl.BlockSpec((None, 6, d), lambda i, pos: (i // per_b, 0, 0)),
                pl.BlockSpec((1, d), lambda i, pos: (0, 0)),
            ],
            out_specs=pl.BlockSpec((tm, d), lambda i, pos: (i, 0)),
            scratch_shapes=[pltpu.VMEM((2, TOP_K, tm, d), F32), pltpu.SemaphoreType.DMA((2,))],
        ),
        out_shape=jax.ShapeDtypeStruct((n, d), F32),
        compiler_params=_params(("arbitrary",), disable_bounds_checks=True),
        name="moe_combine",
    )(pos, ys, route, x2, mod, g)


def ffn_moe(h, route, wg, wu, wd, x2, mod, seq, gate_row, final_g=None, tm=1024):
    n, d = x2.shape
    n_experts = wg.shape[0]
    tm = _tile(TOP_K * n, tm)
    pos, plan = _route_plan(route, n_experts, tm)
    rows = TOP_K * n + n_experts * tm
    xs3 = moe_dispatch(h.reshape(n, d // LANES, LANES), pos, rows)
    ys = moe_ffn(xs3.reshape(rows, d), plan, wg, wu, wd, tm)
    return moe_combine(ys, pos, route, x2, mod, seq, gate_row, final_g)


def _rope_tables(seq):
    f32 = np.float32
    inv = np.power(f32(ROPE_THETA), -np.arange(0, HEAD_DIM, 2, dtype=f32) / f32(HEAD_DIM)).astype(f32)
    ang = np.arange(seq, dtype=f32)[:, None] * inv[None, :]
    cos, sin = np.cos(ang).astype(f32), np.sin(ang).astype(f32)
    return (jnp.asarray(np.concatenate([cos, cos], axis=-1)),
            jnp.asarray(np.concatenate([-sin, sin], axis=-1)))


def kernel(x, c, attn_norm_g, ada_w, ada_b, w_in, forget_b, moba_out_g, fox_out_g, w_o, ffn_norm_g,
           dense_w_gate, dense_w_up, dense_w_down, router_w, moe_w_gate, moe_w_up, moe_w_down,
           final_norm_g):
    batch, seq, d = x.shape
    depth = attn_norm_g.shape[0]
    moba_w = moba_out_g.shape[1]
    fox_w = fox_out_g.shape[1]
    moba_h = moba_w // HEAD_DIM
    fox_h = fox_w // HEAD_DIM
    qkv_w = 3 * moba_w + 3 * fox_w
    assert moba_w == fox_w
    kinds = (ROPE_SCALED, ROPE, PLAIN, SCALED, PLAIN, PLAIN)
    hb = HEAD_DIM

    cos_t, sin_t = _rope_tables(seq)
    mods = ada_modulation(c, ada_w, ada_b)
    x2 = x.reshape(batch * seq, d)
    out = None

    for l in range(depth):
        mod = mods[l]
        w_qkv = w_in[l][:, :qkv_w].astype(BF16)
        wz = jnp.zeros((d, LANES), BF16).at[:, :fox_h].set(w_in[l][:, qkv_w:].astype(BF16))
        proj, fz = in_proj(x2, attn_norm_g[l], mod, w_qkv, wz, cos_t, sin_t, seq, kinds,
                           unit=moba_w, sh_row=0)
        cum = logf_cumsum(fz[:, :fox_h].reshape(batch, seq, fox_h), forget_b[l])
        cum = cum.transpose(0, 2, 1)
        moba_o = attention(proj, batch, seq, moba_h, q_col=0, k_col=moba_w // hb,
                           v_col=2 * moba_w // hb)
        fc = 3 * moba_w // hb
        fox_o = attention(proj, batch, seq, fox_h, q_col=fc, k_col=fc + fox_w // hb,
                          v_col=fc + 2 * fox_w // hb, cum=cum)
        i = l // 2
        last = l == depth - 1
        w_o_l = w_o[l].astype(BF16)
        if l % 2 == 0:
            x2, h = out_proj(moba_o, fox_o, moba_out_g[l], fox_out_g[l], w_o_l, x2, mod,
                             ffn_norm_g[l], seq, gate_row=2, sh_row=3)
            x2 = ffn_dense(h, dense_w_gate[i].astype(BF16), dense_w_up[i].astype(BF16),
                           dense_w_down[i].astype(BF16), x2, mod, seq, gate_row=5)
        else:
            x2, h, route = out_proj(moba_o, fox_o, moba_out_g[l], fox_out_g[l], w_o_l, x2, mod,
                                    ffn_norm_g[l], seq, gate_row=2, sh_row=3, router_w=router_w[i])
            x2 = ffn_moe(h, route, moe_w_gate[i], moe_w_up[i], moe_w_down[i], x2, mod, seq,
                         gate_row=5, final_g=final_norm_g if last else None)
            if last:
                out = x2
    if out is None:
        out = final_norm(x2, final_norm_g)
    return out.reshape(batch, seq, d)
```
